```python
import math
import jax, jax.numpy as jnp
from jax import lax
import numpy as np

D_MODEL = 1024
BATCH = 8
SEQ = 4096
DEPTH = 1

HEAD_DIM = 64
N_HEADS_A = 8
N_KV_A = 2
GROUP_A = N_HEADS_A // N_KV_A
WINDOW_A = 128
N_HEADS_B = 8
DILATED_PATTERNS = ((128, 1), (512, 4), (2048, 16))
WIDTH_A = N_HEADS_A * HEAD_DIM
WIDTH_B = N_HEADS_B * HEAD_DIM
D_MIX = WIDTH_A + WIDTH_B
KV_WIDTH_A = N_KV_A * HEAD_DIM
D_IN_PROJ = WIDTH_A + 2 * KV_WIDTH_A + 3 * WIDTH_B
N_BIAS_HEADS = N_HEADS_A + N_HEADS_B
NUM_BUCKETS = 32
MAX_DISTANCE = 1024
D_FF = 4 * D_MODEL
PLE_DIM = 256
EPS = 1e-6
NEG = -1e30

kernel_name = "hybrid_wingqa_dilated_sandwich_layer"


def rmsnorm(x, g):
    xf = x.astype(jnp.float32)
    y = xf * lax.rsqrt(jnp.mean(xf * xf, axis=-1, keepdims=True) + EPS)
    return (y * g.astype(jnp.float32)).astype(x.dtype)


def t5_bucket(rel):
    half = NUM_BUCKETS // 2
    max_exact = half // 2
    sign = jnp.where(rel > 0, half, 0)
    n = jnp.abs(rel)
    nf = jnp.maximum(n, 1).astype(jnp.float32)
    large = max_exact + (jnp.log(nf / max_exact) / math.log(MAX_DISTANCE / max_exact)
                         * (half - max_exact)).astype(jnp.int32)
    large = jnp.minimum(large, half - 1)
    return sign + jnp.where(n < max_exact, n, large)


def band_rel(block):
    qi = jnp.arange(block)[:, None]
    ki = jnp.arange(3 * block)[None, :]
    return ki - block - qi


def banded_attention(q, k, v, bias, half_window, block, sink):
    b_, hk, g, L, dh = q.shape
    nb = -(-L // block)
    lp = nb * block
    q = jnp.pad(q, ((0, 0), (0, 0), (0, 0), (0, lp - L), (0, 0)))
    kv_pad = ((0, 0), (0, 0), (block, lp - L + block), (0, 0))
    kp = jnp.pad(k, kv_pad).reshape(b_, hk, nb + 2, block, dh)
    vp = jnp.pad(v, kv_pad).reshape(b_, hk, nb + 2, block, dh)
    kw = jnp.concatenate([kp[:, :, :-2], kp[:, :, 1:-1], kp[:, :, 2:]], axis=3)
    vw = jnp.concatenate([vp[:, :, :-2], vp[:, :, 1:-1], vp[:, :, 2:]], axis=3)
    qb = q.reshape(b_, hk, g, nb, block, dh)
    s = jnp.einsum('bhgnqd,bhnkd->bhgnqk', qb, kw).astype(jnp.float32) * (dh ** -0.5)
    s = s + bias[:, :, None]
    rel = band_rel(block)
    kpos = jnp.arange(nb)[:, None, None] * block + jnp.arange(3 * block)[None, None, :] - block
    valid = (jnp.abs(rel) <= half_window)[None] & (kpos >= 0) & (kpos < L)
    s = jnp.where(valid, s, NEG)
    m = jnp.max(s, axis=-1, keepdims=True)
    if sink is not None:
        sinkb = sink.astype(jnp.float32)[None, :, :, None, None, None]
        m = jnp.maximum(m, sinkb)
    e = jnp.exp(s - m)
    denom = jnp.sum(e, axis=-1, keepdims=True)
    if sink is not None:
        denom = denom + jnp.exp(sinkb - m)
    o = jnp.einsum('bhgnqk,bhnkd->bhgnqd', e, vw.astype(jnp.float32)) / denom
    lse = (jnp.log(denom) + m)[..., 0]
    o = o.reshape(b_, hk, g, lp, dh)[:, :, :, :L].astype(k.dtype)
    lse = lse.reshape(b_, hk, g, lp)[..., :L]
    return o, lse


def windowed_gqa_sink(qa, ka, va, bias_table, sink):
    b_, s_, _ = qa.shape
    q = qa.reshape(b_, s_, N_KV_A, GROUP_A, HEAD_DIM).transpose(0, 2, 3, 1, 4)
    k = ka.reshape(b_, s_, N_KV_A, HEAD_DIM).transpose(0, 2, 1, 3)
    v = va.reshape(b_, s_, N_KV_A, HEAD_DIM).transpose(0, 2, 1, 3)
    bias = bias_table[t5_bucket(band_rel(WINDOW_A))][..., :N_HEADS_A]
    bias = bias.transpose(2, 0, 1).reshape(N_KV_A, GROUP_A, WINDOW_A, 3 * WINDOW_A)
    o, _ = banded_attention(q, k, v, bias.astype(jnp.float32), WINDOW_A, WINDOW_A,
                            sink.reshape(N_KV_A, GROUP_A))
    return o.transpose(0, 3, 1, 2, 4).reshape(b_, s_, WIDTH_A)


def dilated_mixture(qb_, kb_, vb_, bias_table):
    b_, s_, _ = qb_.shape
    to_heads = lambda t: t.reshape(b_, s_, N_HEADS_B, HEAD_DIM).transpose(0, 2, 1, 3)
    q, k, v = to_heads(qb_), to_heads(kb_), to_heads(vb_)
    outs, lses = [], []
    for window, dil in DILATED_PATTERNS:
        half = window // (2 * dil)
        ls = s_ // dil
        sub = lambda t: t.reshape(b_, N_HEADS_B, ls, dil, HEAD_DIM).transpose(0, 1, 3, 2, 4) \
                         .reshape(b_, N_HEADS_B * dil, ls, HEAD_DIM)
        bias = bias_table[t5_bucket(band_rel(half) * dil)][..., N_HEADS_A:]
        bias = jnp.repeat(bias.transpose(2, 0, 1), dil, axis=0)[:, None]
        o, lse = banded_attention(sub(q)[:, :, None], sub(k), sub(v),
                                  bias.astype(jnp.float32), half, half, None)
        o = o.reshape(b_, N_HEADS_B, dil, ls, HEAD_DIM).transpose(0, 1, 3, 2, 4) \
             .reshape(b_, N_HEADS_B, s_, HEAD_DIM)
        lse = lse.reshape(b_, N_HEADS_B, dil, ls).transpose(0, 1, 3, 2).reshape(b_, N_HEADS_B, s_)
        outs.append(o)
        lses.append(lse)
    w = jax.nn.softmax(jnp.stack(lses, axis=0), axis=0)
    o = jnp.sum(w[..., None] * jnp.stack(outs, axis=0).astype(jnp.float32), axis=0).astype(q.dtype)
    return o.transpose(0, 2, 1, 3).reshape(b_, s_, WIDTH_B)


def setup_inputs(seed: int = 0) -> dict:
    key = jax.random.key(seed)
    ks = jax.random.split(key, 20)
    nrm = lambda k, shape, scale: (jax.random.normal(k, shape, jnp.float32) * scale)
    gain = lambda k, n: 1.0 + nrm(k, (DEPTH, n), 0.02)
    return {
        "x": nrm(ks[0], (BATCH, SEQ, D_MODEL), 1.0),
        "p": nrm(ks[1], (DEPTH, BATCH, SEQ, PLE_DIM), 1.0),
        "rel_bias_table": nrm(ks[2], (NUM_BUCKETS, N_BIAS_HEADS), 0.5),
        "g_pre_mix": gain(ks[3], D_MODEL),
        "w_in": nrm(ks[4], (DEPTH, D_MODEL, D_IN_PROJ), D_MODEL ** -0.5),
        "sink_a": nrm(ks[5], (DEPTH, N_HEADS_A), 0.5),
        "g_out_a": gain(ks[6], WIDTH_A),
        "g_out_b": gain(ks[7], WIDTH_B),
        "w_o": nrm(ks[8], (DEPTH, D_MIX, D_MODEL), D_MIX ** -0.5),
        "g_post_mix": gain(ks[9], D_MODEL),
        "g_pre_mlp": gain(ks[10], D_MODEL),
        "w_up": nrm(ks[11], (DEPTH, D_MODEL, D_FF), D_MODEL ** -0.5),
        "w_down": nrm(ks[12], (DEPTH, D_FF, D_MODEL), D_FF ** -0.5),
        "g_post_mlp": gain(ks[13], D_MODEL),
        "w_ple_proj": nrm(ks[14], (DEPTH, PLE_DIM, D_MODEL), PLE_DIM ** -0.5),
        "w_ple_gate": nrm(ks[15], (DEPTH, D_MODEL, D_MODEL), D_MODEL ** -0.5),
        "b_ple_gate": nrm(ks[16], (DEPTH, D_MODEL), 0.02),
        "g_post_ple": gain(ks[17], D_MODEL),
    }


def reference(x, p, rel_bias_table, g_pre_mix, w_in, sink_a, g_out_a, g_out_b, w_o,
              g_post_mix, g_pre_mlp, w_up, w_down, g_post_mlp, w_ple_proj, w_ple_gate,
              b_ple_gate, g_post_ple):
    h = x
    offs = np.cumsum([0, WIDTH_A, KV_WIDTH_A, KV_WIDTH_A, WIDTH_B, WIDTH_B, WIDTH_B])
    for i in range(DEPTH):
        u = rmsnorm(h, g_pre_mix[i])
        proj = jnp.einsum('bsd,de->bse', u, w_in[i])
        qa, ka, va, qb, kb, vb = [proj[..., offs[j]:offs[j + 1]] for j in range(6)]
        o_a = rmsnorm(windowed_gqa_sink(qa, ka, va, rel_bias_table, sink_a[i]), g_out_a[i])
        o_b = rmsnorm(dilated_mixture(qb, kb, vb, rel_bias_table), g_out_b[i])
        mix = jnp.einsum('bse,ed->bsd', jnp.concatenate([o_a, o_b], axis=-1), w_o[i])
        h = h + rmsnorm(mix, g_post_mix[i])
        v_ = rmsnorm(h, g_pre_mlp[i])
        a = jax.nn.relu(jnp.einsum('bsd,df->bsf', v_, w_up[i]))
        ff = jnp.einsum('bsf,fd->bsd', a * a, w_down[i])
        h = h + rmsnorm(ff, g_post_mlp[i])
        gate = jax.nn.sigmoid(jnp.einsum('bsd,de->bse', h, w_ple_gate[i]) + b_ple_gate[i])
        ple = jnp.einsum('bsk,kd->bsd', p[i], w_ple_proj[i])
        h = h + rmsnorm(gate * ple, g_post_ple[i])
    return h
```

```python
import functools
import math

import jax
import jax.numpy as jnp
import numpy as np
from jax import lax
from jax.experimental import pallas as pl
from jax.experimental.pallas import tpu as pltpu

HEAD_DIM = 64
N_HEADS_A = 8
N_KV_A = 2
GROUP_A = N_HEADS_A // N_KV_A
WINDOW_A = 128
N_HEADS_B = 8
DILATED_PATTERNS = ((128, 1), (512, 4), (2048, 16))
WIDTH_A = N_HEADS_A * HEAD_DIM
WIDTH_B = N_HEADS_B * HEAD_DIM
KV_WIDTH_A = N_KV_A * HEAD_DIM
NUM_BUCKETS = 32
MAX_DISTANCE = 1024
EPS = 1e-6
NEG = -1e30

LANES = 128
F32 = jnp.float32
BF16 = jnp.bfloat16

_VMEM_LIMIT = 56 * 1024 * 1024


def _rms(x, g):
    ms = jnp.mean(x * x, axis=-1, keepdims=True)
    return (x * lax.rsqrt(ms + EPS)) * g


def _t5_bucket_np(rel):
    half = NUM_BUCKETS // 2
    max_exact = half // 2
    sign = np.where(rel > 0, half, 0)
    n = np.abs(rel)
    nf = np.maximum(n, 1).astype(np.float32)
    large = max_exact + (np.log(nf / np.float32(max_exact))
                         / np.float32(math.log(MAX_DISTANCE / max_exact))
                         * np.float32(half - max_exact)).astype(np.int32)
    large = np.minimum(large, half - 1)
    return (sign + np.where(n < max_exact, n, large)).astype(np.int32)


def _bias_matrices(table_t, rel_sub, dil, half_window):
    idx = _t5_bucket_np(rel_sub * dil)
    valid = np.abs(rel_sub) <= half_window
    vals = jnp.take(table_t, jnp.asarray(idx.reshape(-1)), axis=1)
    vals = vals.reshape((table_t.shape[0],) + rel_sub.shape)
    return jnp.where(jnp.asarray(valid)[None], vals, NEG).astype(F32)


def _inproj_kernel(x_ref, g_ref, w_ref, qa_ref, ka_ref, va_ref, qb_ref, kb_ref, vb_ref):
    u = _rms(x_ref[...], g_ref[...]).astype(BF16)
    col = 0
    for o_ref in (qa_ref, ka_ref, va_ref, qb_ref, kb_ref, vb_ref):
        width = o_ref.shape[-1]
        o_ref[...] = jnp.dot(u, w_ref[:, col:col + width],
                             preferred_element_type=F32).astype(BF16)
        col += width


def _inproj(x2d, g, w, tm):
    n, d = x2d.shape
    widths = (WIDTH_A, KV_WIDTH_A, KV_WIDTH_A, WIDTH_B, WIDTH_B, WIDTH_B)
    return pl.pallas_call(
        _inproj_kernel,
        grid=(n // tm,),
        in_specs=[
            pl.BlockSpec((tm, d), lambda i: (i, 0)),
            pl.BlockSpec((1, d), lambda i: (0, 0)),
            pl.BlockSpec(w.shape, lambda i: (0, 0)),
        ],
        out_specs=[pl.BlockSpec((tm, wd), lambda i: (i, 0)) for wd in widths],
        out_shape=[jax.ShapeDtypeStruct((n, wd), BF16) for wd in widths],
        compiler_params=pltpu.CompilerParams(
            dimension_semantics=("parallel",), vmem_limit_bytes=_VMEM_LIMIT),
        name="inproj",
    )(x2d, g, w)


def _nt_dot(a, b):
    return lax.dot_general(a, b, (((1,), (1,)), ((), ())), preferred_element_type=F32)


def _head_softmax_pv(qm, k_win, v_aug, bias, sink):
    s = _nt_dot(qm, k_win) + bias
    m = jnp.max(s, axis=-1, keepdims=True)
    if sink is not None:
        m = jnp.maximum(m, sink)
    e = jnp.exp(s - m)
    o = jnp.dot(e.astype(BF16), v_aug, preferred_element_type=F32)
    num = o[:, :LANES]
    den = o[:, LANES:]
    if sink is not None:
        den = den + jnp.exp(sink - m)
    return num, den, m


def _win_kernel(sink_ref, q_ref, k_ref, v_ref, bias_ref, o_ref, *, seq, blk):
    nblk = seq // blk
    lane = lax.broadcasted_iota(jnp.int32, (blk, LANES), 1)
    low = lane < HEAD_DIM

    def block(q0, k0, klen, bcol):
        k_win = k_ref[pl.ds(k0, klen), :]
        v_win = v_ref[pl.ds(k0, klen), :]
        v_aug = jnp.concatenate([v_win, jnp.ones_like(v_win)], axis=1)
        for j in range(GROUP_A):
            q_tile = q_ref[pl.ds(q0, blk), j * LANES:(j + 1) * LANES]
            res = []
            for hf in range(N_KV_A):
                h = hf * GROUP_A + j
                qm = jnp.where(low if hf == 0 else ~low, q_tile, jnp.zeros_like(q_tile))
                num, den, _ = _head_softmax_pv(
                    qm, k_win, v_aug, bias_ref[h, :, bcol:bcol + klen], sink_ref[h])
                res.append(num / den)
            o_ref[pl.ds(q0, blk), j * LANES:(j + 1) * LANES] = (
                jnp.where(low, res[0], res[1]).astype(o_ref.dtype))

    block(0, 0, 2 * blk, blk)

    def body(n, c):
        q0 = pl.multiple_of(n * blk, blk)
        block(q0, pl.multiple_of(q0 - blk, blk), 3 * blk, 0)
        return c

    lax.fori_loop(1, nblk - 1, body, 0)
    block(seq - blk, seq - 2 * blk, 2 * blk, 0)


def _windowed(qa, ka, va, bias, sink):
    b, s, _ = qa.shape
    blk = WINDOW_A
    assert s % blk == 0 and s >= 2 * blk
    kern = functools.partial(_win_kernel, seq=s, blk=blk)
    return pl.pallas_call(
        kern,
        grid=(b,),
        in_specs=[
            pl.BlockSpec(memory_space=pltpu.SMEM),
            pl.BlockSpec((None, s, WIDTH_A), lambda i: (i, 0, 0)),
            pl.BlockSpec((None, s, KV_WIDTH_A), lambda i: (i, 0, 0)),
            pl.BlockSpec((None, s, KV_WIDTH_A), lambda i: (i, 0, 0)),
            pl.BlockSpec(bias.shape, lambda i: (0, 0, 0)),
        ],
        out_specs=pl.BlockSpec((None, s, WIDTH_A), lambda i: (i, 0, 0)),
        out_shape=jax.ShapeDtypeStruct((b, s, WIDTH_A), BF16),
        compiler_params=pltpu.CompilerParams(
            dimension_semantics=("parallel",), vmem_limit_bytes=_VMEM_LIMIT),
        name="win_gqa",
    )(sink, qa, ka, va, bias)


def _dil_kernel(q_ref, k_ref, v_ref, bias_ref, o_ref, lse_ref, *, ls, half):
    qb = 2 * half
    kw = 4 * half
    nblk = ls // qb
    lane = lax.broadcasted_iota(jnp.int32, (qb, LANES), 1)
    low = lane < HEAD_DIM

    def block(q0, k0, variant):
        lse_acc = jnp.zeros((qb, LANES), F32)
        for j in range(N_HEADS_B // 2):
            cols = slice(j * LANES, (j + 1) * LANES)
            k_win = k_ref[pl.ds(k0, kw), cols]
            v_win = v_ref[pl.ds(k0, kw), cols]
            v_aug = jnp.concatenate([v_win, jnp.ones_like(v_win)], axis=1)
            q_tile = q_ref[pl.ds(q0, qb), cols]
            res = []
            for hf in range(2):
                h = 2 * j + hf
                qm = jnp.where(low if hf == 0 else ~low, q_tile, jnp.zeros_like(q_tile))
                num, den, m = _head_softmax_pv(qm, k_win, v_aug, bias_ref[variant, h], None)
                res.append(num / den)
                lse_acc = jnp.where(lane == h, jnp.log(den) + m, lse_acc)
            o_ref[pl.ds(q0, qb), cols] = jnp.where(low, res[0], res[1]).astype(o_ref.dtype)
        lse_ref[pl.ds(q0, qb), :] = lse_acc

    block(0, 0, 0)
    if nblk > 2:
        def body(n, c):
            q0 = pl.multiple_of(n * qb, qb)
            block(q0, pl.multiple_of(q0 - half, half), 1)
            return c
        lax.fori_loop(1, nblk - 1, body, 0)
    block(ls - qb, ls - kw, 2)


def _dilated(qb, kb, vb, bias, dil, half):
    b, s, w = qb.shape
    ls = s // dil
    assert s % dil == 0 and ls % (2 * half) == 0 and ls >= 4 * half
    view = lambda t: t.reshape(b, ls, dil * w)
    kern = functools.partial(_dil_kernel, ls=ls, half=half)
    spec = pl.BlockSpec((None, ls, w), lambda i, r: (i, 0, r))
    o, lse = pl.pallas_call(
        kern,
        grid=(b, dil),
        in_specs=[spec, spec, spec,
                  pl.BlockSpec(bias.shape, lambda i, r: (0, 0, 0, 0))],
        out_specs=[spec, pl.BlockSpec((None, ls, LANES), lambda i, r: (i, 0, r))],
        out_shape=[jax.ShapeDtypeStruct((b, ls, dil * w), BF16),
                   jax.ShapeDtypeStruct((b, ls, dil * LANES), F32)],
        compiler_params=pltpu.CompilerParams(
            dimension_semantics=("parallel", "parallel"), vmem_limit_bytes=_VMEM_LIMIT),
        name=f"dilated_d{dil}",
    )(view(qb), view(kb), view(vb), bias)
    return o.reshape(b, s, w), lse.reshape(b, s, LANES)


def _post_kernel(x_ref, p_ref, oa_ref, ob1_ref, ob2_ref, ob3_ref, l1_ref, l2_ref, l3_ref,
                 expand_ref, goa_ref, gob_ref, wo_ref, gpm_ref, gpre_ref, wup_ref, wdn_ref,
                 gmlp_ref, wpp_ref, wpg_ref, bpg_ref, gple_ref, out_ref, *, ff_chunk):
    l1, l2, l3 = l1_ref[...], l2_ref[...], l3_ref[...]
    mx = jnp.maximum(jnp.maximum(l1, l2), l3)
    e1, e2, e3 = jnp.exp(l1 - mx), jnp.exp(l2 - mx), jnp.exp(l3 - mx)
    tot = e1 + e2 + e3
    ob = None
    for e, o_ref in ((e1, ob1_ref), (e2, ob2_ref), (e3, ob3_ref)):
        w = e / tot
        hi = w.astype(BF16)
        lo = (w - hi.astype(F32)).astype(BF16)
        wb = jnp.dot(jnp.concatenate([hi, lo], axis=1), expand_ref[...],
                     preferred_element_type=F32)
        term = wb * o_ref[...].astype(F32)
        ob = term if ob is None else ob + term

    oa = _rms(oa_ref[...].astype(F32), goa_ref[...]).astype(BF16)
    ob = _rms(ob, gob_ref[...]).astype(BF16)
    mix = (jnp.dot(oa, wo_ref[:WIDTH_A, :], preferred_element_type=F32)
           + jnp.dot(ob, wo_ref[WIDTH_A:, :], preferred_element_type=F32))
    h = x_ref[...] + _rms(mix, gpm_ref[...])

    v = _rms(h, gpre_ref[...]).astype(BF16)
    d_ff = wup_ref.shape[1]
    ff = None
    for c in range(0, d_ff, ff_chunk):
        a = jnp.maximum(jnp.dot(v, wup_ref[:, c:c + ff_chunk], preferred_element_type=F32), 0.0)
        t = jnp.dot((a * a).astype(BF16), wdn_ref[c:c + ff_chunk, :], preferred_element_type=F32)
        ff = t if ff is None else ff + t
    h = h + _rms(ff, gmlp_ref[...])

    gate = jax.nn.sigmoid(
        jnp.dot(h.astype(BF16), wpg_ref[...], preferred_element_type=F32) + bpg_ref[...])
    ple = jnp.dot(p_ref[...].astype(BF16), wpp_ref[...], preferred_element_type=F32)
    out_ref[...] = h + _rms(gate * ple, gple_ref[...])


def _post(x2d, p2d, oa, obs, lses, expand, goa, gob, wo, gpm, gpre, wup, wdn, gmlp,
          wpp, wpg, bpg, gple, tm, ff_chunk):
    n, d = x2d.shape
    row = lambda width: pl.BlockSpec((tm, width), lambda i: (i, 0))
    const = lambda a: pl.BlockSpec(a.shape, lambda i: (0,) * a.ndim,
                                   pipeline_mode=pl.Buffered(1))
    consts = (expand, goa, gob, wo, gpm, gpre, wup, wdn, gmlp, wpp, wpg, bpg, gple)
    return pl.pallas_call(
        functools.partial(_post_kernel, ff_chunk=ff_chunk),
        grid=(n // tm,),
        in_specs=[row(d), row(p2d.shape[1]), row(WIDTH_A)]
                 + [row(WIDTH_B)] * 3 + [row(LANES)] * 3
                 + [const(a) for a in consts],
        out_specs=row(d),
        out_shape=jax.ShapeDtypeStruct((n, d), x2d.dtype),
        compiler_params=pltpu.CompilerParams(
            dimension_semantics=("parallel",), vmem_limit_bytes=_VMEM_LIMIT),
        name="post",
    )(x2d, p2d, oa, *obs, *lses, *consts)


def kernel(x, p, rel_bias_table, g_pre_mix, w_in, sink_a, g_out_a, g_out_b, w_o, g_post_mix,
           g_pre_mlp, w_up, w_down, g_post_mlp, w_ple_proj, w_ple_gate, b_ple_gate, g_post_ple):
    b, s, d = x.shape
    depth = w_in.shape[0]
    n = b * s
    tm = 256

    perm_heads = np.arange(N_HEADS_A).reshape(N_KV_A, GROUP_A).T.reshape(-1)
    perm_cols = (perm_heads[:, None] * HEAD_DIM + np.arange(HEAD_DIM)[None, :]).reshape(-1)

    table_t = rel_bias_table.T.astype(F32)
    qi = np.arange(WINDOW_A)[:, None]
    ki = np.arange(3 * WINDOW_A)[None, :]
    bias_a = _bias_matrices(table_t[:N_HEADS_A], ki - WINDOW_A - qi, 1, WINDOW_A)
    bias_b = []
    for window, dil in DILATED_PATTERNS:
        half = window // (2 * dil)
        qi = np.arange(2 * half)[:, None]
        ki = np.arange(4 * half)[None, :]
        rel = np.stack([ki - qi, ki - half - qi, ki - 2 * half - qi])
        bias_b.append(jnp.swapaxes(
            _bias_matrices(table_t[N_HEADS_A:], rel, dil, half), 0, 1))

    expand_np = np.zeros((2 * LANES, WIDTH_B), np.float32)
    for hh in range(N_HEADS_B):
        expand_np[hh, hh * HEAD_DIM:(hh + 1) * HEAD_DIM] = 1.0
        expand_np[LANES + hh, hh * HEAD_DIM:(hh + 1) * HEAD_DIM] = 1.0
    expand = jnp.asarray(expand_np, BF16)

    h2d = x.reshape(n, d)
    row = lambda a: a.reshape(1, -1).astype(F32)
    for i in range(depth):
        scale = HEAD_DIM ** -0.5
        w = w_in[i]
        o0 = WIDTH_A + 2 * KV_WIDTH_A
        w_all = jnp.concatenate([
            w[:, :WIDTH_A][:, perm_cols] * scale,
            w[:, WIDTH_A:o0],
            w[:, o0:o0 + WIDTH_B] * scale,
            w[:, o0 + WIDTH_B:],
        ], axis=1).astype(BF16)
        qa, ka, va, qb, kb, vb = _inproj(h2d, row(g_pre_mix[i]), w_all, tm)
        r3 = lambda t: t.reshape(b, s, t.shape[-1])

        o_a = _windowed(r3(qa), r3(ka), r3(va), bias_a, sink_a[i].astype(F32))
        obs, lses = [], []
        for (window, dil), bias in zip(DILATED_PATTERNS, bias_b):
            o, lse = _dilated(r3(qb), r3(kb), r3(vb), bias, dil, window // (2 * dil))
            obs.append(o.reshape(n, WIDTH_B))
            lses.append(lse.reshape(n, LANES))

        wo = jnp.concatenate([w_o[i][:WIDTH_A][perm_cols], w_o[i][WIDTH_A:]], axis=0).astype(BF16)
        h2d = _post(
            h2d, p[i].reshape(n, -1), o_a.reshape(n, WIDTH_A), obs, lses, expand,
            row(g_out_a[i][perm_cols]), row(g_out_b[i]), wo, row(g_post_mix[i]),
            row(g_pre_mlp[i]), w_up[i].astype(BF16), w_down[i].astype(BF16),
            row(g_post_mlp[i]), w_ple_proj[i].astype(BF16), w_ple_gate[i].astype(BF16),
            row(b_ple_gate[i]), row(g_post_ple[i]), tm, 1024)
    return h2d.reshape(b, s, d)
```

```python
import functools
import math

import jax
import jax.numpy as jnp
import numpy as np
from jax import lax
from jax.experimental import pallas as pl
from jax.experimental.pallas import tpu as pltpu

HEAD_DIM = 64
N_HEADS_A = 8
N_KV_A = 2
GROUP_A = N_HEADS_A // N_KV_A
WINDOW_A = 128
N_HEADS_B = 8
DILATED_PATTERNS = ((128, 1), (512, 4), (2048, 16))
WIDTH_A = N_HEADS_A * HEAD_DIM
WIDTH_B = N_HEADS_B * HEAD_DIM
KV_WIDTH_A = N_KV_A * HEAD_DIM
NUM_BUCKETS = 32
MAX_DISTANCE = 1024
EPS = 1e-6
NEG = -1e30

LANES = 128
SUBLANES = 8
F32 = jnp.float32
BF16 = jnp.bfloat16

_VMEM_LIMIT = 56 * 1024 * 1024


def _rms(x, g):
    ms = jnp.mean(x * x, axis=-1, keepdims=True)
    return (x * lax.rsqrt(ms + EPS)) * g


def _t5_bucket_np(rel):
    half = NUM_BUCKETS // 2
    max_exact = half // 2
    sign = np.where(rel > 0, half, 0)
    n = np.abs(rel)
    nf = np.maximum(n, 1).astype(np.float32)
    large = max_exact + (np.log(nf / np.float32(max_exact))
                         / np.float32(math.log(MAX_DISTANCE / max_exact))
                         * np.float32(half - max_exact)).astype(np.int32)
    large = np.minimum(large, half - 1)
    return (sign + np.where(n < max_exact, n, large)).astype(np.int32)


def _bucket_index(rel_sub, dil, half_window):
    return np.where(np.abs(rel_sub) <= half_window, _t5_bucket_np(rel_sub * dil), -1).astype(np.int32)


def _bias_kernel(table_ref, idxa_ref, idxb_ref, outa_ref, outb_ref):
    def build(idx_ref, out_ref, head0):
        n_tiles, rows, _ = idx_ref.shape
        n_heads = out_ref.shape[1]

        def tile_body(t, carry):
            def chunk_body(i, c):
                r0 = pl.multiple_of(i * SUBLANES, SUBLANES)
                idx = idx_ref[t, pl.ds(r0, SUBLANES), :]
                accs = [jnp.full(idx.shape, NEG, F32)] * n_heads
                for b in range(NUM_BUCKETS):
                    mask = idx == b
                    accs = [jnp.where(mask, table_ref[b, head0 + h], a)
                            for h, a in enumerate(accs)]
                for h in range(n_heads):
                    out_ref[t, h, pl.ds(r0, SUBLANES), :] = accs[h]
                return c
            return lax.fori_loop(0, rows // SUBLANES, chunk_body, carry)

        lax.fori_loop(0, n_tiles, tile_body, 0)

    build(idxa_ref, outa_ref, 0)
    build(idxb_ref, outb_ref, N_HEADS_A)


def _bias(table, idx_a, idx_b):
    shape = lambda idx, nh: (idx.shape[0], nh) + idx.shape[1:]
    vmem = pl.BlockSpec(memory_space=pltpu.VMEM)
    return pl.pallas_call(
        _bias_kernel,
        in_specs=[pl.BlockSpec(memory_space=pltpu.SMEM), vmem, vmem],
        out_specs=[vmem, vmem],
        out_shape=[jax.ShapeDtypeStruct(shape(idx_a, N_HEADS_A), F32),
                   jax.ShapeDtypeStruct(shape(idx_b, N_HEADS_B), F32)],
        compiler_params=pltpu.CompilerParams(vmem_limit_bytes=_VMEM_LIMIT),
        name="bias",
    )(table, idx_a, idx_b)


def _inproj_kernel(x_ref, g_ref, w_ref, *refs, dils):
    scr = refs[-1]
    outs = refs[:-1]
    tm = x_ref.shape[0]
    u = _rms(x_ref[...], g_ref[...]).astype(BF16)
    col = 0
    for o_ref in outs[:3]:
        width = o_ref.shape[-1]
        o_ref[...] = jnp.dot(u, w_ref[:, col:col + width],
                             preferred_element_type=F32).astype(BF16)
        col += width
    for t in range(3):
        seg = jnp.dot(u, w_ref[:, col:col + WIDTH_B], preferred_element_type=F32)
        col += WIDTH_B
        n_slab = WIDTH_B // LANES
        if any(d > 1 for d in dils):
            for c in range(n_slab):
                scr[c] = seg[:, c * LANES:(c + 1) * LANES]
        for pi, d in enumerate(dils):
            o_ref = outs[3 + 3 * pi + t]
            if d == 1:
                o_ref[...] = seg.astype(BF16)
                continue
            for r in range(d):
                for c in range(n_slab):
                    o_ref[:, r * WIDTH_B + c * LANES:r * WIDTH_B + (c + 1) * LANES] = (
                        scr[c, pl.ds(r, tm // d, stride=d), :].astype(BF16))


def _inproj(x2d, g, w, tm, dils):
    n, d_model = x2d.shape
    shapes = [(n, WIDTH_A), (n, KV_WIDTH_A), (n, KV_WIDTH_A)]
    blocks = [(tm, WIDTH_A), (tm, KV_WIDTH_A), (tm, KV_WIDTH_A)]
    for d in dils:
        assert tm % (16 * d) == 0 and n % d == 0
        shapes += [(n // d, d * WIDTH_B)] * 3
        blocks += [(tm // d, d * WIDTH_B)] * 3
    return pl.pallas_call(
        functools.partial(_inproj_kernel, dils=dils),
        grid=(n // tm,),
        in_specs=[
            pl.BlockSpec((tm, d_model), lambda i: (i, 0)),
            pl.BlockSpec((1, d_model), lambda i: (0, 0)),
            pl.BlockSpec(w.shape, lambda i: (0, 0)),
        ],
        out_specs=[pl.BlockSpec(bs, lambda i: (i, 0)) for bs in blocks],
        out_shape=[jax.ShapeDtypeStruct(sh, BF16) for sh in shapes],
        scratch_shapes=[pltpu.VMEM((WIDTH_B // LANES, tm, LANES), F32)],
        compiler_params=pltpu.CompilerParams(
            dimension_semantics=("parallel",), vmem_limit_bytes=_VMEM_LIMIT),
        name="inproj",
    )(x2d, g, w)


def _nt_dot(a, b):
    return lax.dot_general(a, b, (((1,), (1,)), ((), ())), preferred_element_type=F32)


def _head_softmax_pv(qm, k_win, v_aug, bias, sink):
    s = _nt_dot(qm, k_win) + bias
    m = jnp.max(s, axis=-1, keepdims=True)
    if sink is not None:
        m = jnp.maximum(m, sink)
    e = jnp.exp(s - m)
    o = jnp.dot(e.astype(BF16), v_aug, preferred_element_type=F32)
    num = o[:, :LANES]
    den = o[:, LANES:]
    if sink is not None:
        den = den + jnp.exp(sink - m)
    return num, den, m


def _win_kernel(sink_ref, q_ref, k_ref, v_ref, bias_ref, o_ref, *, seq, blk):
    nblk = seq // blk
    lane = lax.broadcasted_iota(jnp.int32, (blk, LANES), 1)
    low = lane < HEAD_DIM

    def block(q0, k0, klen, bcol):
        k_win = k_ref[pl.ds(k0, klen), :]
        v_win = v_ref[pl.ds(k0, klen), :]
        v_aug = jnp.concatenate([v_win, jnp.ones_like(v_win)], axis=1)
        for j in range(GROUP_A):
            q_tile = q_ref[pl.ds(q0, blk), j * LANES:(j + 1) * LANES]
            res = []
            for hf in range(N_KV_A):
                h = hf * GROUP_A + j
                qm = jnp.where(low if hf == 0 else ~low, q_tile, jnp.zeros_like(q_tile))
                num, den, _ = _head_softmax_pv(
                    qm, k_win, v_aug, bias_ref[0, h, :, bcol:bcol + klen], sink_ref[h])
                res.append(num / den)
            o_ref[pl.ds(q0, blk), j * LANES:(j + 1) * LANES] = (
                jnp.where(low, res[0], res[1]).astype(o_ref.dtype))

    block(0, 0, 2 * blk, blk)

    def body(n, c):
        q0 = pl.multiple_of(n * blk, blk)
        block(q0, pl.multiple_of(q0 - blk, blk), 3 * blk, 0)
        return c

    lax.fori_loop(1, nblk - 1, body, 0)
    block(seq - blk, seq - 2 * blk, 2 * blk, 0)


def _windowed(qa, ka, va, bias, sink):
    b, s, _ = qa.shape
    blk = WINDOW_A
    assert s % blk == 0 and s >= 2 * blk
    kern = functools.partial(_win_kernel, seq=s, blk=blk)
    return pl.pallas_call(
        kern,
        grid=(b,),
        in_specs=[
            pl.BlockSpec(memory_space=pltpu.SMEM),
            pl.BlockSpec((None, s, WIDTH_A), lambda i: (i, 0, 0)),
            pl.BlockSpec((None, s, KV_WIDTH_A), lambda i: (i, 0, 0)),
            pl.BlockSpec((None, s, KV_WIDTH_A), lambda i: (i, 0, 0)),
            pl.BlockSpec(bias.shape, lambda i: (0, 0, 0, 0)),
        ],
        out_specs=pl.BlockSpec((None, s, WIDTH_A), lambda i: (i, 0, 0)),
        out_shape=jax.ShapeDtypeStruct((b, s, WIDTH_A), BF16),
        compiler_params=pltpu.CompilerParams(
            dimension_semantics=("parallel",), vmem_limit_bytes=_VMEM_LIMIT),
        name="win_gqa",
    )(sink, qa, ka, va, bias)


def _dil_kernel(q_ref, k_ref, v_ref, bias_ref, o_ref, lse_ref, *, ls, half):
    qb = 2 * half
    kw = 4 * half
    nblk = ls // qb
    lane = lax.broadcasted_iota(jnp.int32, (qb, LANES), 1)
    low = lane < HEAD_DIM

    def block(q0, k0, variant):
        lse_acc = jnp.zeros((qb, LANES), F32)
        for j in range(N_HEADS_B // 2):
            cols = slice(j * LANES, (j + 1) * LANES)
            k_win = k_ref[pl.ds(k0, kw), cols]
            v_win = v_ref[pl.ds(k0, kw), cols]
            v_aug = jnp.concatenate([v_win, jnp.ones_like(v_win)], axis=1)
            q_tile = q_ref[pl.ds(q0, qb), cols]
            res = []
            for hf in range(2):
                h = 2 * j + hf
                qm = jnp.where(low if hf == 0 else ~low, q_tile, jnp.zeros_like(q_tile))
                num, den, m = _head_softmax_pv(qm, k_win, v_aug, bias_ref[variant, h], None)
                res.append(num / den)
                lse_acc = jnp.where(lane == h, jnp.log(den) + m, lse_acc)
            o_ref[pl.ds(q0, qb), cols] = jnp.where(low, res[0], res[1]).astype(o_ref.dtype)
        lse_ref[pl.ds(q0, qb), :] = lse_acc

    block(0, 0, 0)
    if nblk > 2:
        def body(n, c):
            q0 = pl.multiple_of(n * qb, qb)
            block(q0, pl.multiple_of(q0 - half, half), 1)
            return c
        lax.fori_loop(1, nblk - 1, body, 0)
    block(ls - qb, ls - kw, 2)


def _dilated(qd, kd, vd, bias, pattern, batch, dil, half):
    rows, w_all = qd.shape
    w = w_all // dil
    ls = rows // batch
    assert ls % (2 * half) == 0 and ls >= 4 * half
    view = lambda t: t.reshape(batch, ls, w_all)
    kern = functools.partial(_dil_kernel, ls=ls, half=half)
    spec = pl.BlockSpec((None, ls, w), lambda i, r: (i, 0, r))
    n_var = 3
    o, lse = pl.pallas_call(
        kern,
        grid=(batch, dil),
        in_specs=[spec, spec, spec,
                  pl.BlockSpec((n_var,) + bias.shape[1:], lambda i, r: (pattern, 0, 0, 0))],
        out_specs=[spec, pl.BlockSpec((None, ls, LANES), lambda i, r: (i, 0, r))],
        out_shape=[jax.ShapeDtypeStruct((batch, ls, w_all), BF16),
                   jax.ShapeDtypeStruct((batch, ls, dil * LANES), F32)],
        compiler_params=pltpu.CompilerParams(
            dimension_semantics=("parallel", "parallel"), vmem_limit_bytes=_VMEM_LIMIT),
        name=f"dilated_d{dil}",
    )(view(qd), view(kd), view(vd), bias)
    return o.reshape(rows, w_all), lse.reshape(rows, dil * LANES)


def _post_kernel(x_ref, p_ref, oa_ref, *refs, dils, ff_chunk):
    n_pat = len(dils)
    ob_refs = refs[:n_pat]
    lse_refs = refs[n_pat:2 * n_pat]
    (expand_ref, goa_ref, gob_ref, wo_ref, gpm_ref, gpre_ref, wup_ref, wdn_ref,
     gmlp_ref, wpp_ref, wpg_ref, bpg_ref, gple_ref, out_ref) = refs[2 * n_pat:2 * n_pat + 14]
    scratch = refs[2 * n_pat + 14:]
    tm = x_ref.shape[0]
    n_slab = WIDTH_B // LANES

    obs, lses = [], []
    si = 0
    for d, o_ref, l_ref in zip(dils, ob_refs, lse_refs):
        if d == 1:
            obs.append(o_ref[...].astype(F32))
            lses.append(l_ref[...])
            continue
        o_scr, l_scr = scratch[si], scratch[si + 1]
        si += 2
        for r in range(d):
            rows = pl.ds(r, tm // d, stride=d)
            for c in range(n_slab):
                lo = r * WIDTH_B + c * LANES
                o_scr[c, rows, :] = o_ref[:, lo:lo + LANES].astype(F32)
            l_scr[rows, :] = l_ref[:, r * LANES:(r + 1) * LANES]
        obs.append(jnp.concatenate([o_scr[c] for c in range(n_slab)], axis=1))
        lses.append(l_scr[...])

    mx = functools.reduce(jnp.maximum, lses)
    es = [jnp.exp(l - mx) for l in lses]
    tot = functools.reduce(lambda a, b: a + b, es)
    ob = None
    for e, o in zip(es, obs):
        w = e / tot
        hi = w.astype(BF16)
        lo = (w - hi.astype(F32)).astype(BF16)
        wb = jnp.dot(jnp.concatenate([hi, lo], axis=1), expand_ref[...],
                     preferred_element_type=F32)
        term = wb * o
        ob = term if ob is None else ob + term

    oa = _rms(oa_ref[...].astype(F32), goa_ref[...]).astype(BF16)
    ob = _rms(ob, gob_ref[...]).astype(BF16)
    mix = (jnp.dot(oa, wo_ref[:WIDTH_A, :], preferred_element_type=F32)
           + jnp.dot(ob, wo_ref[WIDTH_A:, :], preferred_element_type=F32))
    h = x_ref[...] + _rms(mix, gpm_ref[...])

    v = _rms(h, gpre_ref[...]).astype(BF16)
    d_ff = wup_ref.shape[1]
    ff = None
    for c in range(0, d_ff, ff_chunk):
        a = jnp.maximum(jnp.dot(v, wup_ref[:, c:c + ff_chunk], preferred_element_type=F32), 0.0)
        t = jnp.dot((a * a).astype(BF16), wdn_ref[c:c + ff_chunk, :], preferred_element_type=F32)
        ff = t if ff is None else ff + t
    h = h + _rms(ff, gmlp_ref[...])

    gate = jax.nn.sigmoid(
        jnp.dot(h.astype(BF16), wpg_ref[...], preferred_element_type=F32) + bpg_ref[...])
    ple = jnp.dot(p_ref[...].astype(BF16), wpp_ref[...], preferred_element_type=F32)
    out_ref[...] = h + _rms(gate * ple, gple_ref[...])


def _post(x2d, p2d, oa, obs, lses, dils, consts, tm, ff_chunk):
    n, d_model = x2d.shape
    row = lambda width: pl.BlockSpec((tm, width), lambda i: (i, 0))
    const = lambda a: pl.BlockSpec(a.shape, lambda i: (0,) * a.ndim,
                                   pipeline_mode=pl.Buffered(1))
    scratch = []
    for d in dils:
        if d > 1:
            scratch += [pltpu.VMEM((WIDTH_B // LANES, tm, LANES), F32),
                        pltpu.VMEM((tm, LANES), F32)]
    return pl.pallas_call(
        functools.partial(_post_kernel, dils=dils, ff_chunk=ff_chunk),
        grid=(n // tm,),
        in_specs=[row(d_model), row(p2d.shape[1]), row(WIDTH_A)]
                 + [pl.BlockSpec((tm // d, d * WIDTH_B), lambda i: (i, 0)) for d in dils]
                 + [pl.BlockSpec((tm // d, d * LANES), lambda i: (i, 0)) for d in dils]
                 + [const(a) for a in consts],
        out_specs=row(d_model),
        out_shape=jax.ShapeDtypeStruct((n, d_model), x2d.dtype),
        scratch_shapes=scratch,
        compiler_params=pltpu.CompilerParams(
            dimension_semantics=("parallel",), vmem_limit_bytes=_VMEM_LIMIT),
        name="post",
    )(x2d, p2d, oa, *obs, *lses, *consts)


def kernel(x, p, rel_bias_table, g_pre_mix, w_in, sink_a, g_out_a, g_out_b, w_o, g_post_mix,
           g_pre_mlp, w_up, w_down, g_post_mlp, w_ple_proj, w_ple_gate, b_ple_gate, g_post_ple):
    b, s, d_model = x.shape
    depth = w_in.shape[0]
    n = b * s
    tm = 256
    dils = tuple(dil for _, dil in DILATED_PATTERNS)

    perm_heads = np.arange(N_HEADS_A).reshape(N_KV_A, GROUP_A).T.reshape(-1)
    perm_cols = (perm_heads[:, None] * HEAD_DIM + np.arange(HEAD_DIM)[None, :]).reshape(-1)

    qi = np.arange(WINDOW_A)[:, None]
    ki = np.arange(3 * WINDOW_A)[None, :]
    idx_a = _bucket_index(ki - WINDOW_A - qi, 1, WINDOW_A)[None]
    idx_b = []
    for window, dil in DILATED_PATTERNS:
        half = window // (2 * dil)
        assert half == DILATED_PATTERNS[0][0] // (2 * DILATED_PATTERNS[0][1])
        qi = np.arange(2 * half)[:, None]
        ki = np.arange(4 * half)[None, :]
        for off in (0, half, 2 * half):
            idx_b.append(_bucket_index(ki - off - qi, dil, half))
    bias_a, bias_b = _bias(rel_bias_table.astype(F32), jnp.asarray(idx_a),
                           jnp.asarray(np.stack(idx_b)))

    expand_np = np.zeros((2 * LANES, WIDTH_B), np.float32)
    for hh in range(N_HEADS_B):
        expand_np[hh, hh * HEAD_DIM:(hh + 1) * HEAD_DIM] = 1.0
        expand_np[LANES + hh, hh * HEAD_DIM:(hh + 1) * HEAD_DIM] = 1.0
    expand = jnp.asarray(expand_np, BF16)

    h2d = x.reshape(n, d_model)
    row = lambda a: a.reshape(1, -1).astype(F32)
    for i in range(depth):
        scale = HEAD_DIM ** -0.5
        w = w_in[i]
        o0 = WIDTH_A + 2 * KV_WIDTH_A
        w_all = jnp.concatenate([
            w[:, :WIDTH_A][:, perm_cols] * scale,
            w[:, WIDTH_A:o0],
            w[:, o0:o0 + WIDTH_B] * scale,
            w[:, o0 + WIDTH_B:],
        ], axis=1).astype(BF16)
        outs = _inproj(h2d, row(g_pre_mix[i]), w_all, tm, dils)
        qa, ka, va = (t.reshape(b, s, t.shape[-1]) for t in outs[:3])

        o_a = _windowed(qa, ka, va, bias_a, sink_a[i].astype(F32))
        obs, lses = [], []
        for pi, (window, dil) in enumerate(DILATED_PATTERNS):
            qd, kd, vd = outs[3 + 3 * pi:6 + 3 * pi]
            o, lse = _dilated(qd, kd, vd, bias_b, pi, b, dil, window // (2 * dil))
            obs.append(o)
            lses.append(lse)

        wo = jnp.concatenate([w_o[i][:WIDTH_A][perm_cols], w_o[i][WIDTH_A:]], axis=0).astype(BF16)
        consts = (expand, row(g_out_a[i][perm_cols]), row(g_out_b[i]), wo, row(g_post_mix[i]),
                  row(g_pre_mlp[i]), w_up[i].astype(BF16), w_down[i].astype(BF16),
                  row(g_post_mlp[i]), w_ple_proj[i].astype(BF16), w_ple_gate[i].astype(BF16),
                  row(b_ple_gate[i]), row(g_post_ple[i]))
        h2d = _post(h2d, p[i].reshape(n, -1), o_a.reshape(n, WIDTH_A), obs, lses, dils,
                    consts, tm, 1024)
    return h2d.reshape(b, s, d_model)
```

```python
import functools
import math

import jax
import jax.numpy as jnp
import numpy as np
from jax import lax
from jax.experimental import pallas as pl
from jax.experimental.pallas import tpu as pltpu

HEAD_DIM = 64
N_HEADS_A = 8
N_KV_A = 2
GROUP_A = N_HEADS_A // N_KV_A
WINDOW_A = 128
N_HEADS_B = 8
DILATED_PATTERNS = ((128, 1), (512, 4), (2048, 16))
WIDTH_A = N_HEADS_A * HEAD_DIM
WIDTH_B = N_HEADS_B * HEAD_DIM
KV_WIDTH_A = N_KV_A * HEAD_DIM
NUM_BUCKETS = 32
MAX_DISTANCE = 1024
EPS = 1e-6
NEG = -1e30
LOG2E = 1.4426950408889634

LANES = 128
SUBLANES = 8
F32 = jnp.float32
BF16 = jnp.bfloat16

_VMEM_LIMIT = 56 * 1024 * 1024
_BLOCK_UNROLL = 2


def _rms(x, g):
    ms = jnp.mean(x * x, axis=-1, keepdims=True)
    return (x * lax.rsqrt(ms + EPS)) * g


def _t5_bucket_np(rel):
    half = NUM_BUCKETS // 2
    max_exact = half // 2
    sign = np.where(rel > 0, half, 0)
    n = np.abs(rel)
    nf = np.maximum(n, 1).astype(np.float32)
    large = max_exact + (np.log(nf / np.float32(max_exact))
                         / np.float32(math.log(MAX_DISTANCE / max_exact))
                         * np.float32(half - max_exact)).astype(np.int32)
    large = np.minimum(large, half - 1)
    return (sign + np.where(n < max_exact, n, large)).astype(np.int32)


def _bucket_index(rel_sub, dil, half_window):
    return np.where(np.abs(rel_sub) <= half_window, _t5_bucket_np(rel_sub * dil), -1).astype(np.int32)


def _bias_kernel(table_ref, idxa_ref, idxb_ref, outa_ref, outb_ref):
    def build(idx_ref, out_ref, head0):
        n_tiles, rows, _ = idx_ref.shape
        n_heads = out_ref.shape[1]

        def tile_body(t, carry):
            def chunk_body(i, c):
                r0 = pl.multiple_of(i * SUBLANES, SUBLANES)
                idx = idx_ref[t, pl.ds(r0, SUBLANES), :]
                accs = [jnp.full(idx.shape, NEG, F32)] * n_heads
                for b in range(NUM_BUCKETS):
                    mask = idx == b
                    accs = [jnp.where(mask, table_ref[b, head0 + h], a)
                            for h, a in enumerate(accs)]
                for h in range(n_heads):
                    out_ref[t, h, pl.ds(r0, SUBLANES), :] = accs[h]
                return c
            return lax.fori_loop(0, rows // SUBLANES, chunk_body, carry)

        lax.fori_loop(0, n_tiles, tile_body, 0)

    build(idxa_ref, outa_ref, 0)
    build(idxb_ref, outb_ref, N_HEADS_A)


def _bias(table, idx_a, idx_b):
    shape = lambda idx, nh: (idx.shape[0], nh) + idx.shape[1:]
    vmem = pl.BlockSpec(memory_space=pltpu.VMEM)
    return pl.pallas_call(
        _bias_kernel,
        in_specs=[pl.BlockSpec(memory_space=pltpu.SMEM), vmem, vmem],
        out_specs=[vmem, vmem],
        out_shape=[jax.ShapeDtypeStruct(shape(idx_a, N_HEADS_A), F32),
                   jax.ShapeDtypeStruct(shape(idx_b, N_HEADS_B), F32)],
        compiler_params=pltpu.CompilerParams(vmem_limit_bytes=_VMEM_LIMIT),
        name="bias",
    )(table, idx_a, idx_b)


def _inproj_kernel(x_ref, g_ref, w_ref, *refs, dils):
    scr = refs[-1]
    outs = refs[:-1]
    tm = x_ref.shape[0]
    u = _rms(x_ref[...], g_ref[...]).astype(BF16)
    col = 0
    for o_ref in outs[:3]:
        width = o_ref.shape[-1]
        o_ref[...] = jnp.dot(u, w_ref[:, col:col + width],
                             preferred_element_type=F32).astype(BF16)
        col += width
    for t in range(3):
        seg = jnp.dot(u, w_ref[:, col:col + WIDTH_B], preferred_element_type=F32)
        col += WIDTH_B
        n_slab = WIDTH_B // LANES
        if any(d > 1 for d in dils):
            for c in range(n_slab):
                scr[c] = seg[:, c * LANES:(c + 1) * LANES]
        for pi, d in enumerate(dils):
            o_ref = outs[3 + 3 * pi + t]
            if d == 1:
                o_ref[...] = seg.astype(BF16)
                continue
            for r in range(d):
                for c in range(n_slab):
                    o_ref[:, r * WIDTH_B + c * LANES:r * WIDTH_B + (c + 1) * LANES] = (
                        scr[c, pl.ds(r, tm // d, stride=d), :].astype(BF16))


def _inproj(x2d, g, w, tm, dils):
    n, d_model = x2d.shape
    shapes = [(n, WIDTH_A), (n, KV_WIDTH_A), (n, KV_WIDTH_A)]
    blocks = [(tm, WIDTH_A), (tm, KV_WIDTH_A), (tm, KV_WIDTH_A)]
    for d in dils:
        assert tm % (16 * d) == 0 and n % d == 0
        shapes += [(n // d, d * WIDTH_B)] * 3
        blocks += [(tm // d, d * WIDTH_B)] * 3
    return pl.pallas_call(
        functools.partial(_inproj_kernel, dils=dils),
        grid=(n // tm,),
        in_specs=[
            pl.BlockSpec((tm, d_model), lambda i: (i, 0)),
            pl.BlockSpec((1, d_model), lambda i: (0, 0)),
            pl.BlockSpec(w.shape, lambda i: (0, 0)),
        ],
        out_specs=[pl.BlockSpec(bs, lambda i: (i, 0)) for bs in blocks],
        out_shape=[jax.ShapeDtypeStruct(sh, BF16) for sh in shapes],
        scratch_shapes=[pltpu.VMEM((WIDTH_B // LANES, tm, LANES), F32)],
        compiler_params=pltpu.CompilerParams(
            dimension_semantics=("parallel",), vmem_limit_bytes=_VMEM_LIMIT),
        name="inproj",
    )(x2d, g, w)


def _nt_dot(a, b):
    return lax.dot_general(a, b, (((1,), (1,)), ((), ())), preferred_element_type=F32)


def _head_softmax_pv(qm, k_win, v_aug, bias, sink):
    s = _nt_dot(qm, k_win) + bias
    m = jnp.max(s, axis=-1, keepdims=True)
    if sink is not None:
        m = jnp.maximum(m, sink)
    e = jnp.exp2(s - m)
    o = jnp.dot(e.astype(BF16), v_aug, preferred_element_type=F32)
    num = o[:, :LANES]
    den = o[:, LANES:]
    if sink is not None:
        den = den + jnp.exp2(sink - m)
    return num, den, m


def _win_kernel(sink_ref, q_ref, k_ref, v_ref, bias_ref, o_ref, den_ref, *, seq, blk, unroll):
    nblk = seq // blk
    lane = lax.broadcasted_iota(jnp.int32, (blk, LANES), 1)
    low = lane < HEAD_DIM

    klen = 3 * blk

    def block(n, c):
        q0 = pl.multiple_of(n * blk, blk)
        k0 = pl.multiple_of(jnp.clip(q0 - blk, 0, seq - klen), blk)
        variant = jnp.where(n == 0, 0, jnp.where(n == nblk - 1, 2, 1))
        k_win = k_ref[pl.ds(k0, klen), :]
        v_win = v_ref[pl.ds(k0, klen), :]
        v_aug = jnp.concatenate([v_win, jnp.ones_like(v_win)], axis=1)
        den_acc = jnp.zeros((blk, LANES), F32)
        for j in range(GROUP_A):
            q_tile = q_ref[pl.ds(q0, blk), j * LANES:(j + 1) * LANES]
            res = []
            for hf in range(N_KV_A):
                h = hf * GROUP_A + j
                qm = jnp.where(low if hf == 0 else ~low, q_tile, jnp.zeros_like(q_tile))
                num, den, _ = _head_softmax_pv(
                    qm, k_win, v_aug, bias_ref[variant, h], sink_ref[h])
                res.append(num)
                den_acc = jnp.where(lane == N_KV_A * j + hf, den, den_acc)
            o_ref[pl.ds(q0, blk), j * LANES:(j + 1) * LANES] = (
                jnp.where(low, res[0], res[1]).astype(o_ref.dtype))
        den_ref[pl.ds(q0, blk), :] = den_acc
        return c

    lax.fori_loop(0, nblk, block, 0, unroll=unroll)


def _windowed(qa, ka, va, bias, sink):
    b, s, _ = qa.shape
    blk = WINDOW_A
    nblk = s // blk
    assert s % blk == 0 and nblk >= 3
    kern = functools.partial(_win_kernel, seq=s, blk=blk, unroll=_BLOCK_UNROLL if nblk % _BLOCK_UNROLL == 0 else 1)
    return pl.pallas_call(
        kern,
        grid=(b,),
        in_specs=[
            pl.BlockSpec(memory_space=pltpu.SMEM),
            pl.BlockSpec((None, s, WIDTH_A), lambda i: (i, 0, 0)),
            pl.BlockSpec((None, s, KV_WIDTH_A), lambda i: (i, 0, 0)),
            pl.BlockSpec((None, s, KV_WIDTH_A), lambda i: (i, 0, 0)),
            pl.BlockSpec(bias.shape, lambda i: (0, 0, 0, 0)),
        ],
        out_specs=[pl.BlockSpec((None, s, WIDTH_A), lambda i: (i, 0, 0)),
                   pl.BlockSpec((None, s, LANES), lambda i: (i, 0, 0))],
        out_shape=[jax.ShapeDtypeStruct((b, s, WIDTH_A), BF16),
                   jax.ShapeDtypeStruct((b, s, LANES), F32)],
        compiler_params=pltpu.CompilerParams(
            dimension_semantics=("parallel",), vmem_limit_bytes=_VMEM_LIMIT),
        name="win_gqa",
    )(sink, qa, ka, va, bias)


def _dil_kernel(q_ref, k_ref, v_ref, bias_ref, o_ref, st_ref, *, ls, half, unroll):
    qb = 2 * half
    kw = 4 * half
    nblk = ls // qb
    lane = lax.broadcasted_iota(jnp.int32, (qb, LANES), 1)
    low = lane < HEAD_DIM

    def block(n, c):
        q0 = pl.multiple_of(n * qb, qb)
        k0 = pl.multiple_of(jnp.clip(q0 - half, 0, ls - kw), half)
        variant = jnp.where(n == 0, 0, jnp.where(n == nblk - 1, 2, 1))
        st_acc = jnp.zeros((qb, LANES), F32)
        for j in range(N_HEADS_B // 2):
            cols = slice(j * LANES, (j + 1) * LANES)
            k_win = k_ref[pl.ds(k0, kw), cols]
            v_win = v_ref[pl.ds(k0, kw), cols]
            v_aug = jnp.concatenate([v_win, jnp.ones_like(v_win)], axis=1)
            q_tile = q_ref[pl.ds(q0, qb), cols]
            res = []
            for hf in range(2):
                h = 2 * j + hf
                qm = jnp.where(low if hf == 0 else ~low, q_tile, jnp.zeros_like(q_tile))
                num, den, m = _head_softmax_pv(qm, k_win, v_aug, bias_ref[variant, h], None)
                res.append(num)
                st_acc = jnp.where(lane == h, m, jnp.where(lane == N_HEADS_B + h, den, st_acc))
            o_ref[pl.ds(q0, qb), cols] = jnp.where(low, res[0], res[1]).astype(o_ref.dtype)
        st_ref[pl.ds(q0, qb), :] = st_acc
        return c

    lax.fori_loop(0, nblk, block, 0, unroll=unroll)


def _dilated(qd, kd, vd, bias, pattern, batch, dil, half):
    rows, w_all = qd.shape
    w = w_all // dil
    ls = rows // batch
    nblk = ls // (2 * half)
    assert ls % (2 * half) == 0 and nblk >= 2
    view = lambda t: t.reshape(batch, ls, w_all)
    kern = functools.partial(_dil_kernel, ls=ls, half=half,
                             unroll=_BLOCK_UNROLL if nblk % _BLOCK_UNROLL == 0 else 1)
    spec = pl.BlockSpec((None, ls, w), lambda i, r: (i, 0, r))
    n_var = 3
    o, lse = pl.pallas_call(
        kern,
        grid=(batch, dil),
        in_specs=[spec, spec, spec,
                  pl.BlockSpec((n_var,) + bias.shape[1:], lambda i, r: (pattern, 0, 0, 0))],
        out_specs=[spec, pl.BlockSpec((None, ls, LANES), lambda i, r: (i, 0, r))],
        out_shape=[jax.ShapeDtypeStruct((batch, ls, w_all), BF16),
                   jax.ShapeDtypeStruct((batch, ls, dil * LANES), F32)],
        compiler_params=pltpu.CompilerParams(
            dimension_semantics=("parallel", "parallel"), vmem_limit_bytes=_VMEM_LIMIT),
        name=f"dilated_d{dil}",
    )(view(qd), view(kd), view(vd), bias)
    return o.reshape(rows, w_all), lse.reshape(rows, dil * LANES)


def _post_kernel(x_ref, p_ref, oa_ref, dena_ref, *refs, dils, ff_chunk):
    n_pat = len(dils)
    ob_refs = refs[:n_pat]
    st_refs = refs[n_pat:2 * n_pat]
    (expand_ref, goa_ref, gob_ref, wo_ref, gpm_ref, gpre_ref, wup_ref, wdn_ref,
     gmlp_ref, wpp_ref, wpg_ref, bpg_ref, gple_ref, out_ref) = refs[2 * n_pat:2 * n_pat + 14]
    scratch = refs[2 * n_pat + 14:]
    tm = x_ref.shape[0]
    n_slab = WIDTH_B // LANES
    head_lane = lax.broadcasted_iota(jnp.int32, (tm, LANES), 1) < N_HEADS_B

    def expand(w):
        w = jnp.where(head_lane, w, 0.0)
        hi = w.astype(BF16)
        lo = (w - hi.astype(F32)).astype(BF16)
        return jnp.dot(jnp.concatenate([hi, lo], axis=1), expand_ref[...],
                       preferred_element_type=F32)

    nums, stats = [], []
    si = 0
    for d, o_ref, s_ref in zip(dils, ob_refs, st_refs):
        if d == 1:
            nums.append(o_ref[...].astype(F32))
            stats.append(s_ref[...])
            continue
        o_scr, s_scr = scratch[si], scratch[si + 1]
        si += 2
        for r in range(d):
            rows = pl.ds(r, tm // d, stride=d)
            for c in range(n_slab):
                lo = r * WIDTH_B + c * LANES
                o_scr[c, rows, :] = o_ref[:, lo:lo + LANES].astype(F32)
            s_scr[rows, :] = s_ref[:, r * LANES:(r + 1) * LANES]
        nums.append(jnp.concatenate([o_scr[c] for c in range(n_slab)], axis=1))
        stats.append(s_scr[...])

    mx = functools.reduce(jnp.maximum, stats)
    scales = [jnp.exp2(st - mx) for st in stats]
    dens = [pltpu.roll(st, LANES - N_HEADS_B, 1) for st in stats]
    tot = functools.reduce(lambda a, b: a + b, [sc * dn for sc, dn in zip(scales, dens)])
    ob = None
    for sc, num in zip(scales, nums):
        term = expand(sc / tot) * num
        ob = term if ob is None else ob + term

    oa = expand(1.0 / dena_ref[...]) * oa_ref[...].astype(F32)
    oa = _rms(oa, goa_ref[...]).astype(BF16)
    ob = _rms(ob, gob_ref[...]).astype(BF16)
    mix = (jnp.dot(oa, wo_ref[:WIDTH_A, :], preferred_element_type=F32)
           + jnp.dot(ob, wo_ref[WIDTH_A:, :], preferred_element_type=F32))
    h = x_ref[...] + _rms(mix, gpm_ref[...])

    v = _rms(h, gpre_ref[...]).astype(BF16)
    d_ff = wup_ref.shape[1]
    ff = None
    for c in range(0, d_ff, ff_chunk):
        a = jnp.maximum(jnp.dot(v, wup_ref[:, c:c + ff_chunk], preferred_element_type=F32), 0.0)
        t = jnp.dot((a * a).astype(BF16), wdn_ref[c:c + ff_chunk, :], preferred_element_type=F32)
        ff = t if ff is None else ff + t
    h = h + _rms(ff, gmlp_ref[...])

    gate = jax.nn.sigmoid(
        jnp.dot(h.astype(BF16), wpg_ref[...], preferred_element_type=F32) + bpg_ref[...])
    ple = jnp.dot(p_ref[...].astype(BF16), wpp_ref[...], preferred_element_type=F32)
    out_ref[...] = h + _rms(gate * ple, gple_ref[...])


def _post(x2d, p2d, oa, den_a, obs, stats, dils, consts, tm, ff_chunk):
    n, d_model = x2d.shape
    row = lambda width: pl.BlockSpec((tm, width), lambda i: (i, 0))
    const = lambda a: pl.BlockSpec(a.shape, lambda i: (0,) * a.ndim,
                                   pipeline_mode=pl.Buffered(1))
    scratch = []
    for d in dils:
        if d > 1:
            scratch += [pltpu.VMEM((WIDTH_B // LANES, tm, LANES), F32),
                        pltpu.VMEM((tm, LANES), F32)]
    return pl.pallas_call(
        functools.partial(_post_kernel, dils=dils, ff_chunk=ff_chunk),
        grid=(n // tm,),
        in_specs=[row(d_model), row(p2d.shape[1]), row(WIDTH_A), row(LANES)]
                 + [pl.BlockSpec((tm // d, d * WIDTH_B), lambda i: (i, 0)) for d in dils]
                 + [pl.BlockSpec((tm // d, d * LANES), lambda i: (i, 0)) for d in dils]
                 + [const(a) for a in consts],
        out_specs=row(d_model),
        out_shape=jax.ShapeDtypeStruct((n, d_model), x2d.dtype),
        scratch_shapes=scratch,
        compiler_params=pltpu.CompilerParams(
            dimension_semantics=("parallel",), vmem_limit_bytes=_VMEM_LIMIT),
        name="post",
    )(x2d, p2d, oa, den_a, *obs, *stats, *consts)


def kernel(x, p, rel_bias_table, g_pre_mix, w_in, sink_a, g_out_a, g_out_b, w_o, g_post_mix,
           g_pre_mlp, w_up, w_down, g_post_mlp, w_ple_proj, w_ple_gate, b_ple_gate, g_post_ple):
    b, s, d_model = x.shape
    depth = w_in.shape[0]
    n = b * s
    tm = 256
    dils = tuple(dil for _, dil in DILATED_PATTERNS)

    perm_heads = np.arange(N_HEADS_A).reshape(N_KV_A, GROUP_A).T.reshape(-1)
    perm_cols = (perm_heads[:, None] * HEAD_DIM + np.arange(HEAD_DIM)[None, :]).reshape(-1)

    qi = np.arange(WINDOW_A)[:, None]
    ki = np.arange(3 * WINDOW_A)[None, :]
    idx_a = np.stack([_bucket_index(ki - off - qi, 1, WINDOW_A)
                      for off in (0, WINDOW_A, 2 * WINDOW_A)])
    idx_b = []
    for window, dil in DILATED_PATTERNS:
        half = window // (2 * dil)
        assert half == DILATED_PATTERNS[0][0] // (2 * DILATED_PATTERNS[0][1])
        qi = np.arange(2 * half)[:, None]
        ki = np.arange(4 * half)[None, :]
        for off in (0, half, 2 * half):
            idx_b.append(_bucket_index(ki - off - qi, dil, half))
    bias_a, bias_b = _bias(rel_bias_table.astype(F32) * LOG2E, jnp.asarray(idx_a),
                           jnp.asarray(np.stack(idx_b)))

    expand_np = np.zeros((2 * LANES, WIDTH_B), np.float32)
    for hh in range(N_HEADS_B):
        expand_np[hh, hh * HEAD_DIM:(hh + 1) * HEAD_DIM] = 1.0
        expand_np[LANES + hh, hh * HEAD_DIM:(hh + 1) * HEAD_DIM] = 1.0
    expand = jnp.asarray(expand_np, BF16)

    h2d = x.reshape(n, d_model)
    row = lambda a: a.reshape(1, -1).astype(F32)
    for i in range(depth):
        scale = HEAD_DIM ** -0.5 * LOG2E
        w = w_in[i]
        o0 = WIDTH_A + 2 * KV_WIDTH_A
        w_all = jnp.concatenate([
            w[:, :WIDTH_A][:, perm_cols] * scale,
            w[:, WIDTH_A:o0],
            w[:, o0:o0 + WIDTH_B] * scale,
            w[:, o0 + WIDTH_B:],
        ], axis=1).astype(BF16)
        outs = _inproj(h2d, row(g_pre_mix[i]), w_all, tm, dils)
        qa, ka, va = (t.reshape(b, s, t.shape[-1]) for t in outs[:3])

        o_a, den_a = _windowed(qa, ka, va, bias_a, sink_a[i].astype(F32) * LOG2E)
        obs, stats = [], []
        for pi, (window, dil) in enumerate(DILATED_PATTERNS):
            qd, kd, vd = outs[3 + 3 * pi:6 + 3 * pi]
            o, st = _dilated(qd, kd, vd, bias_b, pi, b, dil, window // (2 * dil))
            obs.append(o)
            stats.append(st)

        wo = jnp.concatenate([w_o[i][:WIDTH_A][perm_cols], w_o[i][WIDTH_A:]], axis=0).astype(BF16)
        consts = (expand, row(g_out_a[i][perm_cols]), row(g_out_b[i]), wo, row(g_post_mix[i]),
                  row(g_pre_mlp[i]), w_up[i].astype(BF16), w_down[i].astype(BF16),
                  row(g_post_mlp[i]), w_ple_proj[i].astype(BF16), w_ple_gate[i].astype(BF16),
                  row(b_ple_gate[i]), row(g_post_ple[i]))
        h2d = _post(h2d, p[i].reshape(n, -1), o_a.reshape(n, WIDTH_A), den_a.reshape(n, LANES),
                    obs, stats, dils, consts, 512, 1024)
    return h2d.reshape(b, s, d_model)
```

```python
import functools
import math

import jax
import jax.numpy as jnp
import numpy as np
from jax import lax
from jax.experimental import pallas as pl
from jax.experimental.pallas import tpu as pltpu

HEAD_DIM = 64
N_HEADS_A = 8
N_KV_A = 2
GROUP_A = N_HEADS_A // N_KV_A
WINDOW_A = 128
N_HEADS_B = 8
DILATED_PATTERNS = ((128, 1), (512, 4), (2048, 16))
WIDTH_A = N_HEADS_A * HEAD_DIM
WIDTH_B = N_HEADS_B * HEAD_DIM
KV_WIDTH_A = N_KV_A * HEAD_DIM
NUM_BUCKETS = 32
MAX_DISTANCE = 1024
EPS = 1e-6
NEG = -1e30
LOG2E = 1.4426950408889634

LANES = 128
SUBLANES = 8
F32 = jnp.float32
BF16 = jnp.bfloat16

_VMEM_LIMIT = 56 * 1024 * 1024
_BLOCKS_PER_TRIP = 8


def _rms(x, g):
    ms = jnp.mean(x * x, axis=-1, keepdims=True)
    return (x * lax.rsqrt(ms + EPS)) * g


def _t5_bucket_np(rel):
    half = NUM_BUCKETS // 2
    max_exact = half // 2
    sign = np.where(rel > 0, half, 0)
    n = np.abs(rel)
    nf = np.maximum(n, 1).astype(np.float32)
    large = max_exact + (np.log(nf / np.float32(max_exact))
                         / np.float32(math.log(MAX_DISTANCE / max_exact))
                         * np.float32(half - max_exact)).astype(np.int32)
    large = np.minimum(large, half - 1)
    return (sign + np.where(n < max_exact, n, large)).astype(np.int32)


def _bucket_index(rel_sub, dil, half_window):
    return np.where(np.abs(rel_sub) <= half_window, _t5_bucket_np(rel_sub * dil), -1).astype(np.int32)


def _bias_kernel(table_ref, idxa_ref, idxb_ref, outa_ref, outb_ref):
    def build(idx_ref, out_ref, head0):
        n_tiles, rows, _ = idx_ref.shape
        n_heads = out_ref.shape[1]

        def tile_body(t, carry):
            def chunk_body(i, c):
                r0 = pl.multiple_of(i * SUBLANES, SUBLANES)
                idx = idx_ref[t, pl.ds(r0, SUBLANES), :]
                accs = [jnp.full(idx.shape, NEG, F32)] * n_heads
                for b in range(NUM_BUCKETS):
                    mask = idx == b
                    accs = [jnp.where(mask, table_ref[b, head0 + h], a)
                            for h, a in enumerate(accs)]
                for h in range(n_heads):
                    out_ref[t, h, pl.ds(r0, SUBLANES), :] = accs[h]
                return c
            return lax.fori_loop(0, rows // SUBLANES, chunk_body, carry)

        lax.fori_loop(0, n_tiles, tile_body, 0)

    build(idxa_ref, outa_ref, 0)
    build(idxb_ref, outb_ref, N_HEADS_A)


def _bias(table, idx_a, idx_b):
    shape = lambda idx, nh: (idx.shape[0], nh) + idx.shape[1:]
    vmem = pl.BlockSpec(memory_space=pltpu.VMEM)
    return pl.pallas_call(
        _bias_kernel,
        in_specs=[pl.BlockSpec(memory_space=pltpu.SMEM), vmem, vmem],
        out_specs=[vmem, vmem],
        out_shape=[jax.ShapeDtypeStruct(shape(idx_a, N_HEADS_A), F32),
                   jax.ShapeDtypeStruct(shape(idx_b, N_HEADS_B), F32)],
        compiler_params=pltpu.CompilerParams(vmem_limit_bytes=_VMEM_LIMIT),
        name="bias",
    )(table, idx_a, idx_b)


def _inproj_kernel(x_ref, g_ref, w_ref, *refs, dils):
    scr = refs[-1]
    outs = refs[:-1]
    tm = x_ref.shape[0]
    u = _rms(x_ref[...], g_ref[...]).astype(BF16)
    col = 0
    for o_ref in outs[:3]:
        width = o_ref.shape[-1]
        o_ref[...] = jnp.dot(u, w_ref[:, col:col + width],
                             preferred_element_type=F32).astype(BF16)
        col += width
    for t in range(3):
        seg = jnp.dot(u, w_ref[:, col:col + WIDTH_B], preferred_element_type=F32)
        col += WIDTH_B
        n_slab = WIDTH_B // LANES
        if any(d > 1 for d in dils):
            for c in range(n_slab):
                scr[c] = seg[:, c * LANES:(c + 1) * LANES]
        for pi, d in enumerate(dils):
            o_ref = outs[3 + 3 * pi + t]
            if d == 1:
                o_ref[...] = seg.astype(BF16)
                continue
            for r in range(d):
                for c in range(n_slab):
                    o_ref[:, r * WIDTH_B + c * LANES:r * WIDTH_B + (c + 1) * LANES] = (
                        scr[c, pl.ds(r, tm // d, stride=d), :].astype(BF16))


def _inproj(x2d, g, w, tm, dils):
    n, d_model = x2d.shape
    shapes = [(n, WIDTH_A), (n, KV_WIDTH_A), (n, KV_WIDTH_A)]
    blocks = [(tm, WIDTH_A), (tm, KV_WIDTH_A), (tm, KV_WIDTH_A)]
    for d in dils:
        assert tm % (16 * d) == 0 and n % d == 0
        shapes += [(n // d, d * WIDTH_B)] * 3
        blocks += [(tm // d, d * WIDTH_B)] * 3
    return pl.pallas_call(
        functools.partial(_inproj_kernel, dils=dils),
        grid=(n // tm,),
        in_specs=[
            pl.BlockSpec((tm, d_model), lambda i: (i, 0)),
            pl.BlockSpec((1, d_model), lambda i: (0, 0)),
            pl.BlockSpec(w.shape, lambda i: (0, 0)),
        ],
        out_specs=[pl.BlockSpec(bs, lambda i: (i, 0)) for bs in blocks],
        out_shape=[jax.ShapeDtypeStruct(sh, BF16) for sh in shapes],
        scratch_shapes=[pltpu.VMEM((WIDTH_B // LANES, tm, LANES), F32)],
        compiler_params=pltpu.CompilerParams(
            dimension_semantics=("parallel",), vmem_limit_bytes=_VMEM_LIMIT),
        name="inproj",
    )(x2d, g, w)


def _nt_dot(a, b):
    return lax.dot_general(a, b, (((1,), (1,)), ((), ())), preferred_element_type=F32)


def _head_softmax_pv(qm, k_win, v_aug, bias, sink):
    s = _nt_dot(qm, k_win) + bias
    m = jnp.max(s, axis=-1, keepdims=True)
    if sink is not None:
        m = jnp.maximum(m, sink)
    e = jnp.exp2(s - m)
    o = jnp.dot(e.astype(BF16), v_aug, preferred_element_type=F32)
    num = o[:, :LANES]
    den = o[:, LANES:]
    if sink is not None:
        den = den + jnp.exp2(sink - m)
    return num, den, m


def _win_kernel(sink_ref, q_ref, k_ref, v_ref, bias_ref, o_ref, den_ref, *, seq, blk, unroll):
    nblk = seq // blk
    lane = lax.broadcasted_iota(jnp.int32, (blk, LANES), 1)
    low = lane < HEAD_DIM

    klen = 3 * blk

    def block(n, c):
        q0 = pl.multiple_of(n * blk, blk)
        k0 = pl.multiple_of(jnp.clip(q0 - blk, 0, seq - klen), blk)
        variant = jnp.where(n == 0, 0, jnp.where(n == nblk - 1, 2, 1))
        k_win = k_ref[pl.ds(k0, klen), :]
        v_win = v_ref[pl.ds(k0, klen), :]
        v_aug = jnp.concatenate([v_win, jnp.ones_like(v_win)], axis=1)
        den_acc = jnp.zeros((blk, LANES), F32)
        for j in range(GROUP_A):
            q_tile = q_ref[pl.ds(q0, blk), j * LANES:(j + 1) * LANES]
            res = []
            for hf in range(N_KV_A):
                h = hf * GROUP_A + j
                qm = jnp.where(low if hf == 0 else ~low, q_tile, jnp.zeros_like(q_tile))
                num, den, _ = _head_softmax_pv(
                    qm, k_win, v_aug, bias_ref[variant, h], sink_ref[h])
                res.append(num)
                den_acc = jnp.where(lane == N_KV_A * j + hf, den, den_acc)
            o_ref[pl.ds(q0, blk), j * LANES:(j + 1) * LANES] = (
                jnp.where(low, res[0], res[1]).astype(o_ref.dtype))
        den_ref[pl.ds(q0, blk), :] = den_acc
        return c

    lax.fori_loop(0, nblk, block, 0, unroll=unroll)


def _windowed(qa, ka, va, bias, sink):
    b, s, _ = qa.shape
    blk = WINDOW_A
    nblk = s // blk
    assert s % blk == 0 and nblk >= 3
    kern = functools.partial(_win_kernel, seq=s, blk=blk,
                             unroll=math.gcd(nblk, _BLOCKS_PER_TRIP))
    return pl.pallas_call(
        kern,
        grid=(b,),
        in_specs=[
            pl.BlockSpec(memory_space=pltpu.SMEM),
            pl.BlockSpec((None, s, WIDTH_A), lambda i: (i, 0, 0)),
            pl.BlockSpec((None, s, KV_WIDTH_A), lambda i: (i, 0, 0)),
            pl.BlockSpec((None, s, KV_WIDTH_A), lambda i: (i, 0, 0)),
            pl.BlockSpec(bias.shape, lambda i: (0, 0, 0, 0)),
        ],
        out_specs=[pl.BlockSpec((None, s, WIDTH_A), lambda i: (i, 0, 0)),
                   pl.BlockSpec((None, s, LANES), lambda i: (i, 0, 0))],
        out_shape=[jax.ShapeDtypeStruct((b, s, WIDTH_A), BF16),
                   jax.ShapeDtypeStruct((b, s, LANES), F32)],
        compiler_params=pltpu.CompilerParams(
            dimension_semantics=("parallel",), vmem_limit_bytes=_VMEM_LIMIT),
        name="win_gqa",
    )(sink, qa, ka, va, bias)


def _dil_kernel(q_ref, k_ref, v_ref, bias_ref, o_ref, st_ref, *, ls, half, n_res, unroll):
    qb = 2 * half
    kw = 4 * half
    nblk = ls // qb
    lane = lax.broadcasted_iota(jnp.int32, (qb, LANES), 1)
    low = lane < HEAD_DIM

    def block(n, res_i):
        q0 = pl.multiple_of(n * qb, qb)
        k0 = pl.multiple_of(jnp.clip(q0 - half, 0, ls - kw), half)
        variant = jnp.where(n == 0, 0, jnp.where(n == nblk - 1, 2, 1))
        st_acc = jnp.zeros((qb, LANES), F32)
        for j in range(N_HEADS_B // 2):
            cols = slice(res_i * WIDTH_B + j * LANES, res_i * WIDTH_B + (j + 1) * LANES)
            k_win = k_ref[pl.ds(k0, kw), cols]
            v_win = v_ref[pl.ds(k0, kw), cols]
            v_aug = jnp.concatenate([v_win, jnp.ones_like(v_win)], axis=1)
            q_tile = q_ref[pl.ds(q0, qb), cols]
            res = []
            for hf in range(2):
                h = 2 * j + hf
                qm = jnp.where(low if hf == 0 else ~low, q_tile, jnp.zeros_like(q_tile))
                num, den, m = _head_softmax_pv(qm, k_win, v_aug, bias_ref[variant, h], None)
                res.append(num)
                st_acc = jnp.where(lane == h, m, jnp.where(lane == N_HEADS_B + h, den, st_acc))
            o_ref[pl.ds(q0, qb), cols] = jnp.where(low, res[0], res[1]).astype(o_ref.dtype)
        st_ref[pl.ds(q0, qb), res_i * LANES:(res_i + 1) * LANES] = st_acc

    def trip(n, c):
        for res_i in range(n_res):
            block(n, res_i)
        return c

    lax.fori_loop(0, nblk, trip, 0, unroll=unroll)


def _dilated(qd, kd, vd, bias, pattern, batch, dil, half):
    rows, w_all = qd.shape
    assert w_all == dil * WIDTH_B
    ls = rows // batch
    nblk = ls // (2 * half)
    assert ls % (2 * half) == 0 and nblk >= 2
    n_res = math.gcd(dil, max(1, _BLOCKS_PER_TRIP // nblk))
    unroll = math.gcd(nblk, max(1, _BLOCKS_PER_TRIP // n_res))
    view = lambda t: t.reshape(batch, ls, w_all)
    kern = functools.partial(_dil_kernel, ls=ls, half=half, n_res=n_res, unroll=unroll)
    spec = pl.BlockSpec((None, ls, n_res * WIDTH_B), lambda i, r: (i, 0, r))
    n_var = 3
    o, st = pl.pallas_call(
        kern,
        grid=(batch, dil // n_res),
        in_specs=[spec, spec, spec,
                  pl.BlockSpec((n_var,) + bias.shape[1:], lambda i, r: (pattern, 0, 0, 0))],
        out_specs=[spec, pl.BlockSpec((None, ls, n_res * LANES), lambda i, r: (i, 0, r))],
        out_shape=[jax.ShapeDtypeStruct((batch, ls, w_all), BF16),
                   jax.ShapeDtypeStruct((batch, ls, dil * LANES), F32)],
        compiler_params=pltpu.CompilerParams(
            dimension_semantics=("parallel", "parallel"), vmem_limit_bytes=_VMEM_LIMIT),
        name=f"dilated_d{dil}",
    )(view(qd), view(kd), view(vd), bias)
    return o.reshape(rows, w_all), st.reshape(rows, dil * LANES)


def _post_kernel(x_ref, p_ref, oa_ref, dena_ref, *refs, dils, ff_chunk):
    n_pat = len(dils)
    ob_refs = refs[:n_pat]
    st_refs = refs[n_pat:2 * n_pat]
    (expand_ref, goa_ref, gob_ref, wo_ref, gpm_ref, gpre_ref, wup_ref, wdn_ref,
     gmlp_ref, wpp_ref, wpg_ref, bpg_ref, gple_ref, out_ref) = refs[2 * n_pat:2 * n_pat + 14]
    scratch = refs[2 * n_pat + 14:]
    tm = x_ref.shape[0]
    n_slab = WIDTH_B // LANES
    head_lane = lax.broadcasted_iota(jnp.int32, (tm, LANES), 1) < N_HEADS_B

    def expand(w):
        w = jnp.where(head_lane, w, 0.0)
        hi = w.astype(BF16)
        lo = (w - hi.astype(F32)).astype(BF16)
        return jnp.dot(jnp.concatenate([hi, lo], axis=1), expand_ref[...],
                       preferred_element_type=F32)

    nums, stats = [], []
    si = 0
    for d, o_ref, s_ref in zip(dils, ob_refs, st_refs):
        if d == 1:
            nums.append(o_ref[...].astype(F32))
            stats.append(s_ref[...])
            continue
        o_scr, s_scr = scratch[si], scratch[si + 1]
        si += 2
        for r in range(d):
            rows = pl.ds(r, tm // d, stride=d)
            for c in range(n_slab):
                lo = r * WIDTH_B + c * LANES
                o_scr[c, rows, :] = o_ref[:, lo:lo + LANES].astype(F32)
            s_scr[rows, :] = s_ref[:, r * LANES:(r + 1) * LANES]
        nums.append(jnp.concatenate([o_scr[c] for c in range(n_slab)], axis=1))
        stats.append(s_scr[...])

    mx = functools.reduce(jnp.maximum, stats)
    scales = [jnp.exp2(st - mx) for st in stats]
    dens = [pltpu.roll(st, LANES - N_HEADS_B, 1) for st in stats]
    tot = functools.reduce(lambda a, b: a + b, [sc * dn for sc, dn in zip(scales, dens)])
    ob = None
    for sc, num in zip(scales, nums):
        term = expand(sc / tot) * num
        ob = term if ob is None else ob + term

    oa = expand(1.0 / dena_ref[...]) * oa_ref[...].astype(F32)
    oa = _rms(oa, goa_ref[...]).astype(BF16)
    ob = _rms(ob, gob_ref[...]).astype(BF16)
    mix = (jnp.dot(oa, wo_ref[:WIDTH_A, :], preferred_element_type=F32)
           + jnp.dot(ob, wo_ref[WIDTH_A:, :], preferred_element_type=F32))
    h = x_ref[...] + _rms(mix, gpm_ref[...])

    v = _rms(h, gpre_ref[...]).astype(BF16)
    d_ff = wup_ref.shape[1]
    ff = None
    for c in range(0, d_ff, ff_chunk):
        a = jnp.maximum(jnp.dot(v, wup_ref[:, c:c + ff_chunk], preferred_element_type=F32), 0.0)
        t = jnp.dot((a * a).astype(BF16), wdn_ref[c:c + ff_chunk, :], preferred_element_type=F32)
        ff = t if ff is None else ff + t
    h = h + _rms(ff, gmlp_ref[...])

    gate = jax.nn.sigmoid(
        jnp.dot(h.astype(BF16), wpg_ref[...], preferred_element_type=F32) + bpg_ref[...])
    ple = jnp.dot(p_ref[...].astype(BF16), wpp_ref[...], preferred_element_type=F32)
    out_ref[...] = h + _rms(gate * ple, gple_ref[...])


def _post(x2d, p2d, oa, den_a, obs, stats, dils, consts, tm, ff_chunk):
    n, d_model = x2d.shape
    row = lambda width: pl.BlockSpec((tm, width), lambda i: (i, 0))
    const = lambda a: pl.BlockSpec(a.shape, lambda i: (0,) * a.ndim,
                                   pipeline_mode=pl.Buffered(1))
    scratch = []
    for d in dils:
        if d > 1:
            scratch += [pltpu.VMEM((WIDTH_B // LANES, tm, LANES), F32),
                        pltpu.VMEM((tm, LANES), F32)]
    return pl.pallas_call(
        functools.partial(_post_kernel, dils=dils, ff_chunk=ff_chunk),
        grid=(n // tm,),
        in_specs=[row(d_model), row(p2d.shape[1]), row(WIDTH_A), row(LANES)]
                 + [pl.BlockSpec((tm // d, d * WIDTH_B), lambda i: (i, 0)) for d in dils]
                 + [pl.BlockSpec((tm // d, d * LANES), lambda i: (i, 0)) for d in dils]
                 + [const(a) for a in consts],
        out_specs=row(d_model),
        out_shape=jax.ShapeDtypeStruct((n, d_model), x2d.dtype),
        scratch_shapes=scratch,
        compiler_params=pltpu.CompilerParams(
            dimension_semantics=("parallel",), vmem_limit_bytes=_VMEM_LIMIT),
        name="post",
    )(x2d, p2d, oa, den_a, *obs, *stats, *consts)


def kernel(x, p, rel_bias_table, g_pre_mix, w_in, sink_a, g_out_a, g_out_b, w_o, g_post_mix,
           g_pre_mlp, w_up, w_down, g_post_mlp, w_ple_proj, w_ple_gate, b_ple_gate, g_post_ple):
    b, s, d_model = x.shape
    depth = w_in.shape[0]
    n = b * s
    tm = 256
    dils = tuple(dil for _, dil in DILATED_PATTERNS)

    perm_heads = np.arange(N_HEADS_A).reshape(N_KV_A, GROUP_A).T.reshape(-1)
    perm_cols = (perm_heads[:, None] * HEAD_DIM + np.arange(HEAD_DIM)[None, :]).reshape(-1)

    qi = np.arange(WINDOW_A)[:, None]
    ki = np.arange(3 * WINDOW_A)[None, :]
    idx_a = np.stack([_bucket_index(ki - off - qi, 1, WINDOW_A)
                      for off in (0, WINDOW_A, 2 * WINDOW_A)])
    idx_b = []
    for window, dil in DILATED_PATTERNS:
        half = window // (2 * dil)
        assert half == DILATED_PATTERNS[0][0] // (2 * DILATED_PATTERNS[0][1])
        qi = np.arange(2 * half)[:, None]
        ki = np.arange(4 * half)[None, :]
        for off in (0, half, 2 * half):
            idx_b.append(_bucket_index(ki - off - qi, dil, half))
    bias_a, bias_b = _bias(rel_bias_table.astype(F32) * LOG2E, jnp.asarray(idx_a),
                           jnp.asarray(np.stack(idx_b)))

    expand_np = np.zeros((2 * LANES, WIDTH_B), np.float32)
    for hh in range(N_HEADS_B):
        expand_np[hh, hh * HEAD_DIM:(hh + 1) * HEAD_DIM] = 1.0
        expand_np[LANES + hh, hh * HEAD_DIM:(hh + 1) * HEAD_DIM] = 1.0
    expand = jnp.asarray(expand_np, BF16)

    h2d = x.reshape(n, d_model)
    row = lambda a: a.reshape(1, -1).astype(F32)
    for i in range(depth):
        scale = HEAD_DIM ** -0.5 * LOG2E
        w = w_in[i]
        o0 = WIDTH_A + 2 * KV_WIDTH_A
        w_all = jnp.concatenate([
            w[:, :WIDTH_A][:, perm_cols] * scale,
            w[:, WIDTH_A:o0],
            w[:, o0:o0 + WIDTH_B] * scale,
            w[:, o0 + WIDTH_B:],
        ], axis=1).astype(BF16)
        outs = _inproj(h2d, row(g_pre_mix[i]), w_all, tm, dils)
        qa, ka, va = (t.reshape(b, s, t.shape[-1]) for t in outs[:3])

        o_a, den_a = _windowed(qa, ka, va, bias_a, sink_a[i].astype(F32) * LOG2E)
        obs, stats = [], []
        for pi, (window, dil) in enumerate(DILATED_PATTERNS):
            qd, kd, vd = outs[3 + 3 * pi:6 + 3 * pi]
            o, st = _dilated(qd, kd, vd, bias_b, pi, b, dil, window // (2 * dil))
            obs.append(o)
            stats.append(st)

        wo = jnp.concatenate([w_o[i][:WIDTH_A][perm_cols], w_o[i][WIDTH_A:]], axis=0).astype(BF16)
        consts = (expand, row(g_out_a[i][perm_cols]), row(g_out_b[i]), wo, row(g_post_mix[i]),
                  row(g_pre_mlp[i]), w_up[i].astype(BF16), w_down[i].astype(BF16),
                  row(g_post_mlp[i]), w_ple_proj[i].astype(BF16), w_ple_gate[i].astype(BF16),
                  row(b_ple_gate[i]), row(g_post_ple[i]))
        h2d = _post(h2d, p[i].reshape(n, -1), o_a.reshape(n, WIDTH_A), den_a.reshape(n, LANES),
                    obs, stats, dils, consts, 512, 1024)
    return h2d.reshape(b, s, d_model)
```

```python
import functools
import math

import jax
import jax.numpy as jnp
import numpy as np
from jax import lax
from jax.experimental import pallas as pl
from jax.experimental.pallas import tpu as pltpu

HEAD_DIM = 64
N_HEADS_A = 8
N_KV_A = 2
GROUP_A = N_HEADS_A // N_KV_A
WINDOW_A = 128
N_HEADS_B = 8
DILATED_PATTERNS = ((128, 1), (512, 4), (2048, 16))
WIDTH_A = N_HEADS_A * HEAD_DIM
WIDTH_B = N_HEADS_B * HEAD_DIM
KV_WIDTH_A = N_KV_A * HEAD_DIM
NUM_BUCKETS = 32
MAX_DISTANCE = 1024
EPS = 1e-6
NEG = -1e30
LOG2E = 1.4426950408889634

LANES = 128
SUBLANES = 8
F32 = jnp.float32
BF16 = jnp.bfloat16

_VMEM_LIMIT = 56 * 1024 * 1024
_BLOCKS_PER_TRIP = 8


def _rms(x, g):
    ms = jnp.mean(x * x, axis=-1, keepdims=True)
    return (x * lax.rsqrt(ms + EPS)) * g


def _t5_bucket_np(rel):
    half = NUM_BUCKETS // 2
    max_exact = half // 2
    sign = np.where(rel > 0, half, 0)
    n = np.abs(rel)
    nf = np.maximum(n, 1).astype(np.float32)
    large = max_exact + (np.log(nf / np.float32(max_exact))
                         / np.float32(math.log(MAX_DISTANCE / max_exact))
                         * np.float32(half - max_exact)).astype(np.int32)
    large = np.minimum(large, half - 1)
    return (sign + np.where(n < max_exact, n, large)).astype(np.int32)


def _band_layout(q_len, k_len, offs, dil, half_window):
    center = q_len - 1 + max(offs)
    length = -(-(center + k_len) // LANES) * LANES
    rel = np.arange(length) - center
    idx = np.where(np.abs(rel) <= half_window, _t5_bucket_np(rel * dil), -1).astype(np.int32)
    return idx, tuple((off - center) % length for off in offs)


def _bias_kernel(tab_ref, idxa_ref, idxb_ref, outa_ref, outb_ref, *, shifts_a, shifts_b):
    def build(idx_ref, out_ref, head0, shifts):
        n_pat, _, length = idx_ref.shape
        _, n_heads, q_len, k_len = out_ref.shape
        for p in range(n_pat):
            idx = idx_ref[p]
            vec = jnp.full((n_heads, length), NEG, F32)
            for b in range(NUM_BUCKETS):
                vec = jnp.where(idx == b, tab_ref[head0:head0 + n_heads, b:b + 1], vec)
            for h in range(n_heads):
                rows = jnp.broadcast_to(vec[h:h + 1, :], (q_len, length))
                for v, s in enumerate(shifts):
                    band = pltpu.roll(rows, s, 1, stride=1, stride_axis=0)
                    out_ref[p * len(shifts) + v, h] = band[:, :k_len]

    build(idxa_ref, outa_ref, 0, shifts_a)
    build(idxb_ref, outb_ref, N_HEADS_A, shifts_b)


def _bias(table_t, idx_a, shifts_a, shape_a, idx_b, shifts_b, shape_b):
    vmem = pl.BlockSpec(memory_space=pltpu.VMEM)
    return pl.pallas_call(
        functools.partial(_bias_kernel, shifts_a=shifts_a, shifts_b=shifts_b),
        in_specs=[vmem, vmem, vmem],
        out_specs=[vmem, vmem],
        out_shape=[jax.ShapeDtypeStruct(shape_a, F32), jax.ShapeDtypeStruct(shape_b, F32)],
        compiler_params=pltpu.CompilerParams(vmem_limit_bytes=_VMEM_LIMIT),
        name="bias",
    )(table_t, idx_a, idx_b)


def _inproj_kernel(x_ref, g_ref, w_ref, *refs, dils, stage_dils):
    n_pat = len(dils)
    outs = refs[:3 + 3 * n_pat]
    scr = refs[3 + 3 * n_pat]
    stage_scr = dict(zip(stage_dils, refs[4 + 3 * n_pat:]))
    tm = x_ref.shape[0]
    u = _rms(x_ref[...], g_ref[...]).astype(BF16)
    n_slab = WIDTH_B // LANES
    col = sum(o_ref.shape[-1] for o_ref in outs[:3])
    for t in range(3):
        seg = jnp.dot(u, w_ref[:, col:col + WIDTH_B], preferred_element_type=F32)
        col += WIDTH_B
        if any(d > 1 for d in dils):
            for c in range(n_slab):
                scr[t, c] = seg[:, c * LANES:(c + 1) * LANES]
        for pi, d in enumerate(dils):
            o_ref = outs[3 + 3 * pi + t]
            if d == 1:
                o_ref[...] = seg.astype(BF16)
                continue
            base = max([b for b in stage_dils if b < d and d % b == 0], default=1)
            step = d // base
            for r in range(d):
                for c in range(n_slab):
                    rows = pl.ds(r // base, tm // d, stride=step)
                    src = scr.at[t, c] if base == 1 else stage_scr[base].at[t, r % base, c]
                    val = src[rows, :]
                    if d in stage_scr:
                        stage_scr[d][t, r, c] = val
                    o_ref[:, r * WIDTH_B + c * LANES:r * WIDTH_B + (c + 1) * LANES] = (
                        val.astype(BF16))
    col = 0
    for o_ref in outs[:3]:
        width = o_ref.shape[-1]
        o_ref[...] = jnp.dot(u, w_ref[:, col:col + width],
                             preferred_element_type=F32).astype(BF16)
        col += width


def _inproj(x2d, g, w, tm, dils):
    n, d_model = x2d.shape
    shapes = [(n, WIDTH_A), (n, KV_WIDTH_A), (n, KV_WIDTH_A)]
    blocks = [(tm, WIDTH_A), (tm, KV_WIDTH_A), (tm, KV_WIDTH_A)]
    for d in dils:
        assert tm % (16 * d) == 0 and n % d == 0
        shapes += [(n // d, d * WIDTH_B)] * 3
        blocks += [(tm // d, d * WIDTH_B)] * 3
    stage_dils = tuple(b for b in dils if b > 1 and any(d > b and d % b == 0 for d in dils))
    n_slab = WIDTH_B // LANES
    return pl.pallas_call(
        functools.partial(_inproj_kernel, dils=dils, stage_dils=stage_dils),
        grid=(n // tm,),
        in_specs=[
            pl.BlockSpec((tm, d_model), lambda i: (i, 0)),
            pl.BlockSpec((1, d_model), lambda i: (0, 0)),
            pl.BlockSpec(w.shape, lambda i: (0, 0)),
        ],
        out_specs=[pl.BlockSpec(bs, lambda i: (i, 0)) for bs in blocks],
        out_shape=[jax.ShapeDtypeStruct(sh, BF16) for sh in shapes],
        scratch_shapes=[pltpu.VMEM((3, n_slab, tm, LANES), F32)]
                       + [pltpu.VMEM((3, b, n_slab, tm // b, LANES), F32) for b in stage_dils],
        compiler_params=pltpu.CompilerParams(
            dimension_semantics=("parallel",), vmem_limit_bytes=_VMEM_LIMIT),
        name="inproj",
    )(x2d, g, w)


def _nt_dot(a, b):
    return lax.dot_general(a, b, (((1,), (1,)), ((), ())), preferred_element_type=F32)


def _head_softmax_pv(qm, k_win, v_aug, bias, sink):
    s = _nt_dot(qm, k_win) + bias
    m = jnp.max(s, axis=-1, keepdims=True)
    if sink is not None:
        m = jnp.maximum(m, sink)
    e = jnp.exp2(s - m)
    o = jnp.dot(e.astype(BF16), v_aug, preferred_element_type=F32)
    num = o[:, :LANES]
    den = o[:, LANES:]
    if sink is not None:
        den = den + jnp.exp2(sink - m)
    return num, den, m


def _win_kernel(sink_ref, q_ref, k_ref, v_ref, bias_ref, o_ref, den_ref, *, seq, blk, unroll):
    nblk = seq // blk
    lane = lax.broadcasted_iota(jnp.int32, (blk, LANES), 1)
    low = lane < HEAD_DIM

    klen = 3 * blk

    def block(n, c):
        q0 = pl.multiple_of(n * blk, blk)
        k0 = pl.multiple_of(jnp.clip(q0 - blk, 0, seq - klen), blk)
        variant = jnp.where(n == 0, 0, jnp.where(n == nblk - 1, 2, 1))
        k_win = k_ref[pl.ds(k0, klen), :]
        v_win = v_ref[pl.ds(k0, klen), :]
        v_aug = jnp.concatenate([v_win, jnp.ones_like(v_win)], axis=1)
        den_acc = jnp.zeros((blk, LANES), F32)
        for j in range(GROUP_A):
            q_tile = q_ref[pl.ds(q0, blk), j * LANES:(j + 1) * LANES]
            res = []
            for hf in range(N_KV_A):
                h = hf * GROUP_A + j
                qm = jnp.where(low if hf == 0 else ~low, q_tile, jnp.zeros_like(q_tile))
                num, den, _ = _head_softmax_pv(
                    qm, k_win, v_aug, bias_ref[variant, h], sink_ref[h])
                res.append(num)
                den_acc = jnp.where(lane == N_KV_A * j + hf, den, den_acc)
            o_ref[pl.ds(q0, blk), j * LANES:(j + 1) * LANES] = (
                jnp.where(low, res[0], res[1]).astype(o_ref.dtype))
        den_ref[pl.ds(q0, blk), :] = den_acc
        return c

    lax.fori_loop(0, nblk, block, 0, unroll=unroll)


def _windowed(qa, ka, va, bias, sink):
    b, s, _ = qa.shape
    blk = WINDOW_A
    nblk = s // blk
    assert s % blk == 0 and nblk >= 3
    kern = functools.partial(_win_kernel, seq=s, blk=blk,
                             unroll=math.gcd(nblk, _BLOCKS_PER_TRIP))
    return pl.pallas_call(
        kern,
        grid=(b,),
        in_specs=[
            pl.BlockSpec(memory_space=pltpu.SMEM),
            pl.BlockSpec((None, s, WIDTH_A), lambda i: (i, 0, 0)),
            pl.BlockSpec((None, s, KV_WIDTH_A), lambda i: (i, 0, 0)),
            pl.BlockSpec((None, s, KV_WIDTH_A), lambda i: (i, 0, 0)),
            pl.BlockSpec(bias.shape, lambda i: (0, 0, 0, 0)),
        ],
        out_specs=[pl.BlockSpec((None, s, WIDTH_A), lambda i: (i, 0, 0)),
                   pl.BlockSpec((None, s, LANES), lambda i: (i, 0, 0))],
        out_shape=[jax.ShapeDtypeStruct((b, s, WIDTH_A), BF16),
                   jax.ShapeDtypeStruct((b, s, LANES), F32)],
        compiler_params=pltpu.CompilerParams(
            dimension_semantics=("parallel",), vmem_limit_bytes=_VMEM_LIMIT),
        name="win_gqa",
    )(sink, qa, ka, va, bias)


def _dil_kernel(q_ref, k_ref, v_ref, bias_ref, o_ref, st_ref, *, ls, half, n_res, unroll):
    qb = 2 * half
    kw = 4 * half
    nblk = ls // qb
    lane = lax.broadcasted_iota(jnp.int32, (qb, LANES), 1)
    low = lane < HEAD_DIM

    def block(n, res_i):
        q0 = pl.multiple_of(n * qb, qb)
        k0 = pl.multiple_of(jnp.clip(q0 - half, 0, ls - kw), half)
        variant = jnp.where(n == 0, 0, jnp.where(n == nblk - 1, 2, 1))
        st_acc = jnp.zeros((qb, LANES), F32)
        for j in range(N_HEADS_B // 2):
            cols = slice(res_i * WIDTH_B + j * LANES, res_i * WIDTH_B + (j + 1) * LANES)
            k_win = k_ref[pl.ds(k0, kw), cols]
            v_win = v_ref[pl.ds(k0, kw), cols]
            v_aug = jnp.concatenate([v_win, jnp.ones_like(v_win)], axis=1)
            q_tile = q_ref[pl.ds(q0, qb), cols]
            res = []
            for hf in range(2):
                h = 2 * j + hf
                qm = jnp.where(low if hf == 0 else ~low, q_tile, jnp.zeros_like(q_tile))
                num, den, m = _head_softmax_pv(qm, k_win, v_aug, bias_ref[variant, h], None)
                res.append(num)
                st_acc = jnp.where(lane == h, m, jnp.where(lane == N_HEADS_B + h, den, st_acc))
            o_ref[pl.ds(q0, qb), cols] = jnp.where(low, res[0], res[1]).astype(o_ref.dtype)
        st_ref[pl.ds(q0, qb), res_i * LANES:(res_i + 1) * LANES] = st_acc

    def trip(n, c):
        for res_i in range(n_res):
            block(n, res_i)
        return c

    lax.fori_loop(0, nblk, trip, 0, unroll=unroll)


def _dilated(qd, kd, vd, bias, pattern, batch, dil, half):
    rows, w_all = qd.shape
    assert w_all == dil * WIDTH_B
    ls = rows // batch
    nblk = ls // (2 * half)
    assert ls % (2 * half) == 0 and nblk >= 2
    n_res = math.gcd(dil, max(1, _BLOCKS_PER_TRIP // nblk))
    unroll = math.gcd(nblk, max(1, _BLOCKS_PER_TRIP // n_res))
    view = lambda t: t.reshape(batch, ls, w_all)
    kern = functools.partial(_dil_kernel, ls=ls, half=half, n_res=n_res, unroll=unroll)
    spec = pl.BlockSpec((None, ls, n_res * WIDTH_B), lambda i, r: (i, 0, r))
    n_var = 3
    o, st = pl.pallas_call(
        kern,
        grid=(batch, dil // n_res),
        in_specs=[spec, spec, spec,
                  pl.BlockSpec((n_var,) + bias.shape[1:], lambda i, r: (pattern, 0, 0, 0))],
        out_specs=[spec, pl.BlockSpec((None, ls, n_res * LANES), lambda i, r: (i, 0, r))],
        out_shape=[jax.ShapeDtypeStruct((batch, ls, w_all), BF16),
                   jax.ShapeDtypeStruct((batch, ls, dil * LANES), F32)],
        compiler_params=pltpu.CompilerParams(
            dimension_semantics=("parallel", "parallel"), vmem_limit_bytes=_VMEM_LIMIT),
        name=f"dilated_d{dil}",
    )(view(qd), view(kd), view(vd), bias)
    return o.reshape(rows, w_all), st.reshape(rows, dil * LANES)


def _post_kernel(x_ref, p_ref, oa_ref, dena_ref, *refs, dils, ff_chunk):
    n_pat = len(dils)
    ob_refs = refs[:n_pat]
    st_refs = refs[n_pat:2 * n_pat]
    (expand_ref, goa_ref, gob_ref, wo_ref, gpm_ref, gpre_ref, wup_ref, wdn_ref,
     gmlp_ref, wpp_ref, wpg_ref, bpg_ref, gple_ref, out_ref) = refs[2 * n_pat:2 * n_pat + 14]
    scratch = refs[2 * n_pat + 14:]
    tm = x_ref.shape[0]
    n_slab = WIDTH_B // LANES
    head_lane = lax.broadcasted_iota(jnp.int32, (tm, LANES), 1) < N_HEADS_B

    def expand(w):
        w = jnp.where(head_lane, w, 0.0)
        hi = w.astype(BF16)
        lo = (w - hi.astype(F32)).astype(BF16)
        return jnp.dot(jnp.concatenate([hi, lo], axis=1), expand_ref[...],
                       preferred_element_type=F32)

    nums, stats = [], []
    si = 0
    for d, o_ref, s_ref in zip(dils, ob_refs, st_refs):
        if d == 1:
            nums.append(o_ref[...].astype(F32))
            stats.append(s_ref[...])
            continue
        o_scr, s_scr = scratch[si], scratch[si + 1]
        si += 2
        for r in range(d):
            rows = pl.ds(r, tm // d, stride=d)
            for c in range(n_slab):
                lo = r * WIDTH_B + c * LANES
                o_scr[c, rows, :] = o_ref[:, lo:lo + LANES].astype(F32)
            s_scr[rows, :] = s_ref[:, r * LANES:(r + 1) * LANES]
        nums.append(jnp.concatenate([o_scr[c] for c in range(n_slab)], axis=1))
        stats.append(s_scr[...])

    mx = functools.reduce(jnp.maximum, stats)
    scales = [jnp.exp2(st - mx) for st in stats]
    dens = [pltpu.roll(st, LANES - N_HEADS_B, 1) for st in stats]
    tot = functools.reduce(lambda a, b: a + b, [sc * dn for sc, dn in zip(scales, dens)])
    ob = None
    for sc, num in zip(scales, nums):
        term = expand(sc / tot) * num
        ob = term if ob is None else ob + term

    oa = expand(1.0 / dena_ref[...]) * oa_ref[...].astype(F32)
    oa = _rms(oa, goa_ref[...]).astype(BF16)
    ob = _rms(ob, gob_ref[...]).astype(BF16)
    mix = (jnp.dot(oa, wo_ref[:WIDTH_A, :], preferred_element_type=F32)
           + jnp.dot(ob, wo_ref[WIDTH_A:, :], preferred_element_type=F32))
    h = x_ref[...] + _rms(mix, gpm_ref[...])

    v = _rms(h, gpre_ref[...]).astype(BF16)
    d_ff = wup_ref.shape[1]
    ff = None
    for c in range(0, d_ff, ff_chunk):
        a = jnp.maximum(jnp.dot(v, wup_ref[:, c:c + ff_chunk], preferred_element_type=F32), 0.0)
        t = jnp.dot((a * a).astype(BF16), wdn_ref[c:c + ff_chunk, :], preferred_element_type=F32)
        ff = t if ff is None else ff + t
    h = h + _rms(ff, gmlp_ref[...])

    gate = jax.nn.sigmoid(
        jnp.dot(h.astype(BF16), wpg_ref[...], preferred_element_type=F32) + bpg_ref[...])
    ple = jnp.dot(p_ref[...].astype(BF16), wpp_ref[...], preferred_element_type=F32)
    out_ref[...] = h + _rms(gate * ple, gple_ref[...])


def _post(x2d, p2d, oa, den_a, obs, stats, dils, consts, tm, ff_chunk):
    n, d_model = x2d.shape
    row = lambda width: pl.BlockSpec((tm, width), lambda i: (i, 0))
    const = lambda a: pl.BlockSpec(a.shape, lambda i: (0,) * a.ndim,
                                   pipeline_mode=pl.Buffered(1))
    scratch = []
    for d in dils:
        if d > 1:
            scratch += [pltpu.VMEM((WIDTH_B // LANES, tm, LANES), F32),
                        pltpu.VMEM((tm, LANES), F32)]
    return pl.pallas_call(
        functools.partial(_post_kernel, dils=dils, ff_chunk=ff_chunk),
        grid=(n // tm,),
        in_specs=[row(d_model), row(p2d.shape[1]), row(WIDTH_A), row(LANES)]
                 + [pl.BlockSpec((tm // d, d * WIDTH_B), lambda i: (i, 0)) for d in dils]
                 + [pl.BlockSpec((tm // d, d * LANES), lambda i: (i, 0)) for d in dils]
                 + [const(a) for a in consts],
        out_specs=row(d_model),
        out_shape=jax.ShapeDtypeStruct((n, d_model), x2d.dtype),
        scratch_shapes=scratch,
        compiler_params=pltpu.CompilerParams(
            dimension_semantics=("parallel",), vmem_limit_bytes=_VMEM_LIMIT),
        name="post",
    )(x2d, p2d, oa, den_a, *obs, *stats, *consts)


def kernel(x, p, rel_bias_table, g_pre_mix, w_in, sink_a, g_out_a, g_out_b, w_o, g_post_mix,
           g_pre_mlp, w_up, w_down, g_post_mlp, w_ple_proj, w_ple_gate, b_ple_gate, g_post_ple):
    b, s, d_model = x.shape
    depth = w_in.shape[0]
    n = b * s
    tm = 512
    dils = tuple(dil for _, dil in DILATED_PATTERNS)

    perm_heads = np.arange(N_HEADS_A).reshape(N_KV_A, GROUP_A).T.reshape(-1)
    perm_cols = (perm_heads[:, None] * HEAD_DIM + np.arange(HEAD_DIM)[None, :]).reshape(-1)

    idx_a, shifts_a = _band_layout(WINDOW_A, 3 * WINDOW_A, (0, WINDOW_A, 2 * WINDOW_A),
                                   1, WINDOW_A)
    half = DILATED_PATTERNS[0][0] // (2 * DILATED_PATTERNS[0][1])
    idx_b = []
    for window, dil in DILATED_PATTERNS:
        assert window // (2 * dil) == half
        idx, shifts_b = _band_layout(2 * half, 4 * half, (0, half, 2 * half), dil, half)
        idx_b.append(idx)
    bias_a, bias_b = _bias(
        rel_bias_table.T.astype(F32) * LOG2E,
        jnp.asarray(idx_a)[None, None], shifts_a,
        (len(shifts_a), N_HEADS_A, WINDOW_A, 3 * WINDOW_A),
        jnp.asarray(np.stack(idx_b))[:, None], shifts_b,
        (len(DILATED_PATTERNS) * len(shifts_b), N_HEADS_B, 2 * half, 4 * half))

    expand_np = np.zeros((2 * LANES, WIDTH_B), np.float32)
    for hh in range(N_HEADS_B):
        expand_np[hh, hh * HEAD_DIM:(hh + 1) * HEAD_DIM] = 1.0
        expand_np[LANES + hh, hh * HEAD_DIM:(hh + 1) * HEAD_DIM] = 1.0
    expand = jnp.asarray(expand_np, BF16)

    h2d = x.reshape(n, d_model)
    row = lambda a: a.reshape(1, -1).astype(F32)
    for i in range(depth):
        scale = HEAD_DIM ** -0.5 * LOG2E
        w = w_in[i]
        o0 = WIDTH_A + 2 * KV_WIDTH_A
        w_all = jnp.concatenate([
            w[:, :WIDTH_A][:, perm_cols] * scale,
            w[:, WIDTH_A:o0],
            w[:, o0:o0 + WIDTH_B] * scale,
            w[:, o0 + WIDTH_B:],
        ], axis=1).astype(BF16)
        outs = _inproj(h2d, row(g_pre_mix[i]), w_all, tm, dils)
        qa, ka, va = (t.reshape(b, s, t.shape[-1]) for t in outs[:3])

        o_a, den_a = _windowed(qa, ka, va, bias_a, sink_a[i].astype(F32) * LOG2E)
        obs, stats = [], []
        for pi, (window, dil) in enumerate(DILATED_PATTERNS):
            qd, kd, vd = outs[3 + 3 * pi:6 + 3 * pi]
            o, st = _dilated(qd, kd, vd, bias_b, pi, b, dil, window // (2 * dil))
            obs.append(o)
            stats.append(st)

        wo = jnp.concatenate([w_o[i][:WIDTH_A][perm_cols], w_o[i][WIDTH_A:]], axis=0).astype(BF16)
        consts = (expand, row(g_out_a[i][perm_cols]), row(g_out_b[i]), wo, row(g_post_mix[i]),
                  row(g_pre_mlp[i]), w_up[i].astype(BF16), w_down[i].astype(BF16),
                  row(g_post_mlp[i]), w_ple_proj[i].astype(BF16), w_ple_gate[i].astype(BF16),
                  row(b_ple_gate[i]), row(g_post_ple[i]))
        h2d = _post(h2d, p[i].reshape(n, -1), o_a.reshape(n, WIDTH_A), den_a.reshape(n, LANES),
                    obs, stats, dils, consts, tm, 1024)
    return h2d.reshape(b, s, d_model)
```

```python
import functools
import math

import jax
import jax.numpy as jnp
import numpy as np
from jax import lax
from jax.experimental import pallas as pl
from jax.experimental.pallas import tpu as pltpu

HEAD_DIM = 64
N_HEADS_A = 8
N_KV_A = 2
GROUP_A = N_HEADS_A // N_KV_A
WINDOW_A = 128
N_HEADS_B = 8
DILATED_PATTERNS = ((128, 1), (512, 4), (2048, 16))
WIDTH_A = N_HEADS_A * HEAD_DIM
WIDTH_B = N_HEADS_B * HEAD_DIM
KV_WIDTH_A = N_KV_A * HEAD_DIM
NUM_BUCKETS = 32
MAX_DISTANCE = 1024
EPS = 1e-6
NEG = -1e30
LOG2E = 1.4426950408889634

LANES = 128
SUBLANES = 8
F32 = jnp.float32
BF16 = jnp.bfloat16

_VMEM_LIMIT = 56 * 1024 * 1024
_BLOCKS_PER_TRIP = 8
_ROW_CHAINS = 2


def _rms(x, g):
    ms = jnp.mean(x * x, axis=-1, keepdims=True)
    return (x * lax.rsqrt(ms + EPS)) * g


def _t5_bucket_np(rel):
    half = NUM_BUCKETS // 2
    max_exact = half // 2
    sign = np.where(rel > 0, half, 0)
    n = np.abs(rel)
    nf = np.maximum(n, 1).astype(np.float32)
    large = max_exact + (np.log(nf / np.float32(max_exact))
                         / np.float32(math.log(MAX_DISTANCE / max_exact))
                         * np.float32(half - max_exact)).astype(np.int32)
    large = np.minimum(large, half - 1)
    return (sign + np.where(n < max_exact, n, large)).astype(np.int32)


def _band_layout(q_len, k_len, offs, dil, half_window):
    center = q_len - 1 + max(offs)
    length = -(-(center + k_len) // LANES) * LANES
    rel = np.arange(length) - center
    idx = np.where(np.abs(rel) <= half_window, _t5_bucket_np(rel * dil), -1).astype(np.int32)
    return idx, tuple((off - center) % length for off in offs)


def _bias_kernel(tab_ref, idxa_ref, idxb_ref, outa_ref, outb_ref, *, shifts_a, shifts_b):
    def build(idx_ref, out_ref, head0, shifts):
        n_pat, _, length = idx_ref.shape
        _, n_heads, q_len, k_len = out_ref.shape
        for p in range(n_pat):
            idx = idx_ref[p]
            vec = jnp.full((n_heads, length), NEG, F32)
            for b in range(NUM_BUCKETS):
                vec = jnp.where(idx == b, tab_ref[head0:head0 + n_heads, b:b + 1], vec)
            for h in range(n_heads):
                rows = jnp.broadcast_to(vec[h:h + 1, :], (q_len, length))
                for v, s in enumerate(shifts):
                    band = pltpu.roll(rows, s, 1, stride=1, stride_axis=0)
                    out_ref[p * len(shifts) + v, h] = band[:, :k_len]

    build(idxa_ref, outa_ref, 0, shifts_a)
    build(idxb_ref, outb_ref, N_HEADS_A, shifts_b)


def _bias(table_t, idx_a, shifts_a, shape_a, idx_b, shifts_b, shape_b):
    vmem = pl.BlockSpec(memory_space=pltpu.VMEM)
    return pl.pallas_call(
        functools.partial(_bias_kernel, shifts_a=shifts_a, shifts_b=shifts_b),
        in_specs=[vmem, vmem, vmem],
        out_specs=[vmem, vmem],
        out_shape=[jax.ShapeDtypeStruct(shape_a, F32), jax.ShapeDtypeStruct(shape_b, F32)],
        compiler_params=pltpu.CompilerParams(vmem_limit_bytes=_VMEM_LIMIT),
        name="bias",
    )(table_t, idx_a, idx_b)


def _inproj_kernel(x_ref, g_ref, w_ref, *refs, dils, stage_dils, n_chains):
    n_pat = len(dils)
    outs = refs[:3 + 3 * n_pat]
    scr = refs[3 + 3 * n_pat]
    stage_scr = dict(zip(stage_dils, refs[4 + 3 * n_pat:]))
    tc = x_ref.shape[0] // n_chains
    n_slab = WIDTH_B // LANES
    col_b = sum(o_ref.shape[-1] for o_ref in outs[:3])

    def norm(ci):
        return lambda st: st.update(
            u=_rms(x_ref[ci * tc:(ci + 1) * tc, :], g_ref[...]).astype(BF16))

    def dilated_tensor(ci, t):
        def stage(st):
            col = col_b + t * WIDTH_B
            seg = jnp.dot(st["u"], w_ref[:, col:col + WIDTH_B], preferred_element_type=F32)
            if any(d > 1 for d in dils):
                for c in range(n_slab):
                    scr[ci, t, c] = seg[:, c * LANES:(c + 1) * LANES]
            for pi, d in enumerate(dils):
                o_ref = outs[3 + 3 * pi + t]
                out_rows = slice(ci * tc // d, (ci + 1) * tc // d)
                if d == 1:
                    o_ref[out_rows, :] = seg.astype(BF16)
                    continue
                base = max([b for b in stage_dils if b < d and d % b == 0], default=1)
                step = d // base
                for r in range(d):
                    for c in range(n_slab):
                        rows = pl.ds(r // base, tc // d, stride=step)
                        src = (scr.at[ci, t, c] if base == 1
                               else stage_scr[base].at[ci, t, r % base, c])
                        val = src[rows, :]
                        if d in stage_scr:
                            stage_scr[d][ci, t, r, c] = val
                        lo = r * WIDTH_B + c * LANES
                        o_ref[out_rows, lo:lo + LANES] = val.astype(BF16)
        return stage

    def windowed_tensors(ci):
        def stage(st):
            col = 0
            for o_ref in outs[:3]:
                width = o_ref.shape[-1]
                o_ref[ci * tc:(ci + 1) * tc, :] = jnp.dot(
                    st["u"], w_ref[:, col:col + width], preferred_element_type=F32).astype(BF16)
                col += width
        return stage

    states = [{} for _ in range(n_chains)]
    for make in ([norm] + [functools.partial(dilated_tensor, t=t) for t in range(3)]
                 + [windowed_tensors]):
        for ci, st in enumerate(states):
            make(ci)(st)


def _inproj(x2d, g, w, tm, dils):
    n, d_model = x2d.shape
    shapes = [(n, WIDTH_A), (n, KV_WIDTH_A), (n, KV_WIDTH_A)]
    blocks = [(tm, WIDTH_A), (tm, KV_WIDTH_A), (tm, KV_WIDTH_A)]
    tc = tm // _ROW_CHAINS
    for d in dils:
        assert tc % (16 * d) == 0 and n % d == 0
        shapes += [(n // d, d * WIDTH_B)] * 3
        blocks += [(tm // d, d * WIDTH_B)] * 3
    stage_dils = tuple(b for b in dils if b > 1 and any(d > b and d % b == 0 for d in dils))
    n_slab = WIDTH_B // LANES
    return pl.pallas_call(
        functools.partial(_inproj_kernel, dils=dils, stage_dils=stage_dils,
                          n_chains=_ROW_CHAINS),
        grid=(n // tm,),
        in_specs=[
            pl.BlockSpec((tm, d_model), lambda i: (i, 0)),
            pl.BlockSpec((1, d_model), lambda i: (0, 0)),
            pl.BlockSpec(w.shape, lambda i: (0, 0)),
        ],
        out_specs=[pl.BlockSpec(bs, lambda i: (i, 0)) for bs in blocks],
        out_shape=[jax.ShapeDtypeStruct(sh, BF16) for sh in shapes],
        scratch_shapes=[pltpu.VMEM((_ROW_CHAINS, 3, n_slab, tc, LANES), F32)]
                       + [pltpu.VMEM((_ROW_CHAINS, 3, b, n_slab, tc // b, LANES), F32)
                          for b in stage_dils],
        compiler_params=pltpu.CompilerParams(
            dimension_semantics=("parallel",), vmem_limit_bytes=_VMEM_LIMIT),
        name="inproj",
    )(x2d, g, w)


def _nt_dot(a, b):
    return lax.dot_general(a, b, (((1,), (1,)), ((), ())), preferred_element_type=F32)


def _head_softmax_pv(qm, k_win, v_aug, bias, sink):
    s = _nt_dot(qm, k_win) + bias
    m = jnp.max(s, axis=-1, keepdims=True)
    if sink is not None:
        m = jnp.maximum(m, sink)
    e = jnp.exp2(s - m)
    o = jnp.dot(e.astype(BF16), v_aug, preferred_element_type=F32)
    num = o[:, :LANES]
    den = o[:, LANES:]
    if sink is not None:
        den = den + jnp.exp2(sink - m)
    return num, den, m


def _win_kernel(sink_ref, q_ref, k_ref, v_ref, bias_ref, o_ref, den_ref, *, seq, blk, unroll):
    nblk = seq // blk
    lane = lax.broadcasted_iota(jnp.int32, (blk, LANES), 1)
    low = lane < HEAD_DIM

    klen = 3 * blk

    def block(n, c):
        q0 = pl.multiple_of(n * blk, blk)
        k0 = pl.multiple_of(jnp.clip(q0 - blk, 0, seq - klen), blk)
        variant = jnp.where(n == 0, 0, jnp.where(n == nblk - 1, 2, 1))
        k_win = k_ref[pl.ds(k0, klen), :]
        v_win = v_ref[pl.ds(k0, klen), :]
        v_aug = jnp.concatenate([v_win, jnp.ones_like(v_win)], axis=1)
        den_acc = jnp.zeros((blk, LANES), F32)
        for j in range(GROUP_A):
            q_tile = q_ref[pl.ds(q0, blk), j * LANES:(j + 1) * LANES]
            res = []
            for hf in range(N_KV_A):
                h = hf * GROUP_A + j
                qm = jnp.where(low if hf == 0 else ~low, q_tile, jnp.zeros_like(q_tile))
                num, den, _ = _head_softmax_pv(
                    qm, k_win, v_aug, bias_ref[variant, h], sink_ref[h])
                res.append(num)
                den_acc = jnp.where(lane == N_KV_A * j + hf, den, den_acc)
            o_ref[pl.ds(q0, blk), j * LANES:(j + 1) * LANES] = (
                jnp.where(low, res[0], res[1]).astype(o_ref.dtype))
        den_ref[pl.ds(q0, blk), :] = den_acc
        return c

    lax.fori_loop(0, nblk, block, 0, unroll=unroll)


def _windowed(qa, ka, va, bias, sink):
    b, s, _ = qa.shape
    blk = WINDOW_A
    nblk = s // blk
    assert s % blk == 0 and nblk >= 3
    kern = functools.partial(_win_kernel, seq=s, blk=blk,
                             unroll=math.gcd(nblk, _BLOCKS_PER_TRIP))
    return pl.pallas_call(
        kern,
        grid=(b,),
        in_specs=[
            pl.BlockSpec(memory_space=pltpu.SMEM),
            pl.BlockSpec((None, s, WIDTH_A), lambda i: (i, 0, 0)),
            pl.BlockSpec((None, s, KV_WIDTH_A), lambda i: (i, 0, 0)),
            pl.BlockSpec((None, s, KV_WIDTH_A), lambda i: (i, 0, 0)),
            pl.BlockSpec(bias.shape, lambda i: (0, 0, 0, 0)),
        ],
        out_specs=[pl.BlockSpec((None, s, WIDTH_A), lambda i: (i, 0, 0)),
                   pl.BlockSpec((None, s, LANES), lambda i: (i, 0, 0))],
        out_shape=[jax.ShapeDtypeStruct((b, s, WIDTH_A), BF16),
                   jax.ShapeDtypeStruct((b, s, LANES), F32)],
        compiler_params=pltpu.CompilerParams(
            dimension_semantics=("parallel",), vmem_limit_bytes=_VMEM_LIMIT),
        name="win_gqa",
    )(sink, qa, ka, va, bias)


def _dil_kernel(q_ref, k_ref, v_ref, bias_ref, o_ref, st_ref, *, ls, half, n_res, unroll):
    qb = 2 * half
    kw = 4 * half
    nblk = ls // qb
    lane = lax.broadcasted_iota(jnp.int32, (qb, LANES), 1)
    low = lane < HEAD_DIM

    def block(n, res_i):
        q0 = pl.multiple_of(n * qb, qb)
        k0 = pl.multiple_of(jnp.clip(q0 - half, 0, ls - kw), half)
        variant = jnp.where(n == 0, 0, jnp.where(n == nblk - 1, 2, 1))
        st_acc = jnp.zeros((qb, LANES), F32)
        for j in range(N_HEADS_B // 2):
            cols = slice(res_i * WIDTH_B + j * LANES, res_i * WIDTH_B + (j + 1) * LANES)
            k_win = k_ref[pl.ds(k0, kw), cols]
            v_win = v_ref[pl.ds(k0, kw), cols]
            v_aug = jnp.concatenate([v_win, jnp.ones_like(v_win)], axis=1)
            q_tile = q_ref[pl.ds(q0, qb), cols]
            res = []
            for hf in range(2):
                h = 2 * j + hf
                qm = jnp.where(low if hf == 0 else ~low, q_tile, jnp.zeros_like(q_tile))
                num, den, m = _head_softmax_pv(qm, k_win, v_aug, bias_ref[variant, h], None)
                res.append(num)
                st_acc = jnp.where(lane == h, m, jnp.where(lane == N_HEADS_B + h, den, st_acc))
            o_ref[pl.ds(q0, qb), cols] = jnp.where(low, res[0], res[1]).astype(o_ref.dtype)
        st_ref[pl.ds(q0, qb), res_i * LANES:(res_i + 1) * LANES] = st_acc

    def trip(n, c):
        for res_i in range(n_res):
            block(n, res_i)
        return c

    lax.fori_loop(0, nblk, trip, 0, unroll=unroll)


def _dilated(qd, kd, vd, bias, pattern, batch, dil, half):
    rows, w_all = qd.shape
    assert w_all == dil * WIDTH_B
    ls = rows // batch
    nblk = ls // (2 * half)
    assert ls % (2 * half) == 0 and nblk >= 2
    n_res = math.gcd(dil, max(1, _BLOCKS_PER_TRIP // nblk))
    unroll = math.gcd(nblk, max(1, _BLOCKS_PER_TRIP // n_res))
    view = lambda t: t.reshape(batch, ls, w_all)
    kern = functools.partial(_dil_kernel, ls=ls, half=half, n_res=n_res, unroll=unroll)
    spec = pl.BlockSpec((None, ls, n_res * WIDTH_B), lambda i, r: (i, 0, r))
    n_var = 3
    o, st = pl.pallas_call(
        kern,
        grid=(batch, dil // n_res),
        in_specs=[spec, spec, spec,
                  pl.BlockSpec((n_var,) + bias.shape[1:], lambda i, r: (pattern, 0, 0, 0))],
        out_specs=[spec, pl.BlockSpec((None, ls, n_res * LANES), lambda i, r: (i, 0, r))],
        out_shape=[jax.ShapeDtypeStruct((batch, ls, w_all), BF16),
                   jax.ShapeDtypeStruct((batch, ls, dil * LANES), F32)],
        compiler_params=pltpu.CompilerParams(
            dimension_semantics=("parallel", "parallel"), vmem_limit_bytes=_VMEM_LIMIT),
        name=f"dilated_d{dil}",
    )(view(qd), view(kd), view(vd), bias)
    return o.reshape(rows, w_all), st.reshape(rows, dil * LANES)


def _post_kernel(x_ref, p_ref, oa_ref, dena_ref, *refs, dils, ff_chunk, n_chains):
    n_pat = len(dils)
    ob_refs = refs[:n_pat]
    st_refs = refs[n_pat:2 * n_pat]
    (expand_ref, goa_ref, gob_ref, wo_ref, gpm_ref, gpre_ref, wup_ref, wdn_ref,
     gmlp_ref, wpp_ref, wpg_ref, bpg_ref, gple_ref, out_ref) = refs[2 * n_pat:2 * n_pat + 14]
    scratch = refs[2 * n_pat + 14:]
    tm = x_ref.shape[0]
    tc = tm // n_chains
    n_slab = WIDTH_B // LANES
    head_lane = lax.broadcasted_iota(jnp.int32, (tc, LANES), 1) < N_HEADS_B

    def expand(w):
        w = jnp.where(head_lane, w, 0.0)
        hi = w.astype(BF16)
        lo = (w - hi.astype(F32)).astype(BF16)
        return jnp.dot(jnp.concatenate([hi, lo], axis=1), expand_ref[...],
                       preferred_element_type=F32)

    si = 0
    sources = []
    for d, o_ref, s_ref in zip(dils, ob_refs, st_refs):
        if d == 1:
            sources.append((None, o_ref, s_ref))
            continue
        o_scr, s_scr = scratch[si], scratch[si + 1]
        si += 2
        for r in range(d):
            rows = pl.ds(r, tm // d, stride=d)
            for c in range(n_slab):
                lo = r * WIDTH_B + c * LANES
                o_scr[c, rows, :] = o_ref[:, lo:lo + LANES].astype(F32)
            s_scr[rows, :] = s_ref[:, r * LANES:(r + 1) * LANES]
        sources.append((o_scr, None, s_scr))

    def combine(st):
        rows = st["rows"]
        nums, stats = [], []
        for o_scr, o_ref, s_ref in sources:
            if o_scr is None:
                nums.append(o_ref[rows, :].astype(F32))
            else:
                nums.append(jnp.concatenate([o_scr[c, rows, :] for c in range(n_slab)], axis=1))
            stats.append(s_ref[rows, :])
        mx = functools.reduce(jnp.maximum, stats)
        scales = [jnp.exp2(s - mx) for s in stats]
        dens = [pltpu.roll(s, LANES - N_HEADS_B, 1) for s in stats]
        tot = functools.reduce(lambda a, b: a + b, [sc * dn for sc, dn in zip(scales, dens)])
        ob = None
        for sc, num in zip(scales, nums):
            term = expand(sc / tot) * num
            ob = term if ob is None else ob + term
        oa = expand(1.0 / dena_ref[rows, :]) * oa_ref[rows, :].astype(F32)
        st["oa"] = _rms(oa, goa_ref[...]).astype(BF16)
        st["ob"] = _rms(ob, gob_ref[...]).astype(BF16)

    def mix(st):
        m = (jnp.dot(st.pop("oa"), wo_ref[:WIDTH_A, :], preferred_element_type=F32)
             + jnp.dot(st.pop("ob"), wo_ref[WIDTH_A:, :], preferred_element_type=F32))
        st["h"] = x_ref[st["rows"], :] + _rms(m, gpm_ref[...])
        st["v"] = _rms(st["h"], gpre_ref[...]).astype(BF16)
        st["ff"] = None

    def mlp_chunk(c):
        def stage(st):
            a = jnp.maximum(
                jnp.dot(st["v"], wup_ref[:, c:c + ff_chunk], preferred_element_type=F32), 0.0)
            t = jnp.dot((a * a).astype(BF16), wdn_ref[c:c + ff_chunk, :],
                        preferred_element_type=F32)
            st["ff"] = t if st["ff"] is None else st["ff"] + t
        return stage

    def ple(st):
        h = st["h"] + _rms(st.pop("ff"), gmlp_ref[...])
        gate = jax.nn.sigmoid(
            jnp.dot(h.astype(BF16), wpg_ref[...], preferred_element_type=F32) + bpg_ref[...])
        emb = jnp.dot(p_ref[st["rows"], :].astype(BF16), wpp_ref[...],
                      preferred_element_type=F32)
        out_ref[st["rows"], :] = h + _rms(gate * emb, gple_ref[...])

    stages = [combine, mix] + [mlp_chunk(c) for c in range(0, wup_ref.shape[1], ff_chunk)] + [ple]
    states = [{"rows": slice(ci * tc, (ci + 1) * tc)} for ci in range(n_chains)]
    for stage in stages:
        for st in states:
            stage(st)


def _post(x2d, p2d, oa, den_a, obs, stats, dils, consts, tm, ff_chunk):
    n, d_model = x2d.shape
    row = lambda width: pl.BlockSpec((tm, width), lambda i: (i, 0))
    const = lambda a: pl.BlockSpec(a.shape, lambda i: (0,) * a.ndim,
                                   pipeline_mode=pl.Buffered(1))
    scratch = []
    for d in dils:
        if d > 1:
            scratch += [pltpu.VMEM((WIDTH_B // LANES, tm, LANES), F32),
                        pltpu.VMEM((tm, LANES), F32)]
    return pl.pallas_call(
        functools.partial(_post_kernel, dils=dils, ff_chunk=ff_chunk, n_chains=_ROW_CHAINS),
        grid=(n // tm,),
        in_specs=[row(d_model), row(p2d.shape[1]), row(WIDTH_A), row(LANES)]
                 + [pl.BlockSpec((tm // d, d * WIDTH_B), lambda i: (i, 0)) for d in dils]
                 + [pl.BlockSpec((tm // d, d * LANES), lambda i: (i, 0)) for d in dils]
                 + [const(a) for a in consts],
        out_specs=row(d_model),
        out_shape=jax.ShapeDtypeStruct((n, d_model), x2d.dtype),
        scratch_shapes=scratch,
        compiler_params=pltpu.CompilerParams(
            dimension_semantics=("parallel",), vmem_limit_bytes=_VMEM_LIMIT),
        name="post",
    )(x2d, p2d, oa, den_a, *obs, *stats, *consts)


def kernel(x, p, rel_bias_table, g_pre_mix, w_in, sink_a, g_out_a, g_out_b, w_o, g_post_mix,
           g_pre_mlp, w_up, w_down, g_post_mlp, w_ple_proj, w_ple_gate, b_ple_gate, g_post_ple):
    b, s, d_model = x.shape
    depth = w_in.shape[0]
    n = b * s
    tm = 512
    dils = tuple(dil for _, dil in DILATED_PATTERNS)

    perm_heads = np.arange(N_HEADS_A).reshape(N_KV_A, GROUP_A).T.reshape(-1)
    perm_cols = (perm_heads[:, None] * HEAD_DIM + np.arange(HEAD_DIM)[None, :]).reshape(-1)

    idx_a, shifts_a = _band_layout(WINDOW_A, 3 * WINDOW_A, (0, WINDOW_A, 2 * WINDOW_A),
                                   1, WINDOW_A)
    half = DILATED_PATTERNS[0][0] // (2 * DILATED_PATTERNS[0][1])
    idx_b = []
    for window, dil in DILATED_PATTERNS:
        assert window // (2 * dil) == half
        idx, shifts_b = _band_layout(2 * half, 4 * half, (0, half, 2 * half), dil, half)
        idx_b.append(idx)
    bias_a, bias_b = _bias(
        rel_bias_table.T.astype(F32) * LOG2E,
        jnp.asarray(idx_a)[None, None], shifts_a,
        (len(shifts_a), N_HEADS_A, WINDOW_A, 3 * WINDOW_A),
        jnp.asarray(np.stack(idx_b))[:, None], shifts_b,
        (len(DILATED_PATTERNS) * len(shifts_b), N_HEADS_B, 2 * half, 4 * half))

    expand_np = np.zeros((2 * LANES, WIDTH_B), np.float32)
    for hh in range(N_HEADS_B):
        expand_np[hh, hh * HEAD_DIM:(hh + 1) * HEAD_DIM] = 1.0
        expand_np[LANES + hh, hh * HEAD_DIM:(hh + 1) * HEAD_DIM] = 1.0
    expand = jnp.asarray(expand_np, BF16)

    h2d = x.reshape(n, d_model)
    row = lambda a: a.reshape(1, -1).astype(F32)
    for i in range(depth):
        scale = HEAD_DIM ** -0.5 * LOG2E
        w = w_in[i]
        o0 = WIDTH_A + 2 * KV_WIDTH_A
        w_all = jnp.concatenate([
            w[:, :WIDTH_A][:, perm_cols] * scale,
            w[:, WIDTH_A:o0],
            w[:, o0:o0 + WIDTH_B] * scale,
            w[:, o0 + WIDTH_B:],
        ], axis=1).astype(BF16)
        outs = _inproj(h2d, row(g_pre_mix[i]), w_all, tm, dils)
        qa, ka, va = (t.reshape(b, s, t.shape[-1]) for t in outs[:3])

        o_a, den_a = _windowed(qa, ka, va, bias_a, sink_a[i].astype(F32) * LOG2E)
        obs, stats = [], []
        for pi, (window, dil) in enumerate(DILATED_PATTERNS):
            qd, kd, vd = outs[3 + 3 * pi:6 + 3 * pi]
            o, st = _dilated(qd, kd, vd, bias_b, pi, b, dil, window // (2 * dil))
            obs.append(o)
            stats.append(st)

        wo = jnp.concatenate([w_o[i][:WIDTH_A][perm_cols], w_o[i][WIDTH_A:]], axis=0).astype(BF16)
        consts = (expand, row(g_out_a[i][perm_cols]), row(g_out_b[i]), wo, row(g_post_mix[i]),
                  row(g_pre_mlp[i]), w_up[i].astype(BF16), w_down[i].astype(BF16),
                  row(g_post_mlp[i]), w_ple_proj[i].astype(BF16), w_ple_gate[i].astype(BF16),
                  row(b_ple_gate[i]), row(g_post_ple[i]))
        h2d = _post(h2d, p[i].reshape(n, -1), o_a.reshape(n, WIDTH_A), den_a.reshape(n, LANES),
                    obs, stats, dils, consts, tm, 2048)
    return h2d.reshape(b, s, d_model)
```

```python
import functools
import math

import jax
import jax.numpy as jnp
import numpy as np
from jax import lax
from jax.experimental import pallas as pl
from jax.experimental.pallas import tpu as pltpu

HEAD_DIM = 64
N_HEADS_A = 8
N_KV_A = 2
GROUP_A = N_HEADS_A // N_KV_A
WINDOW_A = 128
N_HEADS_B = 8
DILATED_PATTERNS = ((128, 1), (512, 4), (2048, 16))
WIDTH_A = N_HEADS_A * HEAD_DIM
WIDTH_B = N_HEADS_B * HEAD_DIM
KV_WIDTH_A = N_KV_A * HEAD_DIM
NUM_BUCKETS = 32
MAX_DISTANCE = 1024
EPS = 1e-6
NEG = -1e30
LOG2E = 1.4426950408889634

LANES = 128
SUBLANES = 8
F32 = jnp.float32
BF16 = jnp.bfloat16

_VMEM_LIMIT = 56 * 1024 * 1024
_BLOCKS_PER_TRIP = 16
_ROW_CHAINS = 2


def _rms(x, g):
    ms = jnp.mean(x * x, axis=-1, keepdims=True)
    return (x * lax.rsqrt(ms + EPS)) * g


def _t5_bucket_np(rel):
    half = NUM_BUCKETS // 2
    max_exact = half // 2
    sign = np.where(rel > 0, half, 0)
    n = np.abs(rel)
    nf = np.maximum(n, 1).astype(np.float32)
    large = max_exact + (np.log(nf / np.float32(max_exact))
                         / np.float32(math.log(MAX_DISTANCE / max_exact))
                         * np.float32(half - max_exact)).astype(np.int32)
    large = np.minimum(large, half - 1)
    return (sign + np.where(n < max_exact, n, large)).astype(np.int32)


def _band_layout(q_len, k_len, offs, dil, half_window):
    center = q_len - 1 + max(offs)
    length = -(-(center + k_len) // LANES) * LANES
    rel = np.arange(length) - center
    idx = np.where(np.abs(rel) <= half_window, _t5_bucket_np(rel * dil), -1).astype(np.int32)
    return idx, tuple((off - center) % length for off in offs)


def _bias_kernel(tab_ref, idxa_ref, idxb_ref, outa_ref, outb_ref, *, shifts_a, shifts_b):
    def build(idx_ref, out_ref, head0, shifts):
        n_pat, _, length = idx_ref.shape
        _, n_heads, q_len, k_len = out_ref.shape
        for p in range(n_pat):
            idx = idx_ref[p]
            vec = jnp.full((n_heads, length), NEG, F32)
            for b in range(NUM_BUCKETS):
                vec = jnp.where(idx == b, tab_ref[head0:head0 + n_heads, b:b + 1], vec)
            for h in range(n_heads):
                rows = jnp.broadcast_to(vec[h:h + 1, :], (q_len, length))
                for v, s in enumerate(shifts):
                    band = pltpu.roll(rows, s, 1, stride=1, stride_axis=0)
                    out_ref[p * len(shifts) + v, h] = band[:, :k_len]

    build(idxa_ref, outa_ref, 0, shifts_a)
    build(idxb_ref, outb_ref, N_HEADS_A, shifts_b)


def _bias(table_t, idx_a, shifts_a, shape_a, idx_b, shifts_b, shape_b):
    vmem = pl.BlockSpec(memory_space=pltpu.VMEM)
    return pl.pallas_call(
        functools.partial(_bias_kernel, shifts_a=shifts_a, shifts_b=shifts_b),
        in_specs=[vmem, vmem, vmem],
        out_specs=[vmem, vmem],
        out_shape=[jax.ShapeDtypeStruct(shape_a, F32), jax.ShapeDtypeStruct(shape_b, F32)],
        compiler_params=pltpu.CompilerParams(vmem_limit_bytes=_VMEM_LIMIT),
        name="bias",
    )(table_t, idx_a, idx_b)


def _inproj_kernel(x_ref, g_ref, w_ref, *refs, dils, stage_dils, n_chains):
    n_pat = len(dils)
    outs = refs[:3 + 3 * n_pat]
    scr = refs[3 + 3 * n_pat]
    stage_scr = dict(zip(stage_dils, refs[4 + 3 * n_pat:]))
    tc = x_ref.shape[0] // n_chains
    n_slab = WIDTH_B // LANES
    col_b = sum(o_ref.shape[-1] for o_ref in outs[:3])

    def norm(ci):
        return lambda st: st.update(
            u=_rms(x_ref[ci * tc:(ci + 1) * tc, :], g_ref[...]).astype(BF16))

    def dilated_tensor(ci, t):
        def stage(st):
            col = col_b + t * WIDTH_B
            seg = jnp.dot(st["u"], w_ref[:, col:col + WIDTH_B], preferred_element_type=F32)
            if any(d > 1 for d in dils):
                for c in range(n_slab):
                    scr[ci, t, c] = seg[:, c * LANES:(c + 1) * LANES]
            for pi, d in enumerate(dils):
                o_ref = outs[3 + 3 * pi + t]
                out_rows = slice(ci * tc // d, (ci + 1) * tc // d)
                if d == 1:
                    o_ref[out_rows, :] = seg.astype(BF16)
                    continue
                base = max([b for b in stage_dils if b < d and d % b == 0], default=1)
                step = d // base
                for r in range(d):
                    for c in range(n_slab):
                        rows = pl.ds(r // base, tc // d, stride=step)
                        src = (scr.at[ci, t, c] if base == 1
                               else stage_scr[base].at[ci, t, r % base, c])
                        val = src[rows, :]
                        if d in stage_scr:
                            stage_scr[d][ci, t, r, c] = val
                        lo = r * WIDTH_B + c * LANES
                        o_ref[out_rows, lo:lo + LANES] = val.astype(BF16)
        return stage

    def windowed_tensors(ci):
        def stage(st):
            col = 0
            for o_ref in outs[:3]:
                width = o_ref.shape[-1]
                o_ref[ci * tc:(ci + 1) * tc, :] = jnp.dot(
                    st["u"], w_ref[:, col:col + width], preferred_element_type=F32).astype(BF16)
                col += width
        return stage

    states = [{} for _ in range(n_chains)]
    for make in ([norm] + [functools.partial(dilated_tensor, t=t) for t in range(3)]
                 + [windowed_tensors]):
        for ci, st in enumerate(states):
            make(ci)(st)


def _inproj(x2d, g, w, tm, dils):
    n, d_model = x2d.shape
    shapes = [(n, WIDTH_A), (n, KV_WIDTH_A), (n, KV_WIDTH_A)]
    blocks = [(tm, WIDTH_A), (tm, KV_WIDTH_A), (tm, KV_WIDTH_A)]
    n_chains = 1
    tc = tm // n_chains
    for d in dils:
        assert tc % (16 * d) == 0 and n % d == 0
        shapes += [(n // d, d * WIDTH_B)] * 3
        blocks += [(tm // d, d * WIDTH_B)] * 3
    stage_dils = tuple(b for b in dils if b > 1 and any(d > b and d % b == 0 for d in dils))
    n_slab = WIDTH_B // LANES
    return pl.pallas_call(
        functools.partial(_inproj_kernel, dils=dils, stage_dils=stage_dils, n_chains=n_chains),
        grid=(n // tm,),
        in_specs=[
            pl.BlockSpec((tm, d_model), lambda i: (i, 0)),
            pl.BlockSpec((1, d_model), lambda i: (0, 0)),
            pl.BlockSpec(w.shape, lambda i: (0, 0)),
        ],
        out_specs=[pl.BlockSpec(bs, lambda i: (i, 0)) for bs in blocks],
        out_shape=[jax.ShapeDtypeStruct(sh, BF16) for sh in shapes],
        scratch_shapes=[pltpu.VMEM((n_chains, 3, n_slab, tc, LANES), F32)]
                       + [pltpu.VMEM((n_chains, 3, b, n_slab, tc // b, LANES), F32)
                          for b in stage_dils],
        compiler_params=pltpu.CompilerParams(
            dimension_semantics=("parallel",), vmem_limit_bytes=_VMEM_LIMIT),
        name="inproj",
    )(x2d, g, w)


def _nt_dot(a, b):
    return lax.dot_general(a, b, (((1,), (1,)), ((), ())), preferred_element_type=F32)


def _head_softmax_pv(qm, k_win, v_aug, bias, sink):
    s = _nt_dot(qm, k_win) + bias
    m = jnp.max(s, axis=-1, keepdims=True)
    if sink is not None:
        m = jnp.maximum(m, sink)
    e = jnp.exp2(s - m)
    o = jnp.dot(e.astype(BF16), v_aug, preferred_element_type=F32)
    num = o[:, :LANES]
    den = o[:, LANES:]
    if sink is not None:
        den = den + jnp.exp2(sink - m)
    return num, den, m


def _win_kernel(sink_ref, q_ref, k_ref, v_ref, bias_ref, o_ref, *, seq, blk, unroll):
    nblk = seq // blk
    lane = lax.broadcasted_iota(jnp.int32, (blk, LANES), 1)
    low = lane < HEAD_DIM

    klen = 3 * blk

    def block(n, c):
        q0 = pl.multiple_of(n * blk, blk)
        k0 = pl.multiple_of(jnp.clip(q0 - blk, 0, seq - klen), blk)
        variant = jnp.where(n == 0, 0, jnp.where(n == nblk - 1, 2, 1))
        k_win = k_ref[pl.ds(k0, klen), :]
        v_win = v_ref[pl.ds(k0, klen), :]
        v_aug = jnp.concatenate([v_win, jnp.ones_like(v_win)], axis=1)
        for j in range(GROUP_A):
            q_tile = q_ref[pl.ds(q0, blk), j * LANES:(j + 1) * LANES]
            nums, dens = [], []
            for hf in range(N_KV_A):
                h = hf * GROUP_A + j
                qm = jnp.where(low if hf == 0 else ~low, q_tile, jnp.zeros_like(q_tile))
                num, den, _ = _head_softmax_pv(
                    qm, k_win, v_aug, bias_ref[variant, h], sink_ref[h])
                nums.append(num)
                dens.append(den)
            o_ref[pl.ds(q0, blk), j * LANES:(j + 1) * LANES] = (
                jnp.where(low, nums[0], nums[1]) / jnp.where(low, dens[0], dens[1])
            ).astype(o_ref.dtype)
        return c

    lax.fori_loop(0, nblk, block, 0, unroll=unroll)


def _windowed(qa, ka, va, bias, sink):
    b, s, _ = qa.shape
    blk = WINDOW_A
    nblk = s // blk
    assert s % blk == 0 and nblk >= 3
    kern = functools.partial(_win_kernel, seq=s, blk=blk,
                             unroll=math.gcd(nblk, _BLOCKS_PER_TRIP))
    return pl.pallas_call(
        kern,
        grid=(b,),
        in_specs=[
            pl.BlockSpec(memory_space=pltpu.SMEM),
            pl.BlockSpec((None, s, WIDTH_A), lambda i: (i, 0, 0)),
            pl.BlockSpec((None, s, KV_WIDTH_A), lambda i: (i, 0, 0)),
            pl.BlockSpec((None, s, KV_WIDTH_A), lambda i: (i, 0, 0)),
            pl.BlockSpec(bias.shape, lambda i: (0, 0, 0, 0)),
        ],
        out_specs=pl.BlockSpec((None, s, WIDTH_A), lambda i: (i, 0, 0)),
        out_shape=jax.ShapeDtypeStruct((b, s, WIDTH_A), BF16),
        compiler_params=pltpu.CompilerParams(
            dimension_semantics=("parallel",), vmem_limit_bytes=_VMEM_LIMIT),
        name="win_gqa",
    )(sink, qa, ka, va, bias)


def _dil_kernel(q_ref, k_ref, v_ref, bias_ref, o_ref, st_ref, *, ls, half, n_res, unroll):
    qb = 2 * half
    kw = 4 * half
    nblk = ls // qb
    lane = lax.broadcasted_iota(jnp.int32, (qb, LANES), 1)
    low = lane < HEAD_DIM

    def block(n, res_i):
        q0 = pl.multiple_of(n * qb, qb)
        k0 = pl.multiple_of(jnp.clip(q0 - half, 0, ls - kw), half)
        variant = jnp.where(n == 0, 0, jnp.where(n == nblk - 1, 2, 1))
        st_acc = jnp.zeros((qb, LANES), F32)
        for j in range(N_HEADS_B // 2):
            cols = slice(res_i * WIDTH_B + j * LANES, res_i * WIDTH_B + (j + 1) * LANES)
            k_win = k_ref[pl.ds(k0, kw), cols]
            v_win = v_ref[pl.ds(k0, kw), cols]
            v_aug = jnp.concatenate([v_win, jnp.ones_like(v_win)], axis=1)
            q_tile = q_ref[pl.ds(q0, qb), cols]
            res = []
            for hf in range(2):
                h = 2 * j + hf
                qm = jnp.where(low if hf == 0 else ~low, q_tile, jnp.zeros_like(q_tile))
                num, den, m = _head_softmax_pv(qm, k_win, v_aug, bias_ref[variant, h], None)
                res.append(num)
                st_acc = jnp.where(lane == h, m, jnp.where(lane == N_HEADS_B + h, den, st_acc))
            o_ref[pl.ds(q0, qb), cols] = jnp.where(low, res[0], res[1]).astype(o_ref.dtype)
        st_ref[pl.ds(q0, qb), res_i * LANES:(res_i + 1) * LANES] = st_acc

    def trip(n, c):
        for res_i in range(n_res):
            block(n, res_i)
        return c

    lax.fori_loop(0, nblk, trip, 0, unroll=unroll)


def _dilated(qd, kd, vd, bias, pattern, batch, dil, half):
    rows, w_all = qd.shape
    assert w_all == dil * WIDTH_B
    ls = rows // batch
    nblk = ls // (2 * half)
    assert ls % (2 * half) == 0 and nblk >= 2
    n_res = math.gcd(dil, max(1, _BLOCKS_PER_TRIP // nblk))
    unroll = math.gcd(nblk, max(1, _BLOCKS_PER_TRIP // n_res))
    view = lambda t: t.reshape(batch, ls, w_all)
    kern = functools.partial(_dil_kernel, ls=ls, half=half, n_res=n_res, unroll=unroll)
    spec = pl.BlockSpec((None, ls, n_res * WIDTH_B), lambda i, r: (i, 0, r))
    n_var = 3
    o, st = pl.pallas_call(
        kern,
        grid=(batch, dil // n_res),
        in_specs=[spec, spec, spec,
                  pl.BlockSpec((n_var,) + bias.shape[1:], lambda i, r: (pattern, 0, 0, 0))],
        out_specs=[spec, pl.BlockSpec((None, ls, n_res * LANES), lambda i, r: (i, 0, r))],
        out_shape=[jax.ShapeDtypeStruct((batch, ls, w_all), BF16),
                   jax.ShapeDtypeStruct((batch, ls, dil * LANES), F32)],
        compiler_params=pltpu.CompilerParams(
            dimension_semantics=("parallel", "parallel"), vmem_limit_bytes=_VMEM_LIMIT),
        name=f"dilated_d{dil}",
    )(view(qd), view(kd), view(vd), bias)
    return o.reshape(rows, w_all), st.reshape(rows, dil * LANES)


def _post_kernel(x_ref, p_ref, oa_ref, *refs, dils, ff_chunk, n_chains):
    n_pat = len(dils)
    ob_refs = refs[:n_pat]
    st_refs = refs[n_pat:2 * n_pat]
    (expand_ref, goa_ref, gob_ref, wo_ref, gpm_ref, gpre_ref, wup_ref, wdn_ref,
     gmlp_ref, wpp_ref, wpg_ref, bpg_ref, gple_ref, out_ref) = refs[2 * n_pat:2 * n_pat + 14]
    scratch = refs[2 * n_pat + 14:]
    tm = x_ref.shape[0]
    tc = tm // n_chains
    n_slab = WIDTH_B // LANES
    head_lane = lax.broadcasted_iota(jnp.int32, (tc, LANES), 1) < N_HEADS_B

    def expand(w):
        w = jnp.where(head_lane, w, 0.0)
        hi = w.astype(BF16)
        lo = (w - hi.astype(F32)).astype(BF16)
        return jnp.dot(jnp.concatenate([hi, lo], axis=1), expand_ref[...],
                       preferred_element_type=F32)

    si = 0
    sources = []
    for d, o_ref, s_ref in zip(dils, ob_refs, st_refs):
        if d == 1:
            sources.append((None, o_ref, s_ref))
            continue
        o_scr, s_scr = scratch[si], scratch[si + 1]
        si += 2
        for r in range(d):
            rows = pl.ds(r, tm // d, stride=d)
            for c in range(n_slab):
                lo = r * WIDTH_B + c * LANES
                o_scr[c, rows, :] = o_ref[:, lo:lo + LANES].astype(F32)
            s_scr[rows, :] = s_ref[:, r * LANES:(r + 1) * LANES]
        sources.append((o_scr, None, s_scr))

    def combine(st):
        rows = st["rows"]
        nums, stats = [], []
        for o_scr, o_ref, s_ref in sources:
            if o_scr is None:
                nums.append(o_ref[rows, :].astype(F32))
            else:
                nums.append(jnp.concatenate([o_scr[c, rows, :] for c in range(n_slab)], axis=1))
            stats.append(s_ref[rows, :])
        mx = functools.reduce(jnp.maximum, stats)
        scales = [jnp.exp2(s - mx) for s in stats]
        dens = [pltpu.roll(s, LANES - N_HEADS_B, 1) for s in stats]
        tot = functools.reduce(lambda a, b: a + b, [sc * dn for sc, dn in zip(scales, dens)])
        ob = None
        for sc, num in zip(scales, nums):
            term = expand(sc / tot) * num
            ob = term if ob is None else ob + term
        st["oa"] = _rms(oa_ref[rows, :].astype(F32), goa_ref[...]).astype(BF16)
        st["ob"] = _rms(ob, gob_ref[...]).astype(BF16)

    def mix(st):
        m = (jnp.dot(st.pop("oa"), wo_ref[:WIDTH_A, :], preferred_element_type=F32)
             + jnp.dot(st.pop("ob"), wo_ref[WIDTH_A:, :], preferred_element_type=F32))
        st["h"] = x_ref[st["rows"], :] + _rms(m, gpm_ref[...])
        st["v"] = _rms(st["h"], gpre_ref[...]).astype(BF16)
        st["ff"] = None

    def mlp_chunk(c):
        def stage(st):
            a = jnp.maximum(
                jnp.dot(st["v"], wup_ref[:, c:c + ff_chunk], preferred_element_type=F32), 0.0)
            t = jnp.dot((a * a).astype(BF16), wdn_ref[c:c + ff_chunk, :],
                        preferred_element_type=F32)
            st["ff"] = t if st["ff"] is None else st["ff"] + t
        return stage

    def ple(st):
        h = st["h"] + _rms(st.pop("ff"), gmlp_ref[...])
        gate = jax.nn.sigmoid(
            jnp.dot(h.astype(BF16), wpg_ref[...], preferred_element_type=F32) + bpg_ref[...])
        emb = jnp.dot(p_ref[st["rows"], :].astype(BF16), wpp_ref[...],
                      preferred_element_type=F32)
        out_ref[st["rows"], :] = h + _rms(gate * emb, gple_ref[...])

    stages = [combine, mix] + [mlp_chunk(c) for c in range(0, wup_ref.shape[1], ff_chunk)] + [ple]
    states = [{"rows": slice(ci * tc, (ci + 1) * tc)} for ci in range(n_chains)]
    for stage in stages:
        for st in states:
            stage(st)


def _post(x2d, p2d, oa, obs, stats, dils, consts, tm, ff_chunk):
    n, d_model = x2d.shape
    row = lambda width: pl.BlockSpec((tm, width), lambda i: (i, 0))
    const = lambda a: pl.BlockSpec(a.shape, lambda i: (0,) * a.ndim,
                                   pipeline_mode=pl.Buffered(1))
    scratch = []
    for d in dils:
        if d > 1:
            scratch += [pltpu.VMEM((WIDTH_B // LANES, tm, LANES), F32),
                        pltpu.VMEM((tm, LANES), F32)]
    return pl.pallas_call(
        functools.partial(_post_kernel, dils=dils, ff_chunk=ff_chunk, n_chains=_ROW_CHAINS),
        grid=(n // tm,),
        in_specs=[row(d_model), row(p2d.shape[1]), row(WIDTH_A)]
                 + [pl.BlockSpec((tm // d, d * WIDTH_B), lambda i: (i, 0)) for d in dils]
                 + [pl.BlockSpec((tm // d, d * LANES), lambda i: (i, 0)) for d in dils]
                 + [const(a) for a in consts],
        out_specs=row(d_model),
        out_shape=jax.ShapeDtypeStruct((n, d_model), x2d.dtype),
        scratch_shapes=scratch,
        compiler_params=pltpu.CompilerParams(
            dimension_semantics=("parallel",), vmem_limit_bytes=_VMEM_LIMIT),
        name="post",
    )(x2d, p2d, oa, *obs, *stats, *consts)


def kernel(x, p, rel_bias_table, g_pre_mix, w_in, sink_a, g_out_a, g_out_b, w_o, g_post_mix,
           g_pre_mlp, w_up, w_down, g_post_mlp, w_ple_proj, w_ple_gate, b_ple_gate, g_post_ple):
    b, s, d_model = x.shape
    depth = w_in.shape[0]
    n = b * s
    tm = 512
    dils = tuple(dil for _, dil in DILATED_PATTERNS)

    def pair_heads(a, axis):
        shape = a.shape
        a = a.reshape(shape[:axis] + (N_KV_A, GROUP_A, HEAD_DIM) + shape[axis + 1:])
        return jnp.swapaxes(a, axis, axis + 1).reshape(shape)

    idx_a, shifts_a = _band_layout(WINDOW_A, 3 * WINDOW_A, (0, WINDOW_A, 2 * WINDOW_A),
                                   1, WINDOW_A)
    half = DILATED_PATTERNS[0][0] // (2 * DILATED_PATTERNS[0][1])
    idx_b = []
    for window, dil in DILATED_PATTERNS:
        assert window // (2 * dil) == half
        idx, shifts_b = _band_layout(2 * half, 4 * half, (0, half, 2 * half), dil, half)
        idx_b.append(idx)
    bias_a, bias_b = _bias(
        rel_bias_table.T.astype(F32) * LOG2E,
        jnp.asarray(idx_a)[None, None], shifts_a,
        (len(shifts_a), N_HEADS_A, WINDOW_A, 3 * WINDOW_A),
        jnp.asarray(np.stack(idx_b))[:, None], shifts_b,
        (len(DILATED_PATTERNS) * len(shifts_b), N_HEADS_B, 2 * half, 4 * half))

    expand_np = np.zeros((2 * LANES, WIDTH_B), np.float32)
    for hh in range(N_HEADS_B):
        expand_np[hh, hh * HEAD_DIM:(hh + 1) * HEAD_DIM] = 1.0
        expand_np[LANES + hh, hh * HEAD_DIM:(hh + 1) * HEAD_DIM] = 1.0
    expand = jnp.asarray(expand_np, BF16)

    h2d = x.reshape(n, d_model)
    row = lambda a: a.reshape(1, -1).astype(F32)
    for i in range(depth):
        scale = HEAD_DIM ** -0.5 * LOG2E
        w = w_in[i]
        o0 = WIDTH_A + 2 * KV_WIDTH_A
        w_all = jnp.concatenate([
            pair_heads(w[:, :WIDTH_A], 1) * scale,
            w[:, WIDTH_A:o0],
            w[:, o0:o0 + WIDTH_B] * scale,
            w[:, o0 + WIDTH_B:],
        ], axis=1).astype(BF16)
        outs = _inproj(h2d, row(g_pre_mix[i]), w_all, tm, dils)
        qa, ka, va = (t.reshape(b, s, t.shape[-1]) for t in outs[:3])

        o_a = _windowed(qa, ka, va, bias_a, sink_a[i].astype(F32) * LOG2E)
        obs, stats = [], []
        for pi, (window, dil) in enumerate(DILATED_PATTERNS):
            qd, kd, vd = outs[3 + 3 * pi:6 + 3 * pi]
            o, st = _dilated(qd, kd, vd, bias_b, pi, b, dil, window // (2 * dil))
            obs.append(o)
            stats.append(st)

        wo = jnp.concatenate([pair_heads(w_o[i][:WIDTH_A], 0), w_o[i][WIDTH_A:]],
                             axis=0).astype(BF16)
        consts = (expand, row(pair_heads(g_out_a[i], 0)), row(g_out_b[i]), wo, row(g_post_mix[i]),
                  row(g_pre_mlp[i]), w_up[i].astype(BF16), w_down[i].astype(BF16),
                  row(g_post_mlp[i]), w_ple_proj[i].astype(BF16), w_ple_gate[i].astype(BF16),
                  row(b_ple_gate[i]), row(g_post_ple[i]))
        h2d = _post(h2d, p[i].reshape(n, -1), o_a.reshape(n, WIDTH_A), obs, stats, dils,
                    consts, tm, 2048)
    return h2d.reshape(b, s, d_model)
```

```python
import functools
import math

import jax
import jax.numpy as jnp
import numpy as np
from jax import lax
from jax.experimental import pallas as pl
from jax.experimental.pallas import tpu as pltpu

HEAD_DIM = 64
N_HEADS_A = 8
N_KV_A = 2
GROUP_A = N_HEADS_A // N_KV_A
WINDOW_A = 128
N_HEADS_B = 8
DILATED_PATTERNS = ((128, 1), (512, 4), (2048, 16))
WIDTH_A = N_HEADS_A * HEAD_DIM
WIDTH_B = N_HEADS_B * HEAD_DIM
KV_WIDTH_A = N_KV_A * HEAD_DIM
NUM_BUCKETS = 32
MAX_DISTANCE = 1024
EPS = 1e-6
NEG = -1e30
LOG2E = 1.4426950408889634

LANES = 128
SUBLANES = 8
F32 = jnp.float32
BF16 = jnp.bfloat16

_VMEM_LIMIT = 56 * 1024 * 1024
_BLOCKS_PER_TRIP = 16
_ROW_CHAINS = (1, 1)


def _rms(x, g):
    ms = jnp.mean(x * x, axis=-1, keepdims=True)
    return (x * lax.rsqrt(ms + EPS)) * g


def _t5_bucket_np(rel):
    half = NUM_BUCKETS // 2
    max_exact = half // 2
    sign = np.where(rel > 0, half, 0)
    n = np.abs(rel)
    nf = np.maximum(n, 1).astype(np.float32)
    large = max_exact + (np.log(nf / np.float32(max_exact))
                         / np.float32(math.log(MAX_DISTANCE / max_exact))
                         * np.float32(half - max_exact)).astype(np.int32)
    large = np.minimum(large, half - 1)
    return (sign + np.where(n < max_exact, n, large)).astype(np.int32)


def _band_layout(q_len, k_len, offs, dil, half_window):
    center = q_len - 1 + max(offs)
    length = -(-(center + k_len) // LANES) * LANES
    rel = np.arange(length) - center
    idx = np.where(np.abs(rel) <= half_window, _t5_bucket_np(rel * dil), -1).astype(np.int32)
    return idx, tuple((off - center) % length for off in offs)


def _bias_kernel(tab_ref, idxa_ref, idxb_ref, outa_ref, outb_ref, *, shifts_a, shifts_b):
    def build(idx_ref, out_ref, head0, shifts):
        n_pat, _, length = idx_ref.shape
        _, n_heads, q_len, k_len = out_ref.shape
        for p in range(n_pat):
            idx = idx_ref[p]
            vec = jnp.full((n_heads, length), NEG, F32)
            for b in range(NUM_BUCKETS):
                vec = jnp.where(idx == b, tab_ref[head0:head0 + n_heads, b:b + 1], vec)
            for h in range(n_heads):
                rows = jnp.broadcast_to(vec[h:h + 1, :], (q_len, length))
                for v, s in enumerate(shifts):
                    band = pltpu.roll(rows, s, 1, stride=1, stride_axis=0)
                    out_ref[p * len(shifts) + v, h] = band[:, :k_len]

    build(idxa_ref, outa_ref, 0, shifts_a)
    build(idxb_ref, outb_ref, N_HEADS_A, shifts_b)


def _bias(table_t, idx_a, shifts_a, shape_a, idx_b, shifts_b, shape_b):
    vmem = pl.BlockSpec(memory_space=pltpu.VMEM)
    return pl.pallas_call(
        functools.partial(_bias_kernel, shifts_a=shifts_a, shifts_b=shifts_b),
        in_specs=[vmem, vmem, vmem],
        out_specs=[vmem, vmem],
        out_shape=[jax.ShapeDtypeStruct(shape_a, F32), jax.ShapeDtypeStruct(shape_b, F32)],
        compiler_params=pltpu.CompilerParams(vmem_limit_bytes=_VMEM_LIMIT),
        name="bias",
    )(table_t, idx_a, idx_b)


def _inproj_kernel(x_ref, g_ref, w_ref, *refs, dils, stage_dils, n_chains):
    n_pat = len(dils)
    outs = refs[:3 + 3 * n_pat]
    scr = refs[3 + 3 * n_pat]
    stage_scr = dict(zip(stage_dils, refs[4 + 3 * n_pat:]))
    tc = x_ref.shape[0] // n_chains
    n_slab = WIDTH_B // LANES
    col_b = sum(o_ref.shape[-1] for o_ref in outs[:3])

    def norm(ci):
        def stage(st):
            x = x_ref[ci * tc:(ci + 1) * tc, :]
            st["u"] = (x * g_ref[...]).astype(BF16)
            st["rstd"] = lax.rsqrt(jnp.mean(x * x, axis=-1, keepdims=True) + EPS)
        return stage

    def dilated_tensor(ci, t):
        def stage(st):
            col = col_b + t * WIDTH_B
            seg = st["rstd"] * jnp.dot(st["u"], w_ref[:, col:col + WIDTH_B],
                                       preferred_element_type=F32)
            if any(d > 1 for d in dils):
                for c in range(n_slab):
                    scr[ci, t, c] = seg[:, c * LANES:(c + 1) * LANES]
            for pi, d in enumerate(dils):
                o_ref = outs[3 + 3 * pi + t]
                out_rows = slice(ci * tc // d, (ci + 1) * tc // d)
                if d == 1:
                    o_ref[out_rows, :] = seg.astype(BF16)
                    continue
                base = max([b for b in stage_dils if b < d and d % b == 0], default=1)
                step = d // base
                for r in range(d):
                    for c in range(n_slab):
                        rows = pl.ds(r // base, tc // d, stride=step)
                        src = (scr.at[ci, t, c] if base == 1
                               else stage_scr[base].at[ci, t, r % base, c])
                        val = src[rows, :]
                        if d in stage_scr:
                            stage_scr[d][ci, t, r, c] = val
                        lo = r * WIDTH_B + c * LANES
                        o_ref[out_rows, lo:lo + LANES] = val.astype(BF16)
        return stage

    def windowed_tensors(ci):
        def stage(st):
            col = 0
            for o_ref in outs[:3]:
                width = o_ref.shape[-1]
                o_ref[ci * tc:(ci + 1) * tc, :] = (st["rstd"] * jnp.dot(
                    st["u"], w_ref[:, col:col + width], preferred_element_type=F32)).astype(BF16)
                col += width
        return stage

    states = [{} for _ in range(n_chains)]
    for make in ([norm] + [functools.partial(dilated_tensor, t=t) for t in range(3)]
                 + [windowed_tensors]):
        for ci, st in enumerate(states):
            make(ci)(st)


def _inproj(x2d, g, w, tm, dils):
    n, d_model = x2d.shape
    shapes = [(n, WIDTH_A), (n, KV_WIDTH_A), (n, KV_WIDTH_A)]
    blocks = [(tm, WIDTH_A), (tm, KV_WIDTH_A), (tm, KV_WIDTH_A)]
    n_chains = 1
    tc = tm // n_chains
    for d in dils:
        assert tc % (16 * d) == 0 and n % d == 0
        shapes += [(n // d, d * WIDTH_B)] * 3
        blocks += [(tm // d, d * WIDTH_B)] * 3
    stage_dils = tuple(b for b in dils if b > 1 and any(d > b and d % b == 0 for d in dils))
    n_slab = WIDTH_B // LANES
    return pl.pallas_call(
        functools.partial(_inproj_kernel, dils=dils, stage_dils=stage_dils, n_chains=n_chains),
        grid=(n // tm,),
        in_specs=[
            pl.BlockSpec((tm, d_model), lambda i: (i, 0)),
            pl.BlockSpec((1, d_model), lambda i: (0, 0)),
            pl.BlockSpec(w.shape, lambda i: (0, 0)),
        ],
        out_specs=[pl.BlockSpec(bs, lambda i: (i, 0)) for bs in blocks],
        out_shape=[jax.ShapeDtypeStruct(sh, BF16) for sh in shapes],
        scratch_shapes=[pltpu.VMEM((n_chains, 3, n_slab, tc, LANES), F32)]
                       + [pltpu.VMEM((n_chains, 3, b, n_slab, tc // b, LANES), F32)
                          for b in stage_dils],
        compiler_params=pltpu.CompilerParams(
            dimension_semantics=("parallel",), vmem_limit_bytes=_VMEM_LIMIT),
        name="inproj",
    )(x2d, g, w)


def _nt_dot(a, b):
    return lax.dot_general(a, b, (((1,), (1,)), ((), ())), preferred_element_type=F32)


def _head_softmax_pv(qm, k_win, v_aug, bias, sink):
    s = _nt_dot(qm, k_win) + bias
    m = jnp.max(s, axis=-1, keepdims=True)
    if sink is not None:
        m = jnp.maximum(m, sink)
    e = jnp.exp2(s - m)
    o = jnp.dot(e.astype(BF16), v_aug, preferred_element_type=F32)
    num = o[:, :LANES]
    den = o[:, LANES:]
    if sink is not None:
        den = den + jnp.exp2(sink - m)
    return num, den, m


def _win_kernel(sink_ref, q_ref, k_ref, v_ref, bias_ref, o_ref, *, seq, blk, unroll):
    nblk = seq // blk
    lane = lax.broadcasted_iota(jnp.int32, (blk, LANES), 1)
    low = lane < HEAD_DIM

    klen = 3 * blk

    def block(n, c):
        q0 = pl.multiple_of(n * blk, blk)
        k0 = pl.multiple_of(jnp.clip(q0 - blk, 0, seq - klen), blk)
        variant = jnp.where(n == 0, 0, jnp.where(n == nblk - 1, 2, 1))
        k_win = k_ref[pl.ds(k0, klen), :]
        v_win = v_ref[pl.ds(k0, klen), :]
        v_aug = jnp.concatenate([v_win, jnp.ones_like(v_win)], axis=1)
        for j in range(GROUP_A):
            q_tile = q_ref[pl.ds(q0, blk), j * LANES:(j + 1) * LANES]
            nums, dens = [], []
            for hf in range(N_KV_A):
                h = hf * GROUP_A + j
                qm = jnp.where(low if hf == 0 else ~low, q_tile, jnp.zeros_like(q_tile))
                num, den, _ = _head_softmax_pv(
                    qm, k_win, v_aug, bias_ref[variant, h], sink_ref[h])
                nums.append(num)
                dens.append(den)
            o_ref[pl.ds(q0, blk), j * LANES:(j + 1) * LANES] = (
                jnp.where(low, nums[0], nums[1]) / jnp.where(low, dens[0], dens[1])
            ).astype(o_ref.dtype)
        return c

    lax.fori_loop(0, nblk, block, 0, unroll=unroll)


def _windowed(qa, ka, va, bias, sink):
    b, s, _ = qa.shape
    blk = WINDOW_A
    nblk = s // blk
    assert s % blk == 0 and nblk >= 3
    kern = functools.partial(_win_kernel, seq=s, blk=blk,
                             unroll=math.gcd(nblk, _BLOCKS_PER_TRIP))
    return pl.pallas_call(
        kern,
        grid=(b,),
        in_specs=[
            pl.BlockSpec(memory_space=pltpu.SMEM),
            pl.BlockSpec((None, s, WIDTH_A), lambda i: (i, 0, 0)),
            pl.BlockSpec((None, s, KV_WIDTH_A), lambda i: (i, 0, 0)),
            pl.BlockSpec((None, s, KV_WIDTH_A), lambda i: (i, 0, 0)),
            pl.BlockSpec(bias.shape, lambda i: (0, 0, 0, 0)),
        ],
        out_specs=pl.BlockSpec((None, s, WIDTH_A), lambda i: (i, 0, 0)),
        out_shape=jax.ShapeDtypeStruct((b, s, WIDTH_A), BF16),
        compiler_params=pltpu.CompilerParams(
            dimension_semantics=("parallel",), vmem_limit_bytes=_VMEM_LIMIT),
        name="win_gqa",
    )(sink, qa, ka, va, bias)


def _dil_kernel(q_ref, k_ref, v_ref, bias_ref, o_ref, st_ref, *, ls, half, n_res, unroll):
    qb = 2 * half
    kw = 4 * half
    nblk = ls // qb
    lane = lax.broadcasted_iota(jnp.int32, (qb, LANES), 1)
    low = lane < HEAD_DIM

    def block(n, res_i):
        q0 = pl.multiple_of(n * qb, qb)
        k0 = pl.multiple_of(jnp.clip(q0 - half, 0, ls - kw), half)
        variant = jnp.where(n == 0, 0, jnp.where(n == nblk - 1, 2, 1))
        st_acc = jnp.zeros((qb, LANES), F32)
        for j in range(N_HEADS_B // 2):
            cols = slice(res_i * WIDTH_B + j * LANES, res_i * WIDTH_B + (j + 1) * LANES)
            k_win = k_ref[pl.ds(k0, kw), cols]
            v_win = v_ref[pl.ds(k0, kw), cols]
            v_aug = jnp.concatenate([v_win, jnp.ones_like(v_win)], axis=1)
            q_tile = q_ref[pl.ds(q0, qb), cols]
            res = []
            for hf in range(2):
                h = 2 * j + hf
                qm = jnp.where(low if hf == 0 else ~low, q_tile, jnp.zeros_like(q_tile))
                num, den, m = _head_softmax_pv(qm, k_win, v_aug, bias_ref[variant, h], None)
                res.append(num)
                st_acc = jnp.where(lane == h, m, jnp.where(lane == N_HEADS_B + h, den, st_acc))
            o_ref[pl.ds(q0, qb), cols] = jnp.where(low, res[0], res[1]).astype(o_ref.dtype)
        st_ref[pl.ds(q0, qb), res_i * LANES:(res_i + 1) * LANES] = st_acc

    def trip(n, c):
        for res_i in range(n_res):
            block(n, res_i)
        return c

    lax.fori_loop(0, nblk, trip, 0, unroll=unroll)


def _dilated(qd, kd, vd, bias, pattern, batch, dil, half):
    rows, w_all = qd.shape
    assert w_all == dil * WIDTH_B
    ls = rows // batch
    nblk = ls // (2 * half)
    assert ls % (2 * half) == 0 and nblk >= 2
    n_res = math.gcd(dil, max(1, _BLOCKS_PER_TRIP // nblk))
    unroll = math.gcd(nblk, max(1, _BLOCKS_PER_TRIP // n_res))
    view = lambda t: t.reshape(batch, ls, w_all)
    kern = functools.partial(_dil_kernel, ls=ls, half=half, n_res=n_res, unroll=unroll)
    spec = pl.BlockSpec((None, ls, n_res * WIDTH_B), lambda i, r: (i, 0, r))
    n_var = 3
    o, st = pl.pallas_call(
        kern,
        grid=(batch, dil // n_res),
        in_specs=[spec, spec, spec,
                  pl.BlockSpec((n_var,) + bias.shape[1:], lambda i, r: (pattern, 0, 0, 0))],
        out_specs=[spec, pl.BlockSpec((None, ls, n_res * LANES), lambda i, r: (i, 0, r))],
        out_shape=[jax.ShapeDtypeStruct((batch, ls, w_all), BF16),
                   jax.ShapeDtypeStruct((batch, ls, dil * LANES), F32)],
        compiler_params=pltpu.CompilerParams(
            dimension_semantics=("parallel", "parallel"), vmem_limit_bytes=_VMEM_LIMIT),
        name=f"dilated_d{dil}",
    )(view(qd), view(kd), view(vd), bias)
    return o.reshape(rows, w_all), st.reshape(rows, dil * LANES)


def _post_kernel(x_ref, p_ref, oa_ref, *refs, dils, ff_chunk, chain_rows):
    n_pat = len(dils)
    ob_refs = refs[:n_pat]
    st_refs = refs[n_pat:2 * n_pat]
    (expand_ref, goa_ref, gob_ref, wo_ref, gpm_ref, gpre_ref, wup_ref, wdn_ref,
     gmlp_ref, wpp_ref, wpg_ref, bpg_ref, gple_ref, out_ref) = refs[2 * n_pat:2 * n_pat + 14]
    scratch = refs[2 * n_pat + 14:]
    tm = x_ref.shape[0]
    assert sum(chain_rows) == tm
    n_slab = WIDTH_B // LANES

    def expand(w):
        head_lane = lax.broadcasted_iota(jnp.int32, w.shape, 1) < N_HEADS_B
        w = jnp.where(head_lane, w, 0.0)
        hi = w.astype(BF16)
        lo = (w - hi.astype(F32)).astype(BF16)
        return jnp.dot(jnp.concatenate([hi, lo], axis=1), expand_ref[...],
                       preferred_element_type=F32)

    def combine(st):
        rows = st["rows"]
        nums, stats = [], []
        si = 0
        for d, o_ref, s_ref in zip(dils, ob_refs, st_refs):
            if d == 1:
                nums.append(o_ref[rows, :].astype(F32))
                stats.append(s_ref[rows, :])
                continue
            o_scr, s_scr = scratch[si], scratch[si + 1]
            si += 2
            assert rows.start % d == 0 and rows.stop % d == 0
            src = slice(rows.start // d, rows.stop // d)
            for r in range(d):
                dst = pl.ds(rows.start + r, (rows.stop - rows.start) // d, stride=d)
                for c in range(n_slab):
                    lo = r * WIDTH_B + c * LANES
                    o_scr[c, dst, :] = o_ref[src, lo:lo + LANES].astype(F32)
                s_scr[dst, :] = s_ref[src, r * LANES:(r + 1) * LANES]
            nums.append(jnp.concatenate([o_scr[c, rows, :] for c in range(n_slab)], axis=1))
            stats.append(s_scr[rows, :])
        mx = functools.reduce(jnp.maximum, stats)
        scales = [jnp.exp2(s - mx) for s in stats]
        dens = [pltpu.roll(s, LANES - N_HEADS_B, 1) for s in stats]
        tot = functools.reduce(lambda a, b: a + b, [sc * dn for sc, dn in zip(scales, dens)])
        ob = None
        for sc, num in zip(scales, nums):
            term = expand(sc / tot) * num
            ob = term if ob is None else ob + term
        st["oa"] = _rms(oa_ref[rows, :].astype(F32), goa_ref[...]).astype(BF16)
        st["ob"] = _rms(ob, gob_ref[...]).astype(BF16)

    def mix(st):
        m = (jnp.dot(st.pop("oa"), wo_ref[:WIDTH_A, :], preferred_element_type=F32)
             + jnp.dot(st.pop("ob"), wo_ref[WIDTH_A:, :], preferred_element_type=F32))
        st["h"] = x_ref[st["rows"], :] + _rms(m, gpm_ref[...])
        st["v"] = _rms(st["h"], gpre_ref[...]).astype(BF16)
        st["ff"] = None

    def mlp_chunk(c):
        def stage(st):
            a = jnp.maximum(
                jnp.dot(st["v"], wup_ref[:, c:c + ff_chunk], preferred_element_type=F32), 0.0)
            t = jnp.dot((a * a).astype(BF16), wdn_ref[c:c + ff_chunk, :],
                        preferred_element_type=F32)
            st["ff"] = t if st["ff"] is None else st["ff"] + t
        return stage

    def ple(st):
        h = st["h"] + _rms(st.pop("ff"), gmlp_ref[...])
        gate = jax.nn.sigmoid(
            jnp.dot(h.astype(BF16), wpg_ref[...], preferred_element_type=F32) + bpg_ref[...])
        emb = jnp.dot(p_ref[st["rows"], :].astype(BF16), wpp_ref[...],
                      preferred_element_type=F32)
        out_ref[st["rows"], :] = h + _rms(gate * emb, gple_ref[...])

    stages = [combine, mix] + [mlp_chunk(c) for c in range(0, wup_ref.shape[1], ff_chunk)] + [ple]
    starts = np.cumsum((0,) + chain_rows)
    states = [{"rows": slice(int(lo), int(hi))} for lo, hi in zip(starts[:-1], starts[1:])]
    for stage in stages:
        for st in states:
            stage(st)


def _post(x2d, p2d, oa, obs, stats, dils, consts, tm, ff_chunk):
    n, d_model = x2d.shape
    row = lambda width: pl.BlockSpec((tm, width), lambda i: (i, 0))
    const = lambda a: pl.BlockSpec(a.shape, lambda i: (0,) * a.ndim,
                                   pipeline_mode=pl.Buffered(1))
    scratch = []
    for d in dils:
        if d > 1:
            scratch += [pltpu.VMEM((WIDTH_B // LANES, tm, LANES), F32),
                        pltpu.VMEM((tm, LANES), F32)]
    return pl.pallas_call(
        functools.partial(_post_kernel, dils=dils, ff_chunk=ff_chunk,
                          chain_rows=tuple(tm * f // sum(_ROW_CHAINS) for f in _ROW_CHAINS)),
        grid=(n // tm,),
        in_specs=[row(d_model), row(p2d.shape[1]), row(WIDTH_A)]
                 + [pl.BlockSpec((tm // d, d * WIDTH_B), lambda i: (i, 0)) for d in dils]
                 + [pl.BlockSpec((tm // d, d * LANES), lambda i: (i, 0)) for d in dils]
                 + [const(a) for a in consts],
        out_specs=row(d_model),
        out_shape=jax.ShapeDtypeStruct((n, d_model), x2d.dtype),
        scratch_shapes=scratch,
        compiler_params=pltpu.CompilerParams(
            dimension_semantics=("parallel",), vmem_limit_bytes=_VMEM_LIMIT),
        name="post",
    )(x2d, p2d, oa, *obs, *stats, *consts)


def kernel(x, p, rel_bias_table, g_pre_mix, w_in, sink_a, g_out_a, g_out_b, w_o, g_post_mix,
           g_pre_mlp, w_up, w_down, g_post_mlp, w_ple_proj, w_ple_gate, b_ple_gate, g_post_ple):
    b, s, d_model = x.shape
    depth = w_in.shape[0]
    n = b * s
    tm = 512
    dils = tuple(dil for _, dil in DILATED_PATTERNS)

    def pair_heads(a, axis):
        shape = a.shape
        a = a.reshape(shape[:axis] + (N_KV_A, GROUP_A, HEAD_DIM) + shape[axis + 1:])
        return jnp.swapaxes(a, axis, axis + 1).reshape(shape)

    idx_a, shifts_a = _band_layout(WINDOW_A, 3 * WINDOW_A, (0, WINDOW_A, 2 * WINDOW_A),
                                   1, WINDOW_A)
    half = DILATED_PATTERNS[0][0] // (2 * DILATED_PATTERNS[0][1])
    idx_b = []
    for window, dil in DILATED_PATTERNS:
        assert window // (2 * dil) == half
        idx, shifts_b = _band_layout(2 * half, 4 * half, (0, half, 2 * half), dil, half)
        idx_b.append(idx)
    bias_a, bias_b = _bias(
        rel_bias_table.T.astype(F32) * LOG2E,
        jnp.asarray(idx_a)[None, None], shifts_a,
        (len(shifts_a), N_HEADS_A, WINDOW_A, 3 * WINDOW_A),
        jnp.asarray(np.stack(idx_b))[:, None], shifts_b,
        (len(DILATED_PATTERNS) * len(shifts_b), N_HEADS_B, 2 * half, 4 * half))

    expand_np = np.zeros((2 * LANES, WIDTH_B), np.float32)
    for hh in range(N_HEADS_B):
        expand_np[hh, hh * HEAD_DIM:(hh + 1) * HEAD_DIM] = 1.0
        expand_np[LANES + hh, hh * HEAD_DIM:(hh + 1) * HEAD_DIM] = 1.0
    expand = jnp.asarray(expand_np, BF16)

    h2d = x.reshape(n, d_model)
    row = lambda a: a.reshape(1, -1).astype(F32)
    for i in range(depth):
        scale = HEAD_DIM ** -0.5 * LOG2E
        w = w_in[i]
        o0 = WIDTH_A + 2 * KV_WIDTH_A
        w_all = jnp.concatenate([
            pair_heads(w[:, :WIDTH_A], 1) * scale,
            w[:, WIDTH_A:o0],
            w[:, o0:o0 + WIDTH_B] * scale,
            w[:, o0 + WIDTH_B:],
        ], axis=1).astype(BF16)
        outs = _inproj(h2d, row(g_pre_mix[i]), w_all, tm, dils)
        qa, ka, va = (t.reshape(b, s, t.shape[-1]) for t in outs[:3])

        o_a = _windowed(qa, ka, va, bias_a, sink_a[i].astype(F32) * LOG2E)
        obs, stats = [], []
        for pi, (window, dil) in enumerate(DILATED_PATTERNS):
            qd, kd, vd = outs[3 + 3 * pi:6 + 3 * pi]
            o, st = _dilated(qd, kd, vd, bias_b, pi, b, dil, window // (2 * dil))
            obs.append(o)
            stats.append(st)

        wo = jnp.concatenate([pair_heads(w_o[i][:WIDTH_A], 0), w_o[i][WIDTH_A:]],
                             axis=0).astype(BF16)
        consts = (expand, row(pair_heads(g_out_a[i], 0)), row(g_out_b[i]), wo, row(g_post_mix[i]),
                  row(g_pre_mlp[i]), w_up[i].astype(BF16), w_down[i].astype(BF16),
                  row(g_post_mlp[i]), w_ple_proj[i].astype(BF16), w_ple_gate[i].astype(BF16),
                  row(b_ple_gate[i]), row(g_post_ple[i]))
        h2d = _post(h2d, p[i].reshape(n, -1), o_a.reshape(n, WIDTH_A), obs, stats, dils,
                    consts, tm, 2048)
    return h2d.reshape(b, s, d_model)
```

```python
import functools
import math

import jax
import jax.numpy as jnp
import numpy as np
from jax import lax
from jax.experimental import pallas as pl
from jax.experimental.pallas import tpu as pltpu

HEAD_DIM = 64
N_HEADS_A = 8
N_KV_A = 2
GROUP_A = N_HEADS_A // N_KV_A
WINDOW_A = 128
N_HEADS_B = 8
DILATED_PATTERNS = ((128, 1), (512, 4), (2048, 16))
WIDTH_A = N_HEADS_A * HEAD_DIM
WIDTH_B = N_HEADS_B * HEAD_DIM
KV_WIDTH_A = N_KV_A * HEAD_DIM
NUM_BUCKETS = 32
MAX_DISTANCE = 1024
EPS = 1e-6
NEG = -1e30
LOG2E = 1.4426950408889634

LANES = 128
SUBLANES = 8
F32 = jnp.float32
BF16 = jnp.bfloat16

_VMEM_LIMIT = 56 * 1024 * 1024
_BLOCKS_PER_TRIP = 16
_WIN_LOOKAHEAD = 2
_DIL_LOOKAHEAD = 4
_ROW_CHAINS = (1, 1)


def _rms(x, g):
    ms = jnp.mean(x * x, axis=-1, keepdims=True)
    return (x * lax.rsqrt(ms + EPS)) * g


def _t5_bucket_np(rel):
    half = NUM_BUCKETS // 2
    max_exact = half // 2
    sign = np.where(rel > 0, half, 0)
    n = np.abs(rel)
    nf = np.maximum(n, 1).astype(np.float32)
    large = max_exact + (np.log(nf / np.float32(max_exact))
                         / np.float32(math.log(MAX_DISTANCE / max_exact))
                         * np.float32(half - max_exact)).astype(np.int32)
    large = np.minimum(large, half - 1)
    return (sign + np.where(n < max_exact, n, large)).astype(np.int32)


def _band_layout(q_len, k_len, offs, dil, half_window):
    center = q_len - 1 + max(offs)
    length = -(-(center + k_len) // LANES) * LANES
    rel = np.arange(length) - center
    idx = np.where(np.abs(rel) <= half_window, _t5_bucket_np(rel * dil), -1).astype(np.int32)
    return idx, tuple((off - center) % length for off in offs)


def _bias_kernel(tab_ref, idxa_ref, idxb_ref, outa_ref, outb_ref, *, shifts_a, shifts_b):
    def build(idx_ref, out_ref, head0, shifts):
        n_pat, _, length = idx_ref.shape
        _, n_heads, q_len, k_len = out_ref.shape
        for p in range(n_pat):
            idx = idx_ref[p]
            vec = jnp.full((n_heads, length), NEG, F32)
            for b in range(NUM_BUCKETS):
                vec = jnp.where(idx == b, tab_ref[head0:head0 + n_heads, b:b + 1], vec)
            for h in range(n_heads):
                rows = jnp.broadcast_to(vec[h:h + 1, :], (q_len, length))
                for v, s in enumerate(shifts):
                    band = pltpu.roll(rows, s, 1, stride=1, stride_axis=0)
                    out_ref[p * len(shifts) + v, h] = band[:, :k_len]

    build(idxa_ref, outa_ref, 0, shifts_a)
    build(idxb_ref, outb_ref, N_HEADS_A, shifts_b)


def _folded_bias_kernel(tab_ref, idx_ref, out_ref, *, head0):
    n_tiles, rows, _ = idx_ref.shape
    n_heads = out_ref.shape[1]

    def chunk(i, c):
        t = i // (rows // SUBLANES)
        r0 = pl.multiple_of((i % (rows // SUBLANES)) * SUBLANES, SUBLANES)
        idx = idx_ref[t, pl.ds(r0, SUBLANES), :]
        accs = [jnp.full(idx.shape, NEG, F32)] * n_heads
        for b in range(NUM_BUCKETS):
            mask = idx == b
            accs = [jnp.where(mask, tab_ref[b, head0 + h], a) for h, a in enumerate(accs)]
        for h in range(n_heads):
            out_ref[t, h, pl.ds(r0, SUBLANES), :] = accs[h]
        return c

    lax.fori_loop(0, n_tiles * (rows // SUBLANES), chunk, 0)


def _folded_bias(table, idx, head0, n_heads):
    vmem = pl.BlockSpec(memory_space=pltpu.VMEM)
    return pl.pallas_call(
        functools.partial(_folded_bias_kernel, head0=head0),
        in_specs=[pl.BlockSpec(memory_space=pltpu.SMEM), vmem],
        out_specs=vmem,
        out_shape=jax.ShapeDtypeStruct((idx.shape[0], n_heads) + idx.shape[1:], F32),
        compiler_params=pltpu.CompilerParams(vmem_limit_bytes=_VMEM_LIMIT),
        name="folded_bias",
    )(table, idx)


def _bias(table_t, idx_a, shifts_a, shape_a, idx_b, shifts_b, shape_b):
    vmem = pl.BlockSpec(memory_space=pltpu.VMEM)
    return pl.pallas_call(
        functools.partial(_bias_kernel, shifts_a=shifts_a, shifts_b=shifts_b),
        in_specs=[vmem, vmem, vmem],
        out_specs=[vmem, vmem],
        out_shape=[jax.ShapeDtypeStruct(shape_a, F32), jax.ShapeDtypeStruct(shape_b, F32)],
        compiler_params=pltpu.CompilerParams(vmem_limit_bytes=_VMEM_LIMIT),
        name="bias",
    )(table_t, idx_a, idx_b)


def _inproj_kernel(x_ref, g_ref, w_ref, *refs, dils, stage_dils, n_chains):
    n_pat = len(dils)
    outs = refs[:3 + 3 * n_pat]
    scr = refs[3 + 3 * n_pat]
    stage_scr = dict(zip(stage_dils, refs[4 + 3 * n_pat:]))
    tc = x_ref.shape[0] // n_chains
    n_slab = WIDTH_B // LANES
    col_b = sum(o_ref.shape[-1] for o_ref in outs[:3])

    def norm(ci):
        def stage(st):
            x = x_ref[ci * tc:(ci + 1) * tc, :]
            st["u"] = (x * g_ref[...]).astype(BF16)
            st["rstd"] = lax.rsqrt(jnp.mean(x * x, axis=-1, keepdims=True) + EPS)
        return stage

    def dilated_tensor(ci, t):
        def stage(st):
            col = col_b + t * WIDTH_B
            seg = st["rstd"] * jnp.dot(st["u"], w_ref[:, col:col + WIDTH_B],
                                       preferred_element_type=F32)
            if any(d > 1 for d in dils):
                for c in range(n_slab):
                    scr[ci, t, c] = seg[:, c * LANES:(c + 1) * LANES]
            for pi, d in enumerate(dils):
                o_ref = outs[3 + 3 * pi + t]
                out_rows = slice(ci * tc // d, (ci + 1) * tc // d)
                if d == 1:
                    o_ref[out_rows, :] = seg.astype(BF16)
                    continue
                base = max([b for b in stage_dils if b < d and d % b == 0], default=1)
                step = d // base
                for r in range(d):
                    for c in range(n_slab):
                        rows = pl.ds(r // base, tc // d, stride=step)
                        src = (scr.at[ci, t, c] if base == 1
                               else stage_scr[base].at[ci, t, r % base, c])
                        val = src[rows, :]
                        if d in stage_scr:
                            stage_scr[d][ci, t, r, c] = val
                        lo = r * WIDTH_B + c * LANES
                        o_ref[out_rows, lo:lo + LANES] = val.astype(BF16)
        return stage

    def windowed_tensors(ci):
        def stage(st):
            col = 0
            for o_ref in outs[:3]:
                width = o_ref.shape[-1]
                o_ref[ci * tc:(ci + 1) * tc, :] = (st["rstd"] * jnp.dot(
                    st["u"], w_ref[:, col:col + width], preferred_element_type=F32)).astype(BF16)
                col += width
        return stage

    states = [{} for _ in range(n_chains)]
    for make in ([norm] + [functools.partial(dilated_tensor, t=t) for t in range(3)]
                 + [windowed_tensors]):
        for ci, st in enumerate(states):
            make(ci)(st)


def _inproj(x2d, g, w, tm, dils):
    n, d_model = x2d.shape
    shapes = [(n, WIDTH_A), (n, KV_WIDTH_A), (n, KV_WIDTH_A)]
    blocks = [(tm, WIDTH_A), (tm, KV_WIDTH_A), (tm, KV_WIDTH_A)]
    n_chains = 1
    tc = tm // n_chains
    for d in dils:
        assert tc % (16 * d) == 0 and n % d == 0
        shapes += [(n // d, d * WIDTH_B)] * 3
        blocks += [(tm // d, d * WIDTH_B)] * 3
    stage_dils = tuple(b for b in dils if b > 1 and any(d > b and d % b == 0 for d in dils))
    n_slab = WIDTH_B // LANES
    return pl.pallas_call(
        functools.partial(_inproj_kernel, dils=dils, stage_dils=stage_dils, n_chains=n_chains),
        grid=(n // tm,),
        in_specs=[
            pl.BlockSpec((tm, d_model), lambda i: (i, 0)),
            pl.BlockSpec((1, d_model), lambda i: (0, 0)),
            pl.BlockSpec(w.shape, lambda i: (0, 0)),
        ],
        out_specs=[pl.BlockSpec(bs, lambda i: (i, 0)) for bs in blocks],
        out_shape=[jax.ShapeDtypeStruct(sh, BF16) for sh in shapes],
        scratch_shapes=[pltpu.VMEM((n_chains, 3, n_slab, tc, LANES), F32)]
                       + [pltpu.VMEM((n_chains, 3, b, n_slab, tc // b, LANES), F32)
                          for b in stage_dils],
        compiler_params=pltpu.CompilerParams(
            dimension_semantics=("parallel",), vmem_limit_bytes=_VMEM_LIMIT),
        name="inproj",
    )(x2d, g, w)


def _nt_dot(a, b):
    return lax.dot_general(a, b, (((1,), (1,)), ((), ())), preferred_element_type=F32)


def _win_kernel(sink_ref, q_ref, k_ref, v_ref, bias_ref, o_ref, *, seq, blk, unroll):
    nblk = seq // blk
    lane = lax.broadcasted_iota(jnp.int32, (blk, LANES), 1)
    low = lane < HEAD_DIM

    klen = 3 * blk

    def block_context(n):
        q0 = pl.multiple_of(n * blk, blk)
        k0 = pl.multiple_of(jnp.clip(q0 - blk, 0, seq - klen), blk)
        variant = jnp.where(n == 0, 0, jnp.where(n == nblk - 1, 2, 1))
        k1 = k_ref[pl.ds(k0, 2 * blk), :]
        k2 = k_ref[pl.ds(k0 + 2 * blk, blk), :]
        v1 = v_ref[pl.ds(k0, 2 * blk), :]
        v2 = v_ref[pl.ds(k0 + 2 * blk, blk), :]
        zero = jnp.zeros_like(k2)
        one_low = low.astype(F32).astype(BF16)
        one_high = (~low).astype(F32).astype(BF16)
        return dict(
            q0=q0, variant=variant, k1=k1,
            v1_aug=jnp.concatenate([v1, jnp.ones_like(v1)], axis=1),
            k2_pack=jnp.concatenate([jnp.where(low, k2, zero), jnp.where(low, zero, k2)], axis=0),
            v2_pack=jnp.concatenate(
                [jnp.concatenate([jnp.where(low, v2, zero), one_low], axis=1),
                 jnp.concatenate([jnp.where(low, zero, v2), one_high], axis=1)], axis=0))

    def scores(ctx, j):
        q_tile = q_ref[pl.ds(ctx["q0"], blk), j * LANES:(j + 1) * LANES]
        s2 = _nt_dot(q_tile, ctx["k2_pack"])
        out = []
        for hf in range(N_KV_A):
            qm = jnp.where(low if hf == 0 else ~low, q_tile, jnp.zeros_like(q_tile))
            out.append(jnp.concatenate(
                [_nt_dot(qm, ctx["k1"]), s2[:, hf * LANES:(hf + 1) * LANES]], axis=1)
                + bias_ref[ctx["variant"], hf * GROUP_A + j])
        return out

    def softmax_pv(ctx, j, logits):
        nums, dens, e2s = [], [], []
        for hf, s in enumerate(logits):
            sink = sink_ref[hf * GROUP_A + j]
            m = jnp.maximum(jnp.max(s, axis=-1, keepdims=True), sink)
            e = jnp.exp2(s - m).astype(BF16)
            o1 = jnp.dot(e[:, :2 * blk], ctx["v1_aug"], preferred_element_type=F32)
            nums.append(o1[:, :LANES])
            dens.append(o1[:, LANES:] + jnp.exp2(sink - m))
            e2s.append(e[:, 2 * blk:])
        o2 = jnp.dot(jnp.concatenate(e2s, axis=1), ctx["v2_pack"], preferred_element_type=F32)
        num = jnp.where(low, nums[0], nums[1]) + o2[:, :LANES]
        den = jnp.where(low, dens[0], dens[1]) + o2[:, LANES:]
        o_ref[pl.ds(ctx["q0"], blk), j * LANES:(j + 1) * LANES] = (num / den).astype(o_ref.dtype)

    def trip(t, c):
        pending = []
        for i in range(unroll):
            ctx = block_context(t * unroll + i)
            for j in range(GROUP_A):
                pending.append((ctx, j, scores(ctx, j)))
                if len(pending) > _WIN_LOOKAHEAD:
                    softmax_pv(*pending.pop(0))
        for task in pending:
            softmax_pv(*task)
        return c

    lax.fori_loop(0, nblk // unroll, trip, 0)


def _windowed(qa, ka, va, bias, sink):
    b, s, _ = qa.shape
    blk = WINDOW_A
    nblk = s // blk
    assert s % blk == 0 and nblk >= 3
    kern = functools.partial(_win_kernel, seq=s, blk=blk,
                             unroll=math.gcd(nblk, _BLOCKS_PER_TRIP))
    return pl.pallas_call(
        kern,
        grid=(b,),
        in_specs=[
            pl.BlockSpec(memory_space=pltpu.SMEM),
            pl.BlockSpec((None, s, WIDTH_A), lambda i: (i, 0, 0)),
            pl.BlockSpec((None, s, KV_WIDTH_A), lambda i: (i, 0, 0)),
            pl.BlockSpec((None, s, KV_WIDTH_A), lambda i: (i, 0, 0)),
            pl.BlockSpec(bias.shape, lambda i: (0, 0, 0, 0)),
        ],
        out_specs=pl.BlockSpec((None, s, WIDTH_A), lambda i: (i, 0, 0)),
        out_shape=jax.ShapeDtypeStruct((b, s, WIDTH_A), BF16),
        compiler_params=pltpu.CompilerParams(
            dimension_semantics=("parallel",), vmem_limit_bytes=_VMEM_LIMIT),
        name="win_gqa",
    )(sink, qa, ka, va, bias)


def _dil_kernel(q_ref, k_ref, v_ref, bias_ref, o_ref, st_ref, *, ls, half, n_res, fold, unroll):
    qb = 2 * half
    kw = 4 * half
    nblk = ls // qb
    lane = lax.broadcasted_iota(jnp.int32, (qb, LANES), 1)
    low = lane < HEAD_DIM

    n_pairs = N_HEADS_B // 2
    assert fold == 1 or n_res == 1

    def rows_of(start, align, size):
        row0 = start if fold == 1 else start // fold
        return pl.ds(pl.multiple_of(row0, align // fold), size // fold)

    def window(ref, start, align, size, res_i, j):
        parts = [ref[rows_of(start, align, size),
                     (res_i * fold + g) * WIDTH_B + j * LANES:
                     (res_i * fold + g) * WIDTH_B + (j + 1) * LANES]
                 for g in range(fold)]
        return parts[0] if fold == 1 else jnp.concatenate(parts, axis=0)

    def store(ref, start, res_i, col0, val):
        rows = qb // fold
        group_width = ref.shape[-1] // (n_res * fold)
        for g in range(fold):
            col = (res_i * fold + g) * group_width + col0
            ref[rows_of(start, qb, qb), col:col + LANES] = val[g * rows:(g + 1) * rows]

    def scores(n, res_i, j):
        q0 = pl.multiple_of(n * qb, qb)
        k0 = pl.multiple_of(jnp.clip(q0 - half, 0, ls - kw), half)
        variant = jnp.where(n == 0, 0, jnp.where(n == nblk - 1, 2, 1))
        k_win = window(k_ref, k0, half, kw, res_i, j)
        q_tile = window(q_ref, q0, qb, qb, res_i, j)
        logits = []
        for hf in range(2):
            qm = jnp.where(low if hf == 0 else ~low, q_tile, jnp.zeros_like(q_tile))
            logits.append(_nt_dot(qm, k_win) + bias_ref[variant, 2 * j + hf])
        return dict(q0=q0, k0=k0, res_i=res_i, j=j, logits=logits)

    def softmax_pv(task, stats):
        j, res_i = task["j"], task["res_i"]
        v_win = window(v_ref, task["k0"], half, kw, res_i, j)
        v_aug = jnp.concatenate([v_win, jnp.ones_like(v_win)], axis=1)
        st_acc = jnp.zeros((qb, LANES), F32) if j == 0 else stats.pop()
        nums = []
        for hf, s in enumerate(task["logits"]):
            h = 2 * j + hf
            m = jnp.max(s, axis=-1, keepdims=True)
            o = jnp.dot(jnp.exp2(s - m).astype(BF16), v_aug, preferred_element_type=F32)
            nums.append(o[:, :LANES])
            st_acc = jnp.where(lane == h, m,
                               jnp.where(lane == N_HEADS_B + h, o[:, LANES:], st_acc))
        store(o_ref, task["q0"], res_i, j * LANES,
              jnp.where(low, nums[0], nums[1]).astype(o_ref.dtype))
        if j == n_pairs - 1:
            store(st_ref, task["q0"], res_i, 0, st_acc)
        else:
            stats.append(st_acc)

    def trip(t, c):
        pending, stats = [], []
        for i in range(unroll):
            for res_i in range(n_res):
                for j in range(n_pairs):
                    pending.append(scores(t * unroll + i, res_i, j))
                    if len(pending) > _DIL_LOOKAHEAD:
                        softmax_pv(pending.pop(0), stats)
        for task in pending:
            softmax_pv(task, stats)
        return c

    lax.fori_loop(0, nblk // unroll, trip, 0)


def _dilated(qd, kd, vd, bias, bias_index, batch, dil, layout, half):
    rows, w_all = qd.shape
    assert w_all == layout * WIDTH_B and layout % dil == 0
    fold = layout // dil
    ls = rows * fold // batch
    nblk = ls // (2 * half)
    assert ls % (2 * half) == 0 and nblk >= 2 and (fold == 1 or dil == 1)
    n_res = math.gcd(dil, max(1, _BLOCKS_PER_TRIP // nblk))
    unroll = math.gcd(nblk, max(1, _BLOCKS_PER_TRIP // n_res))
    view = lambda t: t.reshape(batch, ls // fold, w_all)
    kern = functools.partial(_dil_kernel, ls=ls, half=half, n_res=n_res, fold=fold,
                             unroll=unroll)
    spec = pl.BlockSpec((None, ls // fold, n_res * fold * WIDTH_B), lambda i, r: (i, 0, r))
    n_var = 3
    o, st = pl.pallas_call(
        kern,
        grid=(batch, dil // n_res),
        in_specs=[spec, spec, spec,
                  pl.BlockSpec((n_var,) + bias.shape[1:], lambda i, r: (bias_index, 0, 0, 0))],
        out_specs=[spec, pl.BlockSpec((None, ls // fold, n_res * fold * LANES),
                                      lambda i, r: (i, 0, r))],
        out_shape=[jax.ShapeDtypeStruct((batch, ls // fold, w_all), BF16),
                   jax.ShapeDtypeStruct((batch, ls // fold, layout * LANES), F32)],
        compiler_params=pltpu.CompilerParams(
            dimension_semantics=("parallel", "parallel"), vmem_limit_bytes=_VMEM_LIMIT),
        name=f"dilated_d{dil}",
    )(view(qd), view(kd), view(vd), bias)
    return o.reshape(rows, w_all), st.reshape(rows, layout * LANES)


def _post_kernel(x_ref, p_ref, oa_ref, *refs, dils, ff_chunk, chain_rows):
    n_pat = len(dils)
    ob_refs = refs[:n_pat]
    st_refs = refs[n_pat:2 * n_pat]
    (expand_ref, goa_ref, gob_ref, wo_ref, gpm_ref, gpre_ref, wup_ref, wdn_ref,
     gmlp_ref, wpp_ref, wpg_ref, bpg_ref, gple_ref, out_ref) = refs[2 * n_pat:2 * n_pat + 14]
    scratch = refs[2 * n_pat + 14:]
    tm = x_ref.shape[0]
    assert sum(chain_rows) == tm
    n_slab = WIDTH_B // LANES

    def expand(w):
        head_lane = lax.broadcasted_iota(jnp.int32, w.shape, 1) < N_HEADS_B
        w = jnp.where(head_lane, w, 0.0)
        hi = w.astype(BF16)
        lo = (w - hi.astype(F32)).astype(BF16)
        return jnp.dot(jnp.concatenate([hi, lo], axis=1), expand_ref[...],
                       preferred_element_type=F32)

    def combine(st):
        rows = st["rows"]
        nums, stats = [], []
        si = 0
        for d, o_ref, s_ref in zip(dils, ob_refs, st_refs):
            if d == 1:
                nums.append(o_ref[rows, :].astype(F32))
                stats.append(s_ref[rows, :])
                continue
            o_scr, s_scr = scratch[si], scratch[si + 1]
            si += 2
            assert rows.start % d == 0 and rows.stop % d == 0
            src = slice(rows.start // d, rows.stop // d)
            for r in range(d):
                dst = pl.ds(rows.start + r, (rows.stop - rows.start) // d, stride=d)
                for c in range(n_slab):
                    lo = r * WIDTH_B + c * LANES
                    o_scr[c, dst, :] = o_ref[src, lo:lo + LANES].astype(F32)
                s_scr[dst, :] = s_ref[src, r * LANES:(r + 1) * LANES]
            nums.append(jnp.concatenate([o_scr[c, rows, :] for c in range(n_slab)], axis=1))
            stats.append(s_scr[rows, :])
        mx = functools.reduce(jnp.maximum, stats)
        scales = [jnp.exp2(s - mx) for s in stats]
        dens = [pltpu.roll(s, LANES - N_HEADS_B, 1) for s in stats]
        tot = functools.reduce(lambda a, b: a + b, [sc * dn for sc, dn in zip(scales, dens)])
        ob = None
        for sc, num in zip(scales, nums):
            term = expand(sc / tot) * num
            ob = term if ob is None else ob + term
        st["oa"] = _rms(oa_ref[rows, :].astype(F32), goa_ref[...]).astype(BF16)
        st["ob"] = _rms(ob, gob_ref[...]).astype(BF16)

    def mix(st):
        m = (jnp.dot(st.pop("oa"), wo_ref[:WIDTH_A, :], preferred_element_type=F32)
             + jnp.dot(st.pop("ob"), wo_ref[WIDTH_A:, :], preferred_element_type=F32))
        st["h"] = x_ref[st["rows"], :] + _rms(m, gpm_ref[...])
        st["v"] = _rms(st["h"], gpre_ref[...]).astype(BF16)
        st["ff"] = None

    def mlp_chunk(c):
        def stage(st):
            a = jnp.maximum(
                jnp.dot(st["v"], wup_ref[:, c:c + ff_chunk], preferred_element_type=F32), 0.0)
            t = jnp.dot((a * a).astype(BF16), wdn_ref[c:c + ff_chunk, :],
                        preferred_element_type=F32)
            st["ff"] = t if st["ff"] is None else st["ff"] + t
        return stage

    def ple(st):
        h = st["h"] + _rms(st.pop("ff"), gmlp_ref[...])
        gate = jax.nn.sigmoid(
            jnp.dot(h.astype(BF16), wpg_ref[...], preferred_element_type=F32) + bpg_ref[...])
        emb = jnp.dot(p_ref[st["rows"], :].astype(BF16), wpp_ref[...],
                      preferred_element_type=F32)
        out_ref[st["rows"], :] = h + _rms(gate * emb, gple_ref[...])

    stages = [combine, mix] + [mlp_chunk(c) for c in range(0, wup_ref.shape[1], ff_chunk)] + [ple]
    starts = np.cumsum((0,) + chain_rows)
    states = [{"rows": slice(int(lo), int(hi))} for lo, hi in zip(starts[:-1], starts[1:])]
    for stage in stages:
        for st in states:
            stage(st)


def _post(x2d, p2d, oa, obs, stats, dils, consts, tm, ff_chunk):
    n, d_model = x2d.shape
    row = lambda width: pl.BlockSpec((tm, width), lambda i: (i, 0))
    const = lambda a: pl.BlockSpec(a.shape, lambda i: (0,) * a.ndim,
                                   pipeline_mode=pl.Buffered(1))
    scratch = []
    for d in dils:
        if d > 1:
            scratch += [pltpu.VMEM((WIDTH_B // LANES, tm, LANES), F32),
                        pltpu.VMEM((tm, LANES), F32)]
    return pl.pallas_call(
        functools.partial(_post_kernel, dils=dils, ff_chunk=ff_chunk,
                          chain_rows=tuple(tm * f // sum(_ROW_CHAINS) for f in _ROW_CHAINS)),
        grid=(n // tm,),
        in_specs=[row(d_model), row(p2d.shape[1]), row(WIDTH_A)]
                 + [pl.BlockSpec((tm // d, d * WIDTH_B), lambda i: (i, 0)) for d in dils]
                 + [pl.BlockSpec((tm // d, d * LANES), lambda i: (i, 0)) for d in dils]
                 + [const(a) for a in consts],
        out_specs=row(d_model),
        out_shape=jax.ShapeDtypeStruct((n, d_model), x2d.dtype),
        scratch_shapes=scratch,
        compiler_params=pltpu.CompilerParams(
            dimension_semantics=("parallel",), vmem_limit_bytes=_VMEM_LIMIT),
        name="post",
    )(x2d, p2d, oa, *obs, *stats, *consts)


def kernel(x, p, rel_bias_table, g_pre_mix, w_in, sink_a, g_out_a, g_out_b, w_o, g_post_mix,
           g_pre_mlp, w_up, w_down, g_post_mlp, w_ple_proj, w_ple_gate, b_ple_gate, g_post_ple):
    b, s, d_model = x.shape
    depth = w_in.shape[0]
    n = b * s
    tm = 512
    dils = tuple(dil for _, dil in DILATED_PATTERNS)

    def pair_heads(a, axis):
        shape = a.shape
        a = a.reshape(shape[:axis] + (N_KV_A, GROUP_A, HEAD_DIM) + shape[axis + 1:])
        return jnp.swapaxes(a, axis, axis + 1).reshape(shape)

    idx_a, shifts_a = _band_layout(WINDOW_A, 3 * WINDOW_A, (0, WINDOW_A, 2 * WINDOW_A),
                                   1, WINDOW_A)
    half = DILATED_PATTERNS[0][0] // (2 * DILATED_PATTERNS[0][1])
    idx_b = []
    for window, dil in DILATED_PATTERNS:
        assert window // (2 * dil) == half
        idx, shifts_b = _band_layout(2 * half, 4 * half, (0, half, 2 * half), dil, half)
        idx_b.append(idx)
    table = rel_bias_table.astype(F32) * LOG2E
    bias_a, bias_b = _bias(
        table.T,
        jnp.asarray(idx_a)[None, None], shifts_a,
        (len(shifts_a), N_HEADS_A, WINDOW_A, 3 * WINDOW_A),
        jnp.asarray(np.stack(idx_b))[:, None], shifts_b,
        (len(DILATED_PATTERNS) * len(shifts_b), N_HEADS_B, 2 * half, 4 * half))

    folds_ok = [d for d in dils if d > 1 and (2 * half) % (16 * d) == 0]
    layouts = tuple(min(folds_ok) if d == 1 and folds_ok else d for d in dils)
    copy_dils = tuple(sorted(set(layouts)))
    bias_of = []
    for pi, (d, layout) in enumerate(zip(dils, layouts)):
        fold = layout // d
        if fold == 1:
            bias_of.append((bias_b, pi))
            continue
        q_off = (np.arange(2 * half) % (2 * half // fold)) * fold + np.arange(2 * half) // (2 * half // fold)
        k_off = (np.arange(4 * half) % (4 * half // fold)) * fold + np.arange(4 * half) // (4 * half // fold)
        rel = np.stack([k_off[None, :] - off - q_off[:, None] for off in (0, half, 2 * half)])
        idx = np.where(np.abs(rel) <= half, _t5_bucket_np(rel * d), -1).astype(np.int32)
        bias_of.append((_folded_bias(table, jnp.asarray(idx), N_HEADS_A, N_HEADS_B), 0))

    expand_np = np.zeros((2 * LANES, WIDTH_B), np.float32)
    for hh in range(N_HEADS_B):
        expand_np[hh, hh * HEAD_DIM:(hh + 1) * HEAD_DIM] = 1.0
        expand_np[LANES + hh, hh * HEAD_DIM:(hh + 1) * HEAD_DIM] = 1.0
    expand = jnp.asarray(expand_np, BF16)

    h2d = x.reshape(n, d_model)
    row = lambda a: a.reshape(1, -1).astype(F32)
    for i in range(depth):
        scale = HEAD_DIM ** -0.5 * LOG2E
        w = w_in[i]
        o0 = WIDTH_A + 2 * KV_WIDTH_A
        w_all = jnp.concatenate([
            pair_heads(w[:, :WIDTH_A], 1) * scale,
            w[:, WIDTH_A:o0],
            w[:, o0:o0 + WIDTH_B] * scale,
            w[:, o0 + WIDTH_B:],
        ], axis=1).astype(BF16)
        outs = _inproj(h2d, row(g_pre_mix[i]), w_all, 2 * tm, copy_dils)
        qa, ka, va = (t.reshape(b, s, t.shape[-1]) for t in outs[:3])

        o_a = _windowed(qa, ka, va, bias_a, sink_a[i].astype(F32) * LOG2E)
        obs, stats = [], []
        for (window, dil), layout, (bias, bias_index) in zip(DILATED_PATTERNS, layouts, bias_of):
            ci = copy_dils.index(layout)
            qd, kd, vd = outs[3 + 3 * ci:6 + 3 * ci]
            o, st = _dilated(qd, kd, vd, bias, bias_index, b, dil, layout, window // (2 * dil))
            obs.append(o)
            stats.append(st)

        wo = jnp.concatenate([pair_heads(w_o[i][:WIDTH_A], 0), w_o[i][WIDTH_A:]],
                             axis=0).astype(BF16)
        consts = (expand, row(pair_heads(g_out_a[i], 0)), row(g_out_b[i]), wo, row(g_post_mix[i]),
                  row(g_pre_mlp[i]), w_up[i].astype(BF16), w_down[i].astype(BF16),
                  row(g_post_mlp[i]), w_ple_proj[i].astype(BF16), w_ple_gate[i].astype(BF16),
                  row(b_ple_gate[i]), row(g_post_ple[i]))
        h2d = _post(h2d, p[i].reshape(n, -1), o_a.reshape(n, WIDTH_A), obs, stats, layouts,
                    consts, tm, 2048)
    return h2d.reshape(b, s, d_model)
```

```python
import functools
import math

import jax
import jax.numpy as jnp
import numpy as np
from jax import lax
from jax.experimental import pallas as pl
from jax.experimental.pallas import tpu as pltpu

HEAD_DIM = 64
N_HEADS_A = 8
N_KV_A = 2
GROUP_A = N_HEADS_A // N_KV_A
WINDOW_A = 128
N_HEADS_B = 8
DILATED_PATTERNS = ((128, 1), (512, 4), (2048, 16))
WIDTH_A = N_HEADS_A * HEAD_DIM
WIDTH_B = N_HEADS_B * HEAD_DIM
KV_WIDTH_A = N_KV_A * HEAD_DIM
NUM_BUCKETS = 32
MAX_DISTANCE = 1024
EPS = 1e-6
NEG = -1e30
LOG2E = 1.4426950408889634

LANES = 128
SUBLANES = 8
F32 = jnp.float32
BF16 = jnp.bfloat16

_VMEM_LIMIT = 56 * 1024 * 1024
_BLOCKS_PER_TRIP = 16
_WIN_LOOKAHEAD = 2
_DIL_LOOKAHEAD = 4
_ROW_CHAINS = (1, 1)


def _rms(x, g):
    ms = jnp.mean(x * x, axis=-1, keepdims=True)
    return (x * lax.rsqrt(ms + EPS)) * g


def _t5_bucket_np(rel):
    half = NUM_BUCKETS // 2
    max_exact = half // 2
    sign = np.where(rel > 0, half, 0)
    n = np.abs(rel)
    nf = np.maximum(n, 1).astype(np.float32)
    large = max_exact + (np.log(nf / np.float32(max_exact))
                         / np.float32(math.log(MAX_DISTANCE / max_exact))
                         * np.float32(half - max_exact)).astype(np.int32)
    large = np.minimum(large, half - 1)
    return (sign + np.where(n < max_exact, n, large)).astype(np.int32)


def _band_layout(q_len, k_len, offs, dil, half_window):
    center = q_len - 1 + max(offs)
    length = -(-(center + k_len) // LANES) * LANES
    rel = np.arange(length) - center
    idx = np.where(np.abs(rel) <= half_window, _t5_bucket_np(rel * dil), -1).astype(np.int32)
    return idx, tuple((off - center) % length for off in offs)


def _bias_kernel(tab_ref, idxa_ref, idxb_ref, outa_ref, outb_ref, *, shifts_a, shifts_b):
    def build(idx_ref, out_ref, head0, shifts):
        n_pat, _, length = idx_ref.shape
        _, n_heads, q_len, k_len = out_ref.shape
        for p in range(n_pat):
            idx = idx_ref[p]
            vec = jnp.full((n_heads, length), NEG, F32)
            for b in range(NUM_BUCKETS):
                vec = jnp.where(idx == b, tab_ref[head0:head0 + n_heads, b:b + 1], vec)
            for h in range(n_heads):
                rows = jnp.broadcast_to(vec[h:h + 1, :], (q_len, length))
                for v, s in enumerate(shifts):
                    band = pltpu.roll(rows, s, 1, stride=1, stride_axis=0)
                    out_ref[p * len(shifts) + v, h] = band[:, :k_len]

    build(idxa_ref, outa_ref, 0, shifts_a)
    build(idxb_ref, outb_ref, N_HEADS_A, shifts_b)


def _bias(table_t, idx_a, shifts_a, shape_a, idx_b, shifts_b, shape_b):
    vmem = pl.BlockSpec(memory_space=pltpu.VMEM)
    return pl.pallas_call(
        functools.partial(_bias_kernel, shifts_a=shifts_a, shifts_b=shifts_b),
        in_specs=[vmem, vmem, vmem],
        out_specs=[vmem, vmem],
        out_shape=[jax.ShapeDtypeStruct(shape_a, F32), jax.ShapeDtypeStruct(shape_b, F32)],
        compiler_params=pltpu.CompilerParams(vmem_limit_bytes=_VMEM_LIMIT),
        name="bias",
    )(table_t, idx_a, idx_b)


def _inproj_kernel(x_ref, g_ref, w_ref, *refs, dils, stage_dils):
    n_pat = len(dils)
    outs = refs[:3 + 3 * n_pat]
    scr = refs[3 + 3 * n_pat]
    stage_scr = dict(zip(stage_dils, refs[4 + 3 * n_pat:]))
    tm = x_ref.shape[0]
    n_slab = WIDTH_B // LANES

    x = x_ref[...]
    u = (x * g_ref[...]).astype(BF16)
    rstd = lax.rsqrt(jnp.mean(x * x, axis=-1, keepdims=True) + EPS)

    col = sum(o_ref.shape[-1] for o_ref in outs[:3])
    for t in range(3):
        seg = rstd * jnp.dot(u, w_ref[:, col:col + WIDTH_B], preferred_element_type=F32)
        col += WIDTH_B
        if any(d > 1 for d in dils):
            for c in range(n_slab):
                scr[t, c] = seg[:, c * LANES:(c + 1) * LANES]
        for pi, d in enumerate(dils):
            o_ref = outs[3 + 3 * pi + t]
            if d == 1:
                o_ref[...] = seg.astype(BF16)
                continue
            base = max([b for b in stage_dils if b < d and d % b == 0], default=1)
            step = d // base
            for r in range(d):
                for c in range(n_slab):
                    rows = pl.ds(r // base, tm // d, stride=step)
                    src = scr.at[t, c] if base == 1 else stage_scr[base].at[t, r % base, c]
                    val = src[rows, :]
                    if d in stage_scr:
                        stage_scr[d][t, r, c] = val
                    lo = r * WIDTH_B + c * LANES
                    o_ref[:, lo:lo + LANES] = val.astype(BF16)
    col = 0
    for o_ref in outs[:3]:
        width = o_ref.shape[-1]
        o_ref[...] = (rstd * jnp.dot(u, w_ref[:, col:col + width],
                                     preferred_element_type=F32)).astype(BF16)
        col += width


def _inproj(x2d, g, w, tm, dils):
    n, d_model = x2d.shape
    shapes = [(n, WIDTH_A), (n, KV_WIDTH_A), (n, KV_WIDTH_A)]
    blocks = [(tm, WIDTH_A), (tm, KV_WIDTH_A), (tm, KV_WIDTH_A)]
    for d in dils:
        assert tm % (16 * d) == 0 and n % d == 0
        shapes += [(n // d, d * WIDTH_B)] * 3
        blocks += [(tm // d, d * WIDTH_B)] * 3
    stage_dils = tuple(b for b in dils if b > 1 and any(d > b and d % b == 0 for d in dils))
    n_slab = WIDTH_B // LANES
    return pl.pallas_call(
        functools.partial(_inproj_kernel, dils=dils, stage_dils=stage_dils),
        grid=(n // tm,),
        in_specs=[
            pl.BlockSpec((tm, d_model), lambda i: (i, 0)),
            pl.BlockSpec((1, d_model), lambda i: (0, 0)),
            pl.BlockSpec(w.shape, lambda i: (0, 0)),
        ],
        out_specs=[pl.BlockSpec(bs, lambda i: (i, 0)) for bs in blocks],
        out_shape=[jax.ShapeDtypeStruct(sh, BF16) for sh in shapes],
        scratch_shapes=[pltpu.VMEM((3, n_slab, tm, LANES), F32)]
                       + [pltpu.VMEM((3, b, n_slab, tm // b, LANES), F32) for b in stage_dils],
        compiler_params=pltpu.CompilerParams(
            dimension_semantics=("parallel",), vmem_limit_bytes=_VMEM_LIMIT),
        name="inproj",
    )(x2d, g, w)


def _nt_dot(a, b):
    return lax.dot_general(a, b, (((1,), (1,)), ((), ())), preferred_element_type=F32)


def _win_kernel(sink_ref, q_ref, k_ref, v_ref, bias_ref, o_ref, *, seq, blk, unroll):
    nblk = seq // blk
    lane = lax.broadcasted_iota(jnp.int32, (blk, LANES), 1)
    low = lane < HEAD_DIM

    klen = 3 * blk

    def block_context(n):
        q0 = pl.multiple_of(n * blk, blk)
        k0 = pl.multiple_of(jnp.clip(q0 - blk, 0, seq - klen), blk)
        variant = jnp.where(n == 0, 0, jnp.where(n == nblk - 1, 2, 1))
        k1 = k_ref[pl.ds(k0, 2 * blk), :]
        k2 = k_ref[pl.ds(k0 + 2 * blk, blk), :]
        v1 = v_ref[pl.ds(k0, 2 * blk), :]
        v2 = v_ref[pl.ds(k0 + 2 * blk, blk), :]
        zero = jnp.zeros_like(k2)
        one_low = low.astype(F32).astype(BF16)
        one_high = (~low).astype(F32).astype(BF16)
        return dict(
            q0=q0, variant=variant, k1=k1,
            v1_aug=jnp.concatenate([v1, jnp.ones_like(v1)], axis=1),
            k2_pack=jnp.concatenate([jnp.where(low, k2, zero), jnp.where(low, zero, k2)], axis=0),
            v2_pack=jnp.concatenate(
                [jnp.concatenate([jnp.where(low, v2, zero), one_low], axis=1),
                 jnp.concatenate([jnp.where(low, zero, v2), one_high], axis=1)], axis=0))

    def scores(ctx, j):
        q_tile = q_ref[pl.ds(ctx["q0"], blk), j * LANES:(j + 1) * LANES]
        s2 = _nt_dot(q_tile, ctx["k2_pack"])
        out = []
        for hf in range(N_KV_A):
            qm = jnp.where(low if hf == 0 else ~low, q_tile, jnp.zeros_like(q_tile))
            out.append(jnp.concatenate(
                [_nt_dot(qm, ctx["k1"]), s2[:, hf * LANES:(hf + 1) * LANES]], axis=1)
                + bias_ref[ctx["variant"], hf * GROUP_A + j])
        return out

    def softmax_pv(ctx, j, logits):
        nums, dens, e2s = [], [], []
        for hf, s in enumerate(logits):
            sink = sink_ref[hf * GROUP_A + j]
            m = jnp.maximum(jnp.max(s, axis=-1, keepdims=True), sink)
            e = jnp.exp2(s - m).astype(BF16)
            o1 = jnp.dot(e[:, :2 * blk], ctx["v1_aug"], preferred_element_type=F32)
            nums.append(o1[:, :LANES])
            dens.append(o1[:, LANES:] + jnp.exp2(sink - m))
            e2s.append(e[:, 2 * blk:])
        o2 = jnp.dot(jnp.concatenate(e2s, axis=1), ctx["v2_pack"], preferred_element_type=F32)
        num = jnp.where(low, nums[0], nums[1]) + o2[:, :LANES]
        den = jnp.where(low, dens[0], dens[1]) + o2[:, LANES:]
        o_ref[pl.ds(ctx["q0"], blk), j * LANES:(j + 1) * LANES] = (num / den).astype(o_ref.dtype)

    def trip(t, c):
        pending = []
        for i in range(unroll):
            ctx = block_context(t * unroll + i)
            for j in range(GROUP_A):
                pending.append((ctx, j, scores(ctx, j)))
                if len(pending) > _WIN_LOOKAHEAD:
                    softmax_pv(*pending.pop(0))
        for task in pending:
            softmax_pv(*task)
        return c

    lax.fori_loop(0, nblk // unroll, trip, 0)


def _windowed(qa, ka, va, bias, sink):
    b, s, _ = qa.shape
    blk = WINDOW_A
    nblk = s // blk
    assert s % blk == 0 and nblk >= 3
    kern = functools.partial(_win_kernel, seq=s, blk=blk,
                             unroll=math.gcd(nblk, _BLOCKS_PER_TRIP))
    return pl.pallas_call(
        kern,
        grid=(b,),
        in_specs=[
            pl.BlockSpec(memory_space=pltpu.SMEM),
            pl.BlockSpec((None, s, WIDTH_A), lambda i: (i, 0, 0)),
            pl.BlockSpec((None, s, KV_WIDTH_A), lambda i: (i, 0, 0)),
            pl.BlockSpec((None, s, KV_WIDTH_A), lambda i: (i, 0, 0)),
            pl.BlockSpec(bias.shape, lambda i: (0, 0, 0, 0)),
        ],
        out_specs=pl.BlockSpec((None, s, WIDTH_A), lambda i: (i, 0, 0)),
        out_shape=jax.ShapeDtypeStruct((b, s, WIDTH_A), BF16),
        compiler_params=pltpu.CompilerParams(
            dimension_semantics=("parallel",), vmem_limit_bytes=_VMEM_LIMIT),
        name="win_gqa",
    )(sink, qa, ka, va, bias)


def _dil_kernel(q_ref, k_ref, v_ref, bias_ref, o_ref, st_ref, *, ls, half, n_res, unroll):
    qb = 2 * half
    kw = 4 * half
    nblk = ls // qb
    lane = lax.broadcasted_iota(jnp.int32, (qb, LANES), 1)
    low = lane < HEAD_DIM

    n_pairs = N_HEADS_B // 2

    def scores(n, res_i, j):
        q0 = pl.multiple_of(n * qb, qb)
        k0 = pl.multiple_of(jnp.clip(q0 - half, 0, ls - kw), half)
        variant = jnp.where(n == 0, 0, jnp.where(n == nblk - 1, 2, 1))
        cols = slice(res_i * WIDTH_B + j * LANES, res_i * WIDTH_B + (j + 1) * LANES)
        k_win = k_ref[pl.ds(k0, kw), cols]
        q_tile = q_ref[pl.ds(q0, qb), cols]
        logits = []
        for hf in range(2):
            qm = jnp.where(low if hf == 0 else ~low, q_tile, jnp.zeros_like(q_tile))
            logits.append(_nt_dot(qm, k_win) + bias_ref[variant, 2 * j + hf])
        return dict(q0=q0, k0=k0, cols=cols, res_i=res_i, j=j, logits=logits)

    def softmax_pv(task, stats):
        j, cols = task["j"], task["cols"]
        v_win = v_ref[pl.ds(task["k0"], kw), cols]
        v_aug = jnp.concatenate([v_win, jnp.ones_like(v_win)], axis=1)
        st_acc = jnp.zeros((qb, LANES), F32) if j == 0 else stats.pop()
        nums = []
        for hf, s in enumerate(task["logits"]):
            h = 2 * j + hf
            m = jnp.max(s, axis=-1, keepdims=True)
            o = jnp.dot(jnp.exp2(s - m).astype(BF16), v_aug, preferred_element_type=F32)
            nums.append(o[:, :LANES])
            st_acc = jnp.where(lane == h, m,
                               jnp.where(lane == N_HEADS_B + h, o[:, LANES:], st_acc))
        o_ref[pl.ds(task["q0"], qb), cols] = jnp.where(low, nums[0], nums[1]).astype(o_ref.dtype)
        if j == n_pairs - 1:
            r = task["res_i"]
            st_ref[pl.ds(task["q0"], qb), r * LANES:(r + 1) * LANES] = st_acc
        else:
            stats.append(st_acc)

    def trip(t, c):
        pending, stats = [], []
        for i in range(unroll):
            for res_i in range(n_res):
                for j in range(n_pairs):
                    pending.append(scores(t * unroll + i, res_i, j))
                    if len(pending) > _DIL_LOOKAHEAD:
                        softmax_pv(pending.pop(0), stats)
        for task in pending:
            softmax_pv(task, stats)
        return c

    lax.fori_loop(0, nblk // unroll, trip, 0)


def _dilated(qd, kd, vd, bias, pattern, batch, dil, half):
    rows, w_all = qd.shape
    assert w_all == dil * WIDTH_B
    ls = rows // batch
    nblk = ls // (2 * half)
    assert ls % (2 * half) == 0 and nblk >= 2
    n_res = math.gcd(dil, max(1, _BLOCKS_PER_TRIP // nblk))
    unroll = math.gcd(nblk, max(1, _BLOCKS_PER_TRIP // n_res))
    view = lambda t: t.reshape(batch, ls, w_all)
    kern = functools.partial(_dil_kernel, ls=ls, half=half, n_res=n_res, unroll=unroll)
    spec = pl.BlockSpec((None, ls, n_res * WIDTH_B), lambda i, r: (i, 0, r))
    n_var = 3
    o, st = pl.pallas_call(
        kern,
        grid=(batch, dil // n_res),
        in_specs=[spec, spec, spec,
                  pl.BlockSpec((n_var,) + bias.shape[1:], lambda i, r: (pattern, 0, 0, 0))],
        out_specs=[spec, pl.BlockSpec((None, ls, n_res * LANES), lambda i, r: (i, 0, r))],
        out_shape=[jax.ShapeDtypeStruct((batch, ls, w_all), BF16),
                   jax.ShapeDtypeStruct((batch, ls, dil * LANES), F32)],
        compiler_params=pltpu.CompilerParams(
            dimension_semantics=("parallel", "parallel"), vmem_limit_bytes=_VMEM_LIMIT),
        name=f"dilated_d{dil}",
    )(view(qd), view(kd), view(vd), bias)
    return o.reshape(rows, w_all), st.reshape(rows, dil * LANES)


def _post_kernel(x_ref, p_ref, oa_ref, *refs, dils, ff_chunk, chain_rows):
    n_pat = len(dils)
    ob_refs = refs[:n_pat]
    st_refs = refs[n_pat:2 * n_pat]
    (expand_ref, goa_ref, gob_ref, wo_ref, gpm_ref, gpre_ref, wup_ref, wdn_ref,
     gmlp_ref, wpp_ref, wpg_ref, bpg_ref, gple_ref, out_ref) = refs[2 * n_pat:2 * n_pat + 14]
    scratch = refs[2 * n_pat + 14:]
    tm = x_ref.shape[0]
    assert sum(chain_rows) == tm
    n_slab = WIDTH_B // LANES

    def expand(w):
        head_lane = lax.broadcasted_iota(jnp.int32, w.shape, 1) < N_HEADS_B
        w = jnp.where(head_lane, w, 0.0)
        hi = w.astype(BF16)
        lo = (w - hi.astype(F32)).astype(BF16)
        return jnp.dot(jnp.concatenate([hi, lo], axis=1), expand_ref[...],
                       preferred_element_type=F32)

    def combine(st):
        rows = st["rows"]
        nums, stats = [], []
        si = 0
        for d, o_ref, s_ref in zip(dils, ob_refs, st_refs):
            if d == 1:
                nums.append(o_ref[rows, :].astype(F32))
                stats.append(s_ref[rows, :])
                continue
            o_scr, s_scr = scratch[si], scratch[si + 1]
            si += 2
            assert rows.start % d == 0 and rows.stop % d == 0
            src = slice(rows.start // d, rows.stop // d)
            for r in range(d):
                dst = pl.ds(rows.start + r, (rows.stop - rows.start) // d, stride=d)
                for c in range(n_slab):
                    lo = r * WIDTH_B + c * LANES
                    o_scr[c, dst, :] = o_ref[src, lo:lo + LANES].astype(F32)
                s_scr[dst, :] = s_ref[src, r * LANES:(r + 1) * LANES]
            nums.append(jnp.concatenate([o_scr[c, rows, :] for c in range(n_slab)], axis=1))
            stats.append(s_scr[rows, :])
        mx = functools.reduce(jnp.maximum, stats)
        scales = [jnp.exp2(s - mx) for s in stats]
        dens = [pltpu.roll(s, LANES - N_HEADS_B, 1) for s in stats]
        tot = functools.reduce(lambda a, b: a + b, [sc * dn for sc, dn in zip(scales, dens)])
        ob = None
        for sc, num in zip(scales, nums):
            term = expand(sc / tot) * num
            ob = term if ob is None else ob + term
        st["oa"] = _rms(oa_ref[rows, :].astype(F32), goa_ref[...]).astype(BF16)
        st["ob"] = _rms(ob, gob_ref[...]).astype(BF16)

    def mix(st):
        m = (jnp.dot(st.pop("oa"), wo_ref[:WIDTH_A, :], preferred_element_type=F32)
             + jnp.dot(st.pop("ob"), wo_ref[WIDTH_A:, :], preferred_element_type=F32))
        st["h"] = x_ref[st["rows"], :] + _rms(m, gpm_ref[...])
        st["v"] = _rms(st["h"], gpre_ref[...]).astype(BF16)
        st["ff"] = None

    def mlp_chunk(c):
        def stage(st):
            a = jnp.maximum(
                jnp.dot(st["v"], wup_ref[:, c:c + ff_chunk], preferred_element_type=F32), 0.0)
            t = jnp.dot((a * a).astype(BF16), wdn_ref[c:c + ff_chunk, :],
                        preferred_element_type=F32)
            st["ff"] = t if st["ff"] is None else st["ff"] + t
        return stage

    def ple(st):
        h = st["h"] + _rms(st.pop("ff"), gmlp_ref[...])
        gate = jax.nn.sigmoid(
            jnp.dot(h.astype(BF16), wpg_ref[...], preferred_element_type=F32) + bpg_ref[...])
        emb = jnp.dot(p_ref[st["rows"], :].astype(BF16), wpp_ref[...],
                      preferred_element_type=F32)
        out_ref[st["rows"], :] = h + _rms(gate * emb, gple_ref[...])

    stages = [combine, mix] + [mlp_chunk(c) for c in range(0, wup_ref.shape[1], ff_chunk)] + [ple]
    starts = np.cumsum((0,) + chain_rows)
    states = [{"rows": slice(int(lo), int(hi))} for lo, hi in zip(starts[:-1], starts[1:])]
    for stage in stages:
        for st in states:
            stage(st)


def _post(x2d, p2d, oa, obs, stats, dils, consts, tm, ff_chunk):
    n, d_model = x2d.shape
    row = lambda width: pl.BlockSpec((tm, width), lambda i: (i, 0))
    const = lambda a: pl.BlockSpec(a.shape, lambda i: (0,) * a.ndim,
                                   pipeline_mode=pl.Buffered(1))
    scratch = []
    for d in dils:
        if d > 1:
            scratch += [pltpu.VMEM((WIDTH_B // LANES, tm, LANES), F32),
                        pltpu.VMEM((tm, LANES), F32)]
    return pl.pallas_call(
        functools.partial(_post_kernel, dils=dils, ff_chunk=ff_chunk,
                          chain_rows=tuple(tm * f // sum(_ROW_CHAINS) for f in _ROW_CHAINS)),
        grid=(n // tm,),
        in_specs=[row(d_model), row(p2d.shape[1]), row(WIDTH_A)]
                 + [pl.BlockSpec((tm // d, d * WIDTH_B), lambda i: (i, 0)) for d in dils]
                 + [pl.BlockSpec((tm // d, d * LANES), lambda i: (i, 0)) for d in dils]
                 + [const(a) for a in consts],
        out_specs=row(d_model),
        out_shape=jax.ShapeDtypeStruct((n, d_model), x2d.dtype),
        scratch_shapes=scratch,
        compiler_params=pltpu.CompilerParams(
            dimension_semantics=("parallel",), vmem_limit_bytes=_VMEM_LIMIT),
        name="post",
    )(x2d, p2d, oa, *obs, *stats, *consts)


def kernel(x, p, rel_bias_table, g_pre_mix, w_in, sink_a, g_out_a, g_out_b, w_o, g_post_mix,
           g_pre_mlp, w_up, w_down, g_post_mlp, w_ple_proj, w_ple_gate, b_ple_gate, g_post_ple):
    b, s, d_model = x.shape
    depth = w_in.shape[0]
    n = b * s
    tm = 512
    dils = tuple(dil for _, dil in DILATED_PATTERNS)

    def pair_heads(a, axis):
        shape = a.shape
        a = a.reshape(shape[:axis] + (N_KV_A, GROUP_A, HEAD_DIM) + shape[axis + 1:])
        return jnp.swapaxes(a, axis, axis + 1).reshape(shape)

    idx_a, shifts_a = _band_layout(WINDOW_A, 3 * WINDOW_A, (0, WINDOW_A, 2 * WINDOW_A),
                                   1, WINDOW_A)
    half = DILATED_PATTERNS[0][0] // (2 * DILATED_PATTERNS[0][1])
    idx_b = []
    for window, dil in DILATED_PATTERNS:
        assert window // (2 * dil) == half
        idx, shifts_b = _band_layout(2 * half, 4 * half, (0, half, 2 * half), dil, half)
        idx_b.append(idx)
    bias_a, bias_b = _bias(
        rel_bias_table.T.astype(F32) * LOG2E,
        jnp.asarray(idx_a)[None, None], shifts_a,
        (len(shifts_a), N_HEADS_A, WINDOW_A, 3 * WINDOW_A),
        jnp.asarray(np.stack(idx_b))[:, None], shifts_b,
        (len(DILATED_PATTERNS) * len(shifts_b), N_HEADS_B, 2 * half, 4 * half))

    expand_np = np.zeros((2 * LANES, WIDTH_B), np.float32)
    for hh in range(N_HEADS_B):
        expand_np[hh, hh * HEAD_DIM:(hh + 1) * HEAD_DIM] = 1.0
        expand_np[LANES + hh, hh * HEAD_DIM:(hh + 1) * HEAD_DIM] = 1.0
    expand = jnp.asarray(expand_np, BF16)

    h2d = x.reshape(n, d_model)
    row = lambda a: a.reshape(1, -1).astype(F32)
    for i in range(depth):
        scale = HEAD_DIM ** -0.5 * LOG2E
        w = w_in[i]
        o0 = WIDTH_A + 2 * KV_WIDTH_A
        w_all = jnp.concatenate([
            pair_heads(w[:, :WIDTH_A], 1) * scale,
            w[:, WIDTH_A:o0],
            w[:, o0:o0 + WIDTH_B] * scale,
            w[:, o0 + WIDTH_B:],
        ], axis=1).astype(BF16)
        outs = _inproj(h2d, row(g_pre_mix[i]), w_all, tm, dils)
        qa, ka, va = (t.reshape(b, s, t.shape[-1]) for t in outs[:3])

        o_a = _windowed(qa, ka, va, bias_a, sink_a[i].astype(F32) * LOG2E)
        obs, stats = [], []
        for pi, (window, dil) in enumerate(DILATED_PATTERNS):
            qd, kd, vd = outs[3 + 3 * pi:6 + 3 * pi]
            o, st = _dilated(qd, kd, vd, bias_b, pi, b, dil, window // (2 * dil))
            obs.append(o)
            stats.append(st)

        wo = jnp.concatenate([pair_heads(w_o[i][:WIDTH_A], 0), w_o[i][WIDTH_A:]],
                             axis=0).astype(BF16)
        consts = (expand, row(pair_heads(g_out_a[i], 0)), row(g_out_b[i]), wo, row(g_post_mix[i]),
                  row(g_pre_mlp[i]), w_up[i].astype(BF16), w_down[i].astype(BF16),
                  row(g_post_mlp[i]), w_ple_proj[i].astype(BF16), w_ple_gate[i].astype(BF16),
                  row(b_ple_gate[i]), row(g_post_ple[i]))
        h2d = _post(h2d, p[i].reshape(n, -1), o_a.reshape(n, WIDTH_A), obs, stats, dils,
                    consts, tm, 2048)
    return h2d.reshape(b, s, d_model)
```

```python
import functools
import math

import jax
import jax.numpy as jnp
import numpy as np
from jax import lax
from jax.experimental import pallas as pl
from jax.experimental.pallas import tpu as pltpu

HEAD_DIM = 64
N_HEADS_A = 8
N_KV_A = 2
GROUP_A = N_HEADS_A // N_KV_A
WINDOW_A = 128
N_HEADS_B = 8
DILATED_PATTERNS = ((128, 1), (512, 4), (2048, 16))
WIDTH_A = N_HEADS_A * HEAD_DIM
WIDTH_B = N_HEADS_B * HEAD_DIM
KV_WIDTH_A = N_KV_A * HEAD_DIM
NUM_BUCKETS = 32
MAX_DISTANCE = 1024
EPS = 1e-6
NEG = -1e30
LOG2E = 1.4426950408889634

LANES = 128
SUBLANES = 8
F32 = jnp.float32
BF16 = jnp.bfloat16

_VMEM_LIMIT = 56 * 1024 * 1024
_BLOCKS_PER_TRIP = 32
_WIN_LOOKAHEAD = 2
_DIL_LOOKAHEAD = 4
_ROW_CHAINS = (1, 1)


def _rms(x, g):
    ms = jnp.mean(x * x, axis=-1, keepdims=True)
    return (x * lax.rsqrt(ms + EPS)) * g


def _t5_bucket_np(rel):
    half = NUM_BUCKETS // 2
    max_exact = half // 2
    sign = np.where(rel > 0, half, 0)
    n = np.abs(rel)
    nf = np.maximum(n, 1).astype(np.float32)
    large = max_exact + (np.log(nf / np.float32(max_exact))
                         / np.float32(math.log(MAX_DISTANCE / max_exact))
                         * np.float32(half - max_exact)).astype(np.int32)
    large = np.minimum(large, half - 1)
    return (sign + np.where(n < max_exact, n, large)).astype(np.int32)


def _band_layout(q_len, k_len, offs, dil, half_window):
    center = q_len - 1 + max(offs)
    length = -(-(center + k_len) // LANES) * LANES
    rel = np.arange(length) - center
    idx = np.where(np.abs(rel) <= half_window, _t5_bucket_np(rel * dil), -1).astype(np.int32)
    return idx, tuple((off - center) % length for off in offs)


def _bias_kernel(tab_ref, idxa_ref, idxb_ref, outa_ref, outb_ref, *, shifts_a, shifts_b):
    def build(idx_ref, out_ref, head0, shifts):
        n_pat, _, length = idx_ref.shape
        _, n_heads, q_len, k_len = out_ref.shape
        for p in range(n_pat):
            idx = idx_ref[p]
            vec = jnp.full((n_heads, length), NEG, F32)
            for b in range(NUM_BUCKETS):
                vec = jnp.where(idx == b, tab_ref[head0:head0 + n_heads, b:b + 1], vec)
            for h in range(n_heads):
                rows = jnp.broadcast_to(vec[h:h + 1, :], (q_len, length))
                for v, s in enumerate(shifts):
                    band = pltpu.roll(rows, s, 1, stride=1, stride_axis=0)
                    out_ref[p * len(shifts) + v, h] = band[:, :k_len]

    build(idxa_ref, outa_ref, 0, shifts_a)
    build(idxb_ref, outb_ref, N_HEADS_A, shifts_b)


def _bias(table_t, idx_a, shifts_a, shape_a, idx_b, shifts_b, shape_b):
    vmem = pl.BlockSpec(memory_space=pltpu.VMEM)
    return pl.pallas_call(
        functools.partial(_bias_kernel, shifts_a=shifts_a, shifts_b=shifts_b),
        in_specs=[vmem, vmem, vmem],
        out_specs=[vmem, vmem],
        out_shape=[jax.ShapeDtypeStruct(shape_a, F32), jax.ShapeDtypeStruct(shape_b, F32)],
        compiler_params=pltpu.CompilerParams(vmem_limit_bytes=_VMEM_LIMIT),
        name="bias",
    )(table_t, idx_a, idx_b)


def _inproj_kernel(x_ref, g_ref, w_ref, *refs, dils, stage_dils):
    n_pat = len(dils)
    outs = refs[:3 + 3 * n_pat]
    scr = refs[3 + 3 * n_pat]
    stage_scr = dict(zip(stage_dils, refs[4 + 3 * n_pat:]))
    tm = x_ref.shape[0]
    n_slab = WIDTH_B // LANES

    x = x_ref[...]
    u = (x * g_ref[...]).astype(BF16)
    rstd = lax.rsqrt(jnp.mean(x * x, axis=-1, keepdims=True) + EPS)

    col = sum(o_ref.shape[-1] for o_ref in outs[:3])
    for t in range(3):
        seg = rstd * jnp.dot(u, w_ref[:, col:col + WIDTH_B], preferred_element_type=F32)
        col += WIDTH_B
        if any(d > 1 for d in dils):
            for c in range(n_slab):
                scr[t, c] = seg[:, c * LANES:(c + 1) * LANES]
        for pi, d in enumerate(dils):
            o_ref = outs[3 + 3 * pi + t]
            if d == 1:
                o_ref[...] = seg.astype(BF16)
                continue
            base = max([b for b in stage_dils if b < d and d % b == 0], default=1)
            step = d // base
            for r in range(d):
                for c in range(n_slab):
                    rows = pl.ds(r // base, tm // d, stride=step)
                    src = scr.at[t, c] if base == 1 else stage_scr[base].at[t, r % base, c]
                    val = src[rows, :]
                    if d in stage_scr:
                        stage_scr[d][t, r, c] = val
                    lo = r * WIDTH_B + c * LANES
                    o_ref[:, lo:lo + LANES] = val.astype(BF16)
    col = 0
    for o_ref in outs[:3]:
        width = o_ref.shape[-1]
        o_ref[...] = (rstd * jnp.dot(u, w_ref[:, col:col + width],
                                     preferred_element_type=F32)).astype(BF16)
        col += width


def _inproj(x2d, g, w, tm, dils):
    n, d_model = x2d.shape
    shapes = [(n, WIDTH_A), (n, KV_WIDTH_A), (n, KV_WIDTH_A)]
    blocks = [(tm, WIDTH_A), (tm, KV_WIDTH_A), (tm, KV_WIDTH_A)]
    for d in dils:
        assert tm % (16 * d) == 0 and n % d == 0
        shapes += [(n // d, d * WIDTH_B)] * 3
        blocks += [(tm // d, d * WIDTH_B)] * 3
    stage_dils = tuple(b for b in dils if b > 1 and any(d > b and d % b == 0 for d in dils))
    n_slab = WIDTH_B // LANES
    return pl.pallas_call(
        functools.partial(_inproj_kernel, dils=dils, stage_dils=stage_dils),
        grid=(n // tm,),
        in_specs=[
            pl.BlockSpec((tm, d_model), lambda i: (i, 0)),
            pl.BlockSpec((1, d_model), lambda i: (0, 0)),
            pl.BlockSpec(w.shape, lambda i: (0, 0)),
        ],
        out_specs=[pl.BlockSpec(bs, lambda i: (i, 0)) for bs in blocks],
        out_shape=[jax.ShapeDtypeStruct(sh, BF16) for sh in shapes],
        scratch_shapes=[pltpu.VMEM((3, n_slab, tm, LANES), F32)]
                       + [pltpu.VMEM((3, b, n_slab, tm // b, LANES), F32) for b in stage_dils],
        compiler_params=pltpu.CompilerParams(
            dimension_semantics=("parallel",), vmem_limit_bytes=_VMEM_LIMIT),
        name="inproj",
    )(x2d, g, w)


def _nt_dot(a, b):
    return lax.dot_general(a, b, (((1,), (1,)), ((), ())), preferred_element_type=F32)


def _win_kernel(sink_ref, q_ref, k_ref, v_ref, bias_ref, o_ref, *, seq, blk, unroll):
    nblk = seq // blk
    lane = lax.broadcasted_iota(jnp.int32, (blk, LANES), 1)
    low = lane < HEAD_DIM

    klen = 3 * blk

    def block_context(n):
        q0 = pl.multiple_of(n * blk, blk)
        k0 = pl.multiple_of(jnp.clip(q0 - blk, 0, seq - klen), blk)
        variant = jnp.where(n == 0, 0, jnp.where(n == nblk - 1, 2, 1))
        k1 = k_ref[pl.ds(k0, 2 * blk), :]
        k2 = k_ref[pl.ds(k0 + 2 * blk, blk), :]
        v1 = v_ref[pl.ds(k0, 2 * blk), :]
        v2 = v_ref[pl.ds(k0 + 2 * blk, blk), :]
        zero = jnp.zeros_like(k2)
        one_low = low.astype(F32).astype(BF16)
        one_high = (~low).astype(F32).astype(BF16)
        return dict(
            q0=q0, variant=variant, k1=k1,
            v1_aug=jnp.concatenate([v1, jnp.ones_like(v1)], axis=1),
            k2_pack=jnp.concatenate([jnp.where(low, k2, zero), jnp.where(low, zero, k2)], axis=0),
            v2_pack=jnp.concatenate(
                [jnp.concatenate([jnp.where(low, v2, zero), one_low], axis=1),
                 jnp.concatenate([jnp.where(low, zero, v2), one_high], axis=1)], axis=0))

    def scores(ctx, j):
        q_tile = q_ref[pl.ds(ctx["q0"], blk), j * LANES:(j + 1) * LANES]
        s2 = _nt_dot(q_tile, ctx["k2_pack"])
        out = []
        for hf in range(N_KV_A):
            qm = jnp.where(low if hf == 0 else ~low, q_tile, jnp.zeros_like(q_tile))
            out.append(jnp.concatenate(
                [_nt_dot(qm, ctx["k1"]), s2[:, hf * LANES:(hf + 1) * LANES]], axis=1)
                + bias_ref[ctx["variant"], hf * GROUP_A + j])
        return out

    def softmax_pv(ctx, j, logits):
        nums, dens, e2s = [], [], []
        for hf, s in enumerate(logits):
            sink = sink_ref[hf * GROUP_A + j]
            m = jnp.maximum(jnp.max(s, axis=-1, keepdims=True), sink)
            e = jnp.exp2(s - m).astype(BF16)
            o1 = jnp.dot(e[:, :2 * blk], ctx["v1_aug"], preferred_element_type=F32)
            nums.append(o1[:, :LANES])
            dens.append(o1[:, LANES:] + jnp.exp2(sink - m))
            e2s.append(e[:, 2 * blk:])
        o2 = jnp.dot(jnp.concatenate(e2s, axis=1), ctx["v2_pack"], preferred_element_type=F32)
        num = jnp.where(low, nums[0], nums[1]) + o2[:, :LANES]
        den = jnp.where(low, dens[0], dens[1]) + o2[:, LANES:]
        o_ref[pl.ds(ctx["q0"], blk), j * LANES:(j + 1) * LANES] = (num / den).astype(o_ref.dtype)

    def trip(t, c):
        pending = []
        for i in range(unroll):
            ctx = block_context(t * unroll + i)
            for j in range(GROUP_A):
                pending.append((ctx, j, scores(ctx, j)))
                if len(pending) > _WIN_LOOKAHEAD:
                    softmax_pv(*pending.pop(0))
        for task in pending:
            softmax_pv(*task)
        return c

    lax.fori_loop(0, nblk // unroll, trip, 0)


def _windowed(qa, ka, va, bias, sink):
    b, s, _ = qa.shape
    blk = WINDOW_A
    nblk = s // blk
    assert s % blk == 0 and nblk >= 3
    kern = functools.partial(_win_kernel, seq=s, blk=blk,
                             unroll=math.gcd(nblk, _BLOCKS_PER_TRIP))
    return pl.pallas_call(
        kern,
        grid=(b,),
        in_specs=[
            pl.BlockSpec(memory_space=pltpu.SMEM),
            pl.BlockSpec((None, s, WIDTH_A), lambda i: (i, 0, 0)),
            pl.BlockSpec((None, s, KV_WIDTH_A), lambda i: (i, 0, 0)),
            pl.BlockSpec((None, s, KV_WIDTH_A), lambda i: (i, 0, 0)),
            pl.BlockSpec(bias.shape, lambda i: (0, 0, 0, 0)),
        ],
        out_specs=pl.BlockSpec((None, s, WIDTH_A), lambda i: (i, 0, 0)),
        out_shape=jax.ShapeDtypeStruct((b, s, WIDTH_A), BF16),
        compiler_params=pltpu.CompilerParams(
            dimension_semantics=("parallel",), vmem_limit_bytes=_VMEM_LIMIT),
        name="win_gqa",
    )(sink, qa, ka, va, bias)


def _dil_kernel(q_ref, k_ref, v_ref, bias_ref, o_ref, st_ref, *, ls, half, n_res, unroll):
    qb = 2 * half
    kw = 4 * half
    nblk = ls // qb
    lane = lax.broadcasted_iota(jnp.int32, (qb, LANES), 1)
    low = lane < HEAD_DIM

    n_pairs = N_HEADS_B // 2

    def scores(n, res_i, j):
        q0 = pl.multiple_of(n * qb, qb)
        k0 = pl.multiple_of(jnp.clip(q0 - half, 0, ls - kw), half)
        variant = jnp.where(n == 0, 0, jnp.where(n == nblk - 1, 2, 1))
        cols = slice(res_i * WIDTH_B + j * LANES, res_i * WIDTH_B + (j + 1) * LANES)
        k_win = k_ref[pl.ds(k0, kw), cols]
        q_tile = q_ref[pl.ds(q0, qb), cols]
        logits = []
        for hf in range(2):
            qm = jnp.where(low if hf == 0 else ~low, q_tile, jnp.zeros_like(q_tile))
            logits.append(_nt_dot(qm, k_win) + bias_ref[variant, 2 * j + hf])
        return dict(q0=q0, k0=k0, cols=cols, res_i=res_i, j=j, logits=logits)

    def softmax_pv(task, stats):
        j, cols = task["j"], task["cols"]
        v_win = v_ref[pl.ds(task["k0"], kw), cols]
        v_aug = jnp.concatenate([v_win, jnp.ones_like(v_win)], axis=1)
        st_acc = jnp.zeros((qb, LANES), F32) if j == 0 else stats.pop()
        nums = []
        for hf, s in enumerate(task["logits"]):
            h = 2 * j + hf
            m = jnp.max(s, axis=-1, keepdims=True)
            o = jnp.dot(jnp.exp2(s - m).astype(BF16), v_aug, preferred_element_type=F32)
            nums.append(o[:, :LANES])
            st_acc = jnp.where(lane == h, m,
                               jnp.where(lane == N_HEADS_B + h, o[:, LANES:], st_acc))
        o_ref[pl.ds(task["q0"], qb), cols] = jnp.where(low, nums[0], nums[1]).astype(o_ref.dtype)
        if j == n_pairs - 1:
            r = task["res_i"]
            st_ref[pl.ds(task["q0"], qb), r * LANES:(r + 1) * LANES] = st_acc
        else:
            stats.append(st_acc)

    def trip(t, c):
        pending, stats = [], []
        for i in range(unroll):
            for res_i in range(n_res):
                for j in range(n_pairs):
                    pending.append(scores(t * unroll + i, res_i, j))
                    if len(pending) > _DIL_LOOKAHEAD:
                        softmax_pv(pending.pop(0), stats)
        for task in pending:
            softmax_pv(task, stats)
        return c

    lax.fori_loop(0, nblk // unroll, trip, 0)


def _dilated(qd, kd, vd, bias, pattern, batch, dil, half):
    rows, w_all = qd.shape
    assert w_all == dil * WIDTH_B
    ls = rows // batch
    nblk = ls // (2 * half)
    assert ls % (2 * half) == 0 and nblk >= 2
    n_res = math.gcd(dil, max(1, _BLOCKS_PER_TRIP // nblk))
    unroll = math.gcd(nblk, max(1, _BLOCKS_PER_TRIP // n_res))
    view = lambda t: t.reshape(batch, ls, w_all)
    kern = functools.partial(_dil_kernel, ls=ls, half=half, n_res=n_res, unroll=unroll)
    spec = pl.BlockSpec((None, ls, n_res * WIDTH_B), lambda i, r: (i, 0, r))
    n_var = 3
    o, st = pl.pallas_call(
        kern,
        grid=(batch, dil // n_res),
        in_specs=[spec, spec, spec,
                  pl.BlockSpec((n_var,) + bias.shape[1:], lambda i, r: (pattern, 0, 0, 0))],
        out_specs=[spec, pl.BlockSpec((None, ls, n_res * LANES), lambda i, r: (i, 0, r))],
        out_shape=[jax.ShapeDtypeStruct((batch, ls, w_all), BF16),
                   jax.ShapeDtypeStruct((batch, ls, dil * LANES), F32)],
        compiler_params=pltpu.CompilerParams(
            dimension_semantics=("parallel", "parallel"), vmem_limit_bytes=_VMEM_LIMIT),
        name=f"dilated_d{dil}",
    )(view(qd), view(kd), view(vd), bias)
    return o.reshape(rows, w_all), st.reshape(rows, dil * LANES)


def _post_kernel(x_ref, p_ref, oa_ref, *refs, dils, ff_chunk, chain_rows):
    n_pat = len(dils)
    ob_refs = refs[:n_pat]
    st_refs = refs[n_pat:2 * n_pat]
    (expand_ref, goa_ref, gob_ref, wo_ref, gpm_ref, gpre_ref, wup_ref, wdn_ref,
     gmlp_ref, wpp_ref, wpg_ref, bpg_ref, gple_ref, out_ref) = refs[2 * n_pat:2 * n_pat + 14]
    scratch = refs[2 * n_pat + 14:]
    tm = x_ref.shape[0]
    assert sum(chain_rows) == tm
    n_slab = WIDTH_B // LANES

    def expand(w):
        head_lane = lax.broadcasted_iota(jnp.int32, w.shape, 1) < N_HEADS_B
        w = jnp.where(head_lane, w, 0.0)
        hi = w.astype(BF16)
        lo = (w - hi.astype(F32)).astype(BF16)
        return jnp.dot(jnp.concatenate([hi, lo], axis=1), expand_ref[...],
                       preferred_element_type=F32)

    def combine(st):
        rows = st["rows"]
        nums, stats = [], []
        si = 0
        for d, o_ref, s_ref in zip(dils, ob_refs, st_refs):
            if d == 1:
                nums.append(o_ref[rows, :].astype(F32))
                stats.append(s_ref[rows, :])
                continue
            o_scr, s_scr = scratch[si], scratch[si + 1]
            si += 2
            assert rows.start % d == 0 and rows.stop % d == 0
            src = slice(rows.start // d, rows.stop // d)
            for r in range(d):
                dst = pl.ds(rows.start + r, (rows.stop - rows.start) // d, stride=d)
                for c in range(n_slab):
                    lo = r * WIDTH_B + c * LANES
                    o_scr[c, dst, :] = o_ref[src, lo:lo + LANES].astype(F32)
                s_scr[dst, :] = s_ref[src, r * LANES:(r + 1) * LANES]
            nums.append(jnp.concatenate([o_scr[c, rows, :] for c in range(n_slab)], axis=1))
            stats.append(s_scr[rows, :])
        mx = functools.reduce(jnp.maximum, stats)
        scales = [jnp.exp2(s - mx) for s in stats]
        dens = [pltpu.roll(s, LANES - N_HEADS_B, 1) for s in stats]
        tot = functools.reduce(lambda a, b: a + b, [sc * dn for sc, dn in zip(scales, dens)])
        ob = None
        for sc, num in zip(scales, nums):
            term = expand(sc / tot) * num
            ob = term if ob is None else ob + term
        st["oa"] = _rms(oa_ref[rows, :].astype(F32), goa_ref[...]).astype(BF16)
        st["ob"] = _rms(ob, gob_ref[...]).astype(BF16)

    def mix(st):
        m = (jnp.dot(st.pop("oa"), wo_ref[:WIDTH_A, :], preferred_element_type=F32)
             + jnp.dot(st.pop("ob"), wo_ref[WIDTH_A:, :], preferred_element_type=F32))
        st["h"] = x_ref[st["rows"], :] + _rms(m, gpm_ref[...])
        st["v"] = _rms(st["h"], gpre_ref[...]).astype(BF16)
        st["ff"] = None

    def mlp_chunk(c):
        def stage(st):
            a = jnp.maximum(
                jnp.dot(st["v"], wup_ref[:, c:c + ff_chunk], preferred_element_type=F32), 0.0)
            t = jnp.dot((a * a).astype(BF16), wdn_ref[c:c + ff_chunk, :],
                        preferred_element_type=F32)
            st["ff"] = t if st["ff"] is None else st["ff"] + t
        return stage

    def ple(st):
        h = st["h"] + _rms(st.pop("ff"), gmlp_ref[...])
        gate = jax.nn.sigmoid(
            jnp.dot(h.astype(BF16), wpg_ref[...], preferred_element_type=F32) + bpg_ref[...])
        emb = jnp.dot(p_ref[st["rows"], :].astype(BF16), wpp_ref[...],
                      preferred_element_type=F32)
        out_ref[st["rows"], :] = h + _rms(gate * emb, gple_ref[...])

    stages = [combine, mix] + [mlp_chunk(c) for c in range(0, wup_ref.shape[1], ff_chunk)] + [ple]
    starts = np.cumsum((0,) + chain_rows)
    states = [{"rows": slice(int(lo), int(hi))} for lo, hi in zip(starts[:-1], starts[1:])]
    for stage in stages:
        for st in states:
            stage(st)


def _post(x2d, p2d, oa, obs, stats, dils, consts, tm, ff_chunk):
    n, d_model = x2d.shape
    row = lambda width: pl.BlockSpec((tm, width), lambda i: (i, 0))
    const = lambda a: pl.BlockSpec(a.shape, lambda i: (0,) * a.ndim,
                                   pipeline_mode=pl.Buffered(1))
    scratch = []
    for d in dils:
        if d > 1:
            scratch += [pltpu.VMEM((WIDTH_B // LANES, tm, LANES), F32),
                        pltpu.VMEM((tm, LANES), F32)]
    return pl.pallas_call(
        functools.partial(_post_kernel, dils=dils, ff_chunk=ff_chunk,
                          chain_rows=tuple(tm * f // sum(_ROW_CHAINS) for f in _ROW_CHAINS)),
        grid=(n // tm,),
        in_specs=[row(d_model), row(p2d.shape[1]), row(WIDTH_A)]
                 + [pl.BlockSpec((tm // d, d * WIDTH_B), lambda i: (i, 0)) for d in dils]
                 + [pl.BlockSpec((tm // d, d * LANES), lambda i: (i, 0)) for d in dils]
                 + [const(a) for a in consts],
        out_specs=row(d_model),
        out_shape=jax.ShapeDtypeStruct((n, d_model), x2d.dtype),
        scratch_shapes=scratch,
        compiler_params=pltpu.CompilerParams(
            dimension_semantics=("parallel",), vmem_limit_bytes=_VMEM_LIMIT),
        name="post",
    )(x2d, p2d, oa, *obs, *stats, *consts)


def kernel(x, p, rel_bias_table, g_pre_mix, w_in, sink_a, g_out_a, g_out_b, w_o, g_post_mix,
           g_pre_mlp, w_up, w_down, g_post_mlp, w_ple_proj, w_ple_gate, b_ple_gate, g_post_ple):
    b, s, d_model = x.shape
    depth = w_in.shape[0]
    n = b * s
    tm = 512
    dils = tuple(dil for _, dil in DILATED_PATTERNS)

    def pair_heads(a, axis):
        shape = a.shape
        a = a.reshape(shape[:axis] + (N_KV_A, GROUP_A, HEAD_DIM) + shape[axis + 1:])
        return jnp.swapaxes(a, axis, axis + 1).reshape(shape)

    idx_a, shifts_a = _band_layout(WINDOW_A, 3 * WINDOW_A, (0, WINDOW_A, 2 * WINDOW_A),
                                   1, WINDOW_A)
    half = DILATED_PATTERNS[0][0] // (2 * DILATED_PATTERNS[0][1])
    idx_b = []
    for window, dil in DILATED_PATTERNS:
        assert window // (2 * dil) == half
        idx, shifts_b = _band_layout(2 * half, 4 * half, (0, half, 2 * half), dil, half)
        idx_b.append(idx)
    bias_a, bias_b = _bias(
        rel_bias_table.T.astype(F32) * LOG2E,
        jnp.asarray(idx_a)[None, None], shifts_a,
        (len(shifts_a), N_HEADS_A, WINDOW_A, 3 * WINDOW_A),
        jnp.asarray(np.stack(idx_b))[:, None], shifts_b,
        (len(DILATED_PATTERNS) * len(shifts_b), N_HEADS_B, 2 * half, 4 * half))

    expand_np = np.zeros((2 * LANES, WIDTH_B), np.float32)
    for hh in range(N_HEADS_B):
        expand_np[hh, hh * HEAD_DIM:(hh + 1) * HEAD_DIM] = 1.0
        expand_np[LANES + hh, hh * HEAD_DIM:(hh + 1) * HEAD_DIM] = 1.0
    expand = jnp.asarray(expand_np, BF16)

    h2d = x.reshape(n, d_model)
    row = lambda a: a.reshape(1, -1).astype(F32)
    for i in range(depth):
        scale = HEAD_DIM ** -0.5 * LOG2E
        w = w_in[i]
        o0 = WIDTH_A + 2 * KV_WIDTH_A
        w_all = jnp.concatenate([
            pair_heads(w[:, :WIDTH_A], 1) * scale,
            w[:, WIDTH_A:o0],
            w[:, o0:o0 + WIDTH_B] * scale,
            w[:, o0 + WIDTH_B:],
        ], axis=1).astype(BF16)
        outs = _inproj(h2d, row(g_pre_mix[i]), w_all, tm, dils)
        qa, ka, va = (t.reshape(b, s, t.shape[-1]) for t in outs[:3])

        o_a = _windowed(qa, ka, va, bias_a, sink_a[i].astype(F32) * LOG2E)
        obs, stats = [], []
        for pi, (window, dil) in enumerate(DILATED_PATTERNS):
            qd, kd, vd = outs[3 + 3 * pi:6 + 3 * pi]
            o, st = _dilated(qd, kd, vd, bias_b, pi, b, dil, window // (2 * dil))
            obs.append(o)
            stats.append(st)

        wo = jnp.concatenate([pair_heads(w_o[i][:WIDTH_A], 0), w_o[i][WIDTH_A:]],
                             axis=0).astype(BF16)
        consts = (expand, row(pair_heads(g_out_a[i], 0)), row(g_out_b[i]), wo, row(g_post_mix[i]),
                  row(g_pre_mlp[i]), w_up[i].astype(BF16), w_down[i].astype(BF16),
                  row(g_post_mlp[i]), w_ple_proj[i].astype(BF16), w_ple_gate[i].astype(BF16),
                  row(b_ple_gate[i]), row(g_post_ple[i]))
        h2d = _post(h2d, p[i].reshape(n, -1), o_a.reshape(n, WIDTH_A), obs, stats, dils,
                    consts, tm, 2048)
    return h2d.reshape(b, s, d_model)
```

```python
import functools
import math

import jax
import jax.numpy as jnp
import numpy as np
from jax import lax
from jax.experimental import pallas as pl
from jax.experimental.pallas import tpu as pltpu

HEAD_DIM = 64
N_HEADS_A = 8
N_KV_A = 2
GROUP_A = N_HEADS_A // N_KV_A
WINDOW_A = 128
N_HEADS_B = 8
DILATED_PATTERNS = ((128, 1), (512, 4), (2048, 16))
WIDTH_A = N_HEADS_A * HEAD_DIM
WIDTH_B = N_HEADS_B * HEAD_DIM
KV_WIDTH_A = N_KV_A * HEAD_DIM
NUM_BUCKETS = 32
MAX_DISTANCE = 1024
EPS = 1e-6
NEG = -1e30
LOG2E = 1.4426950408889634

LANES = 128
SUBLANES = 8
F32 = jnp.float32
BF16 = jnp.bfloat16

_VMEM_LIMIT = 56 * 1024 * 1024
_TOKEN_TILE = 512
_FF_CHUNK = 2048
_BLOCKS_PER_TRIP = 16
_WIN_LOOKAHEAD = 2
_DIL_LOOKAHEAD = 4
_ROW_CHAINS = (1, 1)


def _rms(x, g):
    ms = jnp.mean(x * x, axis=-1, keepdims=True)
    return (x * lax.rsqrt(ms + EPS)) * g


def _t5_bucket_np(rel):
    half = NUM_BUCKETS // 2
    max_exact = half // 2
    sign = np.where(rel > 0, half, 0)
    n = np.abs(rel)
    nf = np.maximum(n, 1).astype(np.float32)
    large = max_exact + (np.log(nf / np.float32(max_exact))
                         / np.float32(math.log(MAX_DISTANCE / max_exact))
                         * np.float32(half - max_exact)).astype(np.int32)
    large = np.minimum(large, half - 1)
    return (sign + np.where(n < max_exact, n, large)).astype(np.int32)


def _band_layout(q_len, k_len, offs, dil, half_window):
    center = q_len - 1 + max(offs)
    length = -(-(center + k_len) // LANES) * LANES
    rel = np.arange(length) - center
    idx = np.where(np.abs(rel) <= half_window, _t5_bucket_np(rel * dil), -1).astype(np.int32)
    return idx, tuple((off - center) % length for off in offs)


def _bias_kernel(tab_ref, idxa_ref, idxb_ref, outa_ref, outb_ref, *, shifts_a, shifts_b):
    def build(idx_ref, out_ref, head0, shifts):
        n_pat, _, length = idx_ref.shape
        _, n_heads, q_len, k_len = out_ref.shape
        for p in range(n_pat):
            idx = idx_ref[p]
            vec = jnp.full((n_heads, length), NEG, F32)
            for b in range(NUM_BUCKETS):
                vec = jnp.where(idx == b, tab_ref[head0:head0 + n_heads, b:b + 1], vec)
            for h in range(n_heads):
                rows = jnp.broadcast_to(vec[h:h + 1, :], (q_len, length))
                for v, s in enumerate(shifts):
                    band = pltpu.roll(rows, s, 1, stride=1, stride_axis=0)
                    out_ref[p * len(shifts) + v, h] = band[:, :k_len]

    build(idxa_ref, outa_ref, 0, shifts_a)
    build(idxb_ref, outb_ref, N_HEADS_A, shifts_b)


def _bias(table_t, idx_a, shifts_a, shape_a, idx_b, shifts_b, shape_b):
    vmem = pl.BlockSpec(memory_space=pltpu.VMEM)
    return pl.pallas_call(
        functools.partial(_bias_kernel, shifts_a=shifts_a, shifts_b=shifts_b),
        in_specs=[vmem, vmem, vmem],
        out_specs=[vmem, vmem],
        out_shape=[jax.ShapeDtypeStruct(shape_a, F32), jax.ShapeDtypeStruct(shape_b, F32)],
        compiler_params=pltpu.CompilerParams(vmem_limit_bytes=_VMEM_LIMIT),
        name="bias",
    )(table_t, idx_a, idx_b)


def _inproj_kernel(x_ref, g_ref, w_ref, *refs, dils, stage_dils):
    n_pat = len(dils)
    outs = refs[:3 + 3 * n_pat]
    scr = refs[3 + 3 * n_pat]
    stage_scr = dict(zip(stage_dils, refs[4 + 3 * n_pat:]))
    tm = x_ref.shape[0]
    n_slab = WIDTH_B // LANES

    x = x_ref[...]
    u = (x * g_ref[...]).astype(BF16)
    rstd = lax.rsqrt(jnp.mean(x * x, axis=-1, keepdims=True) + EPS)

    col = sum(o_ref.shape[-1] for o_ref in outs[:3])
    for t in range(3):
        seg = rstd * jnp.dot(u, w_ref[:, col:col + WIDTH_B], preferred_element_type=F32)
        col += WIDTH_B
        if any(d > 1 for d in dils):
            for c in range(n_slab):
                scr[t, c] = seg[:, c * LANES:(c + 1) * LANES]
        for pi, d in enumerate(dils):
            o_ref = outs[3 + 3 * pi + t]
            if d == 1:
                o_ref[...] = seg.astype(BF16)
                continue
            base = max([b for b in stage_dils if b < d and d % b == 0], default=1)
            step = d // base
            for r in range(d):
                for c in range(n_slab):
                    rows = pl.ds(r // base, tm // d, stride=step)
                    src = scr.at[t, c] if base == 1 else stage_scr[base].at[t, r % base, c]
                    val = src[rows, :]
                    if d in stage_scr:
                        stage_scr[d][t, r, c] = val
                    lo = r * WIDTH_B + c * LANES
                    o_ref[:, lo:lo + LANES] = val.astype(BF16)
    col = 0
    for o_ref in outs[:3]:
        width = o_ref.shape[-1]
        o_ref[...] = (rstd * jnp.dot(u, w_ref[:, col:col + width],
                                     preferred_element_type=F32)).astype(BF16)
        col += width


def _inproj(x2d, g, w, tm, dils):
    n, d_model = x2d.shape
    shapes = [(n, WIDTH_A), (n, KV_WIDTH_A), (n, KV_WIDTH_A)]
    blocks = [(tm, WIDTH_A), (tm, KV_WIDTH_A), (tm, KV_WIDTH_A)]
    for d in dils:
        assert tm % (16 * d) == 0 and n % d == 0
        shapes += [(n // d, d * WIDTH_B)] * 3
        blocks += [(tm // d, d * WIDTH_B)] * 3
    stage_dils = tuple(b for b in dils if b > 1 and any(d > b and d % b == 0 for d in dils))
    n_slab = WIDTH_B // LANES
    return pl.pallas_call(
        functools.partial(_inproj_kernel, dils=dils, stage_dils=stage_dils),
        grid=(n // tm,),
        in_specs=[
            pl.BlockSpec((tm, d_model), lambda i: (i, 0)),
            pl.BlockSpec((1, d_model), lambda i: (0, 0)),
            pl.BlockSpec(w.shape, lambda i: (0, 0)),
        ],
        out_specs=[pl.BlockSpec(bs, lambda i: (i, 0)) for bs in blocks],
        out_shape=[jax.ShapeDtypeStruct(sh, BF16) for sh in shapes],
        scratch_shapes=[pltpu.VMEM((3, n_slab, tm, LANES), F32)]
                       + [pltpu.VMEM((3, b, n_slab, tm // b, LANES), F32) for b in stage_dils],
        compiler_params=pltpu.CompilerParams(
            dimension_semantics=("parallel",), vmem_limit_bytes=_VMEM_LIMIT),
        name="inproj",
    )(x2d, g, w)


def _nt_dot(a, b):
    return lax.dot_general(a, b, (((1,), (1,)), ((), ())), preferred_element_type=F32)


def _win_kernel(sink_ref, q_ref, k_ref, v_ref, bias_ref, o_ref, *, seq, blk, unroll):
    nblk = seq // blk
    lane = lax.broadcasted_iota(jnp.int32, (blk, LANES), 1)
    low = lane < HEAD_DIM

    klen = 3 * blk

    def block_context(n):
        q0 = pl.multiple_of(n * blk, blk)
        k0 = pl.multiple_of(jnp.clip(q0 - blk, 0, seq - klen), blk)
        variant = jnp.where(n == 0, 0, jnp.where(n == nblk - 1, 2, 1))
        k1 = k_ref[pl.ds(k0, 2 * blk), :]
        k2 = k_ref[pl.ds(k0 + 2 * blk, blk), :]
        v1 = v_ref[pl.ds(k0, 2 * blk), :]
        v2 = v_ref[pl.ds(k0 + 2 * blk, blk), :]
        zero = jnp.zeros_like(k2)
        one_low = low.astype(F32).astype(BF16)
        one_high = (~low).astype(F32).astype(BF16)
        return dict(
            q0=q0, variant=variant, k1=k1,
            v1_aug=jnp.concatenate([v1, jnp.ones_like(v1)], axis=1),
            k2_pack=jnp.concatenate([jnp.where(low, k2, zero), jnp.where(low, zero, k2)], axis=0),
            v2_pack=jnp.concatenate(
                [jnp.concatenate([jnp.where(low, v2, zero), one_low], axis=1),
                 jnp.concatenate([jnp.where(low, zero, v2), one_high], axis=1)], axis=0))

    def scores(ctx, j):
        q_tile = q_ref[pl.ds(ctx["q0"], blk), j * LANES:(j + 1) * LANES]
        s2 = _nt_dot(q_tile, ctx["k2_pack"])
        out = []
        for hf in range(N_KV_A):
            qm = jnp.where(low if hf == 0 else ~low, q_tile, jnp.zeros_like(q_tile))
            out.append(jnp.concatenate(
                [_nt_dot(qm, ctx["k1"]), s2[:, hf * LANES:(hf + 1) * LANES]], axis=1)
                + bias_ref[ctx["variant"], hf * GROUP_A + j])
        return out

    def softmax_pv(ctx, j, logits):
        nums, dens, e2s = [], [], []
        for hf, s in enumerate(logits):
            sink = sink_ref[hf * GROUP_A + j]
            m = jnp.maximum(jnp.max(s, axis=-1, keepdims=True), sink)
            e = jnp.exp2(s - m).astype(BF16)
            o1 = jnp.dot(e[:, :2 * blk], ctx["v1_aug"], preferred_element_type=F32)
            nums.append(o1[:, :LANES])
            dens.append(o1[:, LANES:] + jnp.exp2(sink - m))
            e2s.append(e[:, 2 * blk:])
        o2 = jnp.dot(jnp.concatenate(e2s, axis=1), ctx["v2_pack"], preferred_element_type=F32)
        num = jnp.where(low, nums[0], nums[1]) + o2[:, :LANES]
        den = jnp.where(low, dens[0], dens[1]) + o2[:, LANES:]
        o_ref[pl.ds(ctx["q0"], blk), j * LANES:(j + 1) * LANES] = (num / den).astype(o_ref.dtype)

    def trip(t, c):
        pending = []
        for i in range(unroll):
            ctx = block_context(t * unroll + i)
            for j in range(GROUP_A):
                pending.append((ctx, j, scores(ctx, j)))
                if len(pending) > _WIN_LOOKAHEAD:
                    softmax_pv(*pending.pop(0))
        for task in pending:
            softmax_pv(*task)
        return c

    lax.fori_loop(0, nblk // unroll, trip, 0)


def _windowed(qa, ka, va, bias, sink):
    b, s, _ = qa.shape
    blk = WINDOW_A
    nblk = s // blk
    assert s % blk == 0 and nblk >= 3
    kern = functools.partial(_win_kernel, seq=s, blk=blk,
                             unroll=math.gcd(nblk, _BLOCKS_PER_TRIP))
    return pl.pallas_call(
        kern,
        grid=(b,),
        in_specs=[
            pl.BlockSpec(memory_space=pltpu.SMEM),
            pl.BlockSpec((None, s, WIDTH_A), lambda i: (i, 0, 0)),
            pl.BlockSpec((None, s, KV_WIDTH_A), lambda i: (i, 0, 0)),
            pl.BlockSpec((None, s, KV_WIDTH_A), lambda i: (i, 0, 0)),
            pl.BlockSpec(bias.shape, lambda i: (0, 0, 0, 0)),
        ],
        out_specs=pl.BlockSpec((None, s, WIDTH_A), lambda i: (i, 0, 0)),
        out_shape=jax.ShapeDtypeStruct((b, s, WIDTH_A), BF16),
        compiler_params=pltpu.CompilerParams(
            dimension_semantics=("parallel",), vmem_limit_bytes=_VMEM_LIMIT),
        name="win_gqa",
    )(sink, qa, ka, va, bias)


def _dil_kernel(q_ref, k_ref, v_ref, bias_ref, o_ref, st_ref, *, ls, half, n_res, unroll):
    qb = 2 * half
    kw = 4 * half
    nblk = ls // qb
    lane = lax.broadcasted_iota(jnp.int32, (qb, LANES), 1)
    low = lane < HEAD_DIM

    n_pairs = N_HEADS_B // 2

    def scores(n, res_i, j):
        q0 = pl.multiple_of(n * qb, qb)
        k0 = pl.multiple_of(jnp.clip(q0 - half, 0, ls - kw), half)
        variant = jnp.where(n == 0, 0, jnp.where(n == nblk - 1, 2, 1))
        cols = slice(res_i * WIDTH_B + j * LANES, res_i * WIDTH_B + (j + 1) * LANES)
        k_win = k_ref[pl.ds(k0, kw), cols]
        q_tile = q_ref[pl.ds(q0, qb), cols]
        logits = []
        for hf in range(2):
            qm = jnp.where(low if hf == 0 else ~low, q_tile, jnp.zeros_like(q_tile))
            logits.append(_nt_dot(qm, k_win) + bias_ref[variant, 2 * j + hf])
        return dict(q0=q0, k0=k0, cols=cols, res_i=res_i, j=j, logits=logits)

    def softmax_pv(task, stats):
        j, cols = task["j"], task["cols"]
        v_win = v_ref[pl.ds(task["k0"], kw), cols]
        v_aug = jnp.concatenate([v_win, jnp.ones_like(v_win)], axis=1)
        st_acc = jnp.zeros((qb, LANES), F32) if j == 0 else stats.pop()
        nums = []
        for hf, s in enumerate(task["logits"]):
            h = 2 * j + hf
            m = jnp.max(s, axis=-1, keepdims=True)
            o = jnp.dot(jnp.exp2(s - m).astype(BF16), v_aug, preferred_element_type=F32)
            nums.append(o[:, :LANES])
            st_acc = jnp.where(lane == h, m,
                               jnp.where(lane == N_HEADS_B + h, o[:, LANES:], st_acc))
        o_ref[pl.ds(task["q0"], qb), cols] = jnp.where(low, nums[0], nums[1]).astype(o_ref.dtype)
        if j == n_pairs - 1:
            r = task["res_i"]
            st_ref[pl.ds(task["q0"], qb), r * LANES:(r + 1) * LANES] = st_acc
        else:
            stats.append(st_acc)

    def trip(t, c):
        pending, stats = [], []
        for i in range(unroll):
            for res_i in range(n_res):
                for j in range(n_pairs):
                    pending.append(scores(t * unroll + i, res_i, j))
                    if len(pending) > _DIL_LOOKAHEAD:
                        softmax_pv(pending.pop(0), stats)
        for task in pending:
            softmax_pv(task, stats)
        return c

    lax.fori_loop(0, nblk // unroll, trip, 0)


def _dilated(qd, kd, vd, bias, pattern, batch, dil, half):
    rows, w_all = qd.shape
    assert w_all == dil * WIDTH_B
    ls = rows // batch
    nblk = ls // (2 * half)
    assert ls % (2 * half) == 0 and nblk >= 2
    n_res = math.gcd(dil, max(1, _BLOCKS_PER_TRIP // nblk))
    unroll = math.gcd(nblk, max(1, _BLOCKS_PER_TRIP // n_res))
    view = lambda t: t.reshape(batch, ls, w_all)
    kern = functools.partial(_dil_kernel, ls=ls, half=half, n_res=n_res, unroll=unroll)
    spec = pl.BlockSpec((None, ls, n_res * WIDTH_B), lambda i, r: (i, 0, r))
    n_var = 3
    o, st = pl.pallas_call(
        kern,
        grid=(batch, dil // n_res),
        in_specs=[spec, spec, spec,
                  pl.BlockSpec((n_var,) + bias.shape[1:], lambda i, r: (pattern, 0, 0, 0))],
        out_specs=[spec, pl.BlockSpec((None, ls, n_res * LANES), lambda i, r: (i, 0, r))],
        out_shape=[jax.ShapeDtypeStruct((batch, ls, w_all), BF16),
                   jax.ShapeDtypeStruct((batch, ls, dil * LANES), F32)],
        compiler_params=pltpu.CompilerParams(
            dimension_semantics=("parallel", "parallel"), vmem_limit_bytes=_VMEM_LIMIT),
        name=f"dilated_d{dil}",
    )(view(qd), view(kd), view(vd), bias)
    return o.reshape(rows, w_all), st.reshape(rows, dil * LANES)


def _post_kernel(x_ref, p_ref, oa_ref, *refs, dils, ff_chunk, chain_rows):
    n_pat = len(dils)
    ob_refs = refs[:n_pat]
    st_refs = refs[n_pat:2 * n_pat]
    (expand_ref, goa_ref, gob_ref, wo_ref, gpm_ref, gpre_ref, wup_ref, wdn_ref,
     gmlp_ref, wpp_ref, wpg_ref, bpg_ref, gple_ref, out_ref) = refs[2 * n_pat:2 * n_pat + 14]
    scratch = refs[2 * n_pat + 14:]
    tm = x_ref.shape[0]
    assert sum(chain_rows) == tm
    n_slab = WIDTH_B // LANES

    def expand(w):
        head_lane = lax.broadcasted_iota(jnp.int32, w.shape, 1) < N_HEADS_B
        w = jnp.where(head_lane, w, 0.0)
        hi = w.astype(BF16)
        lo = (w - hi.astype(F32)).astype(BF16)
        return jnp.dot(jnp.concatenate([hi, lo], axis=1), expand_ref[...],
                       preferred_element_type=F32)

    def combine(st):
        rows = st["rows"]
        nums, stats = [], []
        si = 0
        for d, o_ref, s_ref in zip(dils, ob_refs, st_refs):
            if d == 1:
                nums.append(o_ref[rows, :].astype(F32))
                stats.append(s_ref[rows, :])
                continue
            o_scr, s_scr = scratch[si], scratch[si + 1]
            si += 2
            assert rows.start % d == 0 and rows.stop % d == 0
            src = slice(rows.start // d, rows.stop // d)
            for r in range(d):
                dst = pl.ds(rows.start + r, (rows.stop - rows.start) // d, stride=d)
                for c in range(n_slab):
                    lo = r * WIDTH_B + c * LANES
                    o_scr[c, dst, :] = o_ref[src, lo:lo + LANES].astype(F32)
                s_scr[dst, :] = s_ref[src, r * LANES:(r + 1) * LANES]
            nums.append(jnp.concatenate([o_scr[c, rows, :] for c in range(n_slab)], axis=1))
            stats.append(s_scr[rows, :])
        mx = functools.reduce(jnp.maximum, stats)
        scales = [jnp.exp2(s - mx) for s in stats]
        dens = [pltpu.roll(s, LANES - N_HEADS_B, 1) for s in stats]
        tot = functools.reduce(lambda a, b: a + b, [sc * dn for sc, dn in zip(scales, dens)])
        ob = None
        for sc, num in zip(scales, nums):
            term = expand(sc / tot) * num
            ob = term if ob is None else ob + term
        st["oa"] = _rms(oa_ref[rows, :].astype(F32), goa_ref[...]).astype(BF16)
        st["ob"] = _rms(ob, gob_ref[...]).astype(BF16)

    def mix(st):
        m = (jnp.dot(st.pop("oa"), wo_ref[:WIDTH_A, :], preferred_element_type=F32)
             + jnp.dot(st.pop("ob"), wo_ref[WIDTH_A:, :], preferred_element_type=F32))
        st["h"] = x_ref[st["rows"], :] + _rms(m, gpm_ref[...])
        st["v"] = _rms(st["h"], gpre_ref[...]).astype(BF16)
        st["ff"] = None

    def mlp_chunk(c):
        def stage(st):
            a = jnp.maximum(
                jnp.dot(st["v"], wup_ref[:, c:c + ff_chunk], preferred_element_type=F32), 0.0)
            t = jnp.dot((a * a).astype(BF16), wdn_ref[c:c + ff_chunk, :],
                        preferred_element_type=F32)
            st["ff"] = t if st["ff"] is None else st["ff"] + t
        return stage

    def ple(st):
        h = st["h"] + _rms(st.pop("ff"), gmlp_ref[...])
        gate = jax.nn.sigmoid(
            jnp.dot(h.astype(BF16), wpg_ref[...], preferred_element_type=F32) + bpg_ref[...])
        emb = jnp.dot(p_ref[st["rows"], :].astype(BF16), wpp_ref[...],
                      preferred_element_type=F32)
        out_ref[st["rows"], :] = h + _rms(gate * emb, gple_ref[...])

    stages = [combine, mix] + [mlp_chunk(c) for c in range(0, wup_ref.shape[1], ff_chunk)] + [ple]
    starts = np.cumsum((0,) + chain_rows)
    states = [{"rows": slice(int(lo), int(hi))} for lo, hi in zip(starts[:-1], starts[1:])]
    for stage in stages:
        for st in states:
            stage(st)


def _post(x2d, p2d, oa, obs, stats, dils, consts, tm, ff_chunk):
    n, d_model = x2d.shape
    row = lambda width: pl.BlockSpec((tm, width), lambda i: (i, 0))
    const = lambda a: pl.BlockSpec(a.shape, lambda i: (0,) * a.ndim,
                                   pipeline_mode=pl.Buffered(1))
    scratch = []
    for d in dils:
        if d > 1:
            scratch += [pltpu.VMEM((WIDTH_B // LANES, tm, LANES), F32),
                        pltpu.VMEM((tm, LANES), F32)]
    return pl.pallas_call(
        functools.partial(_post_kernel, dils=dils, ff_chunk=ff_chunk,
                          chain_rows=tuple(tm * f // sum(_ROW_CHAINS) for f in _ROW_CHAINS)),
        grid=(n // tm,),
        in_specs=[row(d_model), row(p2d.shape[1]), row(WIDTH_A)]
                 + [pl.BlockSpec((tm // d, d * WIDTH_B), lambda i: (i, 0)) for d in dils]
                 + [pl.BlockSpec((tm // d, d * LANES), lambda i: (i, 0)) for d in dils]
                 + [const(a) for a in consts],
        out_specs=row(d_model),
        out_shape=jax.ShapeDtypeStruct((n, d_model), x2d.dtype),
        scratch_shapes=scratch,
        compiler_params=pltpu.CompilerParams(
            dimension_semantics=("parallel",), vmem_limit_bytes=_VMEM_LIMIT),
        name="post",
    )(x2d, p2d, oa, *obs, *stats, *consts)


def kernel(x, p, rel_bias_table, g_pre_mix, w_in, sink_a, g_out_a, g_out_b, w_o, g_post_mix,
           g_pre_mlp, w_up, w_down, g_post_mlp, w_ple_proj, w_ple_gate, b_ple_gate, g_post_ple):
    b, s, d_model = x.shape
    depth = w_in.shape[0]
    n = b * s
    tm = _TOKEN_TILE
    assert n % tm == 0 and s % tm == 0
    dils = tuple(dil for _, dil in DILATED_PATTERNS)

    def pair_heads(a, axis):
        shape = a.shape
        a = a.reshape(shape[:axis] + (N_KV_A, GROUP_A, HEAD_DIM) + shape[axis + 1:])
        return jnp.swapaxes(a, axis, axis + 1).reshape(shape)

    idx_a, shifts_a = _band_layout(WINDOW_A, 3 * WINDOW_A, (0, WINDOW_A, 2 * WINDOW_A),
                                   1, WINDOW_A)
    half = DILATED_PATTERNS[0][0] // (2 * DILATED_PATTERNS[0][1])
    idx_b = []
    for window, dil in DILATED_PATTERNS:
        assert window // (2 * dil) == half
        idx, shifts_b = _band_layout(2 * half, 4 * half, (0, half, 2 * half), dil, half)
        idx_b.append(idx)
    bias_a, bias_b = _bias(
        rel_bias_table.T.astype(F32) * LOG2E,
        jnp.asarray(idx_a)[None, None], shifts_a,
        (len(shifts_a), N_HEADS_A, WINDOW_A, 3 * WINDOW_A),
        jnp.asarray(np.stack(idx_b))[:, None], shifts_b,
        (len(DILATED_PATTERNS) * len(shifts_b), N_HEADS_B, 2 * half, 4 * half))

    expand_np = np.zeros((2 * LANES, WIDTH_B), np.float32)
    for hh in range(N_HEADS_B):
        expand_np[hh, hh * HEAD_DIM:(hh + 1) * HEAD_DIM] = 1.0
        expand_np[LANES + hh, hh * HEAD_DIM:(hh + 1) * HEAD_DIM] = 1.0
    expand = jnp.asarray(expand_np, BF16)

    h2d = x.reshape(n, d_model)
    row = lambda a: a.reshape(1, -1).astype(F32)
    for i in range(depth):
        scale = HEAD_DIM ** -0.5 * LOG2E
        w = w_in[i]
        o0 = WIDTH_A + 2 * KV_WIDTH_A
        w_all = jnp.concatenate([
            pair_heads(w[:, :WIDTH_A], 1) * scale,
            w[:, WIDTH_A:o0],
            w[:, o0:o0 + WIDTH_B] * scale,
            w[:, o0 + WIDTH_B:],
        ], axis=1).astype(BF16)
        outs = _inproj(h2d, row(g_pre_mix[i]), w_all, tm, dils)
        qa, ka, va = (t.reshape(b, s, t.shape[-1]) for t in outs[:3])

        o_a = _windowed(qa, ka, va, bias_a, sink_a[i].astype(F32) * LOG2E)
        obs, stats = [], []
        for pi, (window, dil) in enumerate(DILATED_PATTERNS):
            qd, kd, vd = outs[3 + 3 * pi:6 + 3 * pi]
            o, st = _dilated(qd, kd, vd, bias_b, pi, b, dil, window // (2 * dil))
            obs.append(o)
            stats.append(st)

        wo = jnp.concatenate([pair_heads(w_o[i][:WIDTH_A], 0), w_o[i][WIDTH_A:]],
                             axis=0).astype(BF16)
        consts = (expand, row(pair_heads(g_out_a[i], 0)), row(g_out_b[i]), wo, row(g_post_mix[i]),
                  row(g_pre_mlp[i]), w_up[i].astype(BF16), w_down[i].astype(BF16),
                  row(g_post_mlp[i]), w_ple_proj[i].astype(BF16), w_ple_gate[i].astype(BF16),
                  row(b_ple_gate[i]), row(g_post_ple[i]))
        h2d = _post(h2d, p[i].reshape(n, -1), o_a.reshape(n, WIDTH_A), obs, stats, dils,
                    consts, tm, _FF_CHUNK)
    return h2d.reshape(b, s, d_model)
```

```python
import functools
import math

import jax
import jax.numpy as jnp
import numpy as np
from jax import lax
from jax.experimental import pallas as pl
from jax.experimental.pallas import tpu as pltpu

HEAD_DIM = 64
N_HEADS_A = 8
N_KV_A = 2
GROUP_A = N_HEADS_A // N_KV_A
WINDOW_A = 128
N_HEADS_B = 8
DILATED_PATTERNS = ((128, 1), (512, 4), (2048, 16))
WIDTH_A = N_HEADS_A * HEAD_DIM
WIDTH_B = N_HEADS_B * HEAD_DIM
KV_WIDTH_A = N_KV_A * HEAD_DIM
NUM_BUCKETS = 32
MAX_DISTANCE = 1024
EPS = 1e-6
NEG = -1e30
LOG2E = 1.4426950408889634

LANES = 128
SUBLANES = 8
F32 = jnp.float32
BF16 = jnp.bfloat16

_VMEM_LIMIT = 56 * 1024 * 1024
_TOKEN_TILE = 512
_FF_CHUNK = 2048
_BLOCKS_PER_TRIP = 16
_WIN_LOOKAHEAD = 2
_DIL_LOOKAHEAD = 4
_ROW_CHAINS = (1, 1)


def _rms(x, g):
    ms = jnp.mean(x * x, axis=-1, keepdims=True)
    return (x * lax.rsqrt(ms + EPS)) * g


def _t5_bucket_np(rel):
    half = NUM_BUCKETS // 2
    max_exact = half // 2
    sign = np.where(rel > 0, half, 0)
    n = np.abs(rel)
    nf = np.maximum(n, 1).astype(np.float32)
    large = max_exact + (np.log(nf / np.float32(max_exact))
                         / np.float32(math.log(MAX_DISTANCE / max_exact))
                         * np.float32(half - max_exact)).astype(np.int32)
    large = np.minimum(large, half - 1)
    return (sign + np.where(n < max_exact, n, large)).astype(np.int32)


def _band_layout(q_len, k_len, offs, dil, half_window):
    center = q_len - 1 + max(offs)
    length = -(-(center + k_len) // LANES) * LANES
    rel = np.arange(length) - center
    idx = np.where(np.abs(rel) <= half_window, _t5_bucket_np(rel * dil), -1).astype(np.int32)
    return idx, tuple((off - center) % length for off in offs)


def _bias_kernel(tab_ref, idxa_ref, idxb_ref, outa_ref, outb_ref, *, shifts_a, shifts_b):
    def build(idx_ref, out_ref, head0, shifts):
        n_pat, _, length = idx_ref.shape
        _, n_heads, q_len, k_len = out_ref.shape
        for p in range(n_pat):
            idx = idx_ref[p]
            vec = jnp.full((n_heads, length), NEG, F32)
            for b in range(NUM_BUCKETS):
                vec = jnp.where(idx == b, tab_ref[head0:head0 + n_heads, b:b + 1], vec)
            for h in range(n_heads):
                rows = jnp.broadcast_to(vec[h:h + 1, :], (q_len, length))
                for v, s in enumerate(shifts):
                    band = pltpu.roll(rows, s, 1, stride=1, stride_axis=0)
                    out_ref[p * len(shifts) + v, h] = band[:, :k_len]

    build(idxa_ref, outa_ref, 0, shifts_a)
    build(idxb_ref, outb_ref, N_HEADS_A, shifts_b)


def _bias(table_t, idx_a, shifts_a, shape_a, idx_b, shifts_b, shape_b):
    vmem = pl.BlockSpec(memory_space=pltpu.VMEM)
    return pl.pallas_call(
        functools.partial(_bias_kernel, shifts_a=shifts_a, shifts_b=shifts_b),
        in_specs=[vmem, vmem, vmem],
        out_specs=[vmem, vmem],
        out_shape=[jax.ShapeDtypeStruct(shape_a, F32), jax.ShapeDtypeStruct(shape_b, F32)],
        compiler_params=pltpu.CompilerParams(vmem_limit_bytes=_VMEM_LIMIT),
        name="bias",
    )(table_t, idx_a, idx_b)


def _inproj_kernel(x_ref, g_ref, w_ref, *refs, dils, stage_dils):
    n_pat = len(dils)
    win_ref, d_refs = refs[0], refs[1:1 + n_pat]
    scr = refs[1 + n_pat]
    stage_scr = dict(zip(stage_dils, refs[2 + n_pat:]))
    tm = x_ref.shape[0]
    n_slab = WIDTH_B // LANES

    x = x_ref[...]
    u = (x * g_ref[...]).astype(BF16)
    rstd = lax.rsqrt(jnp.mean(x * x, axis=-1, keepdims=True) + EPS)

    col = win_ref.shape[-1]
    for t in range(3):
        seg = rstd * jnp.dot(u, w_ref[:, col:col + WIDTH_B], preferred_element_type=F32)
        col += WIDTH_B
        if any(d > 1 for d in dils):
            for c in range(n_slab):
                scr[t, c] = seg[:, c * LANES:(c + 1) * LANES]
        for d, o_ref in zip(dils, d_refs):
            if d == 1:
                o_ref[:, t * WIDTH_B:(t + 1) * WIDTH_B] = seg.astype(BF16)
                continue
            base = max([b for b in stage_dils if b < d and d % b == 0], default=1)
            step = d // base
            for r in range(d):
                for c in range(n_slab):
                    rows = pl.ds(r // base, tm // d, stride=step)
                    src = scr.at[t, c] if base == 1 else stage_scr[base].at[t, r % base, c]
                    val = src[rows, :]
                    if d in stage_scr:
                        stage_scr[d][t, r, c] = val
                    lo = (3 * r + t) * WIDTH_B + c * LANES
                    o_ref[:, lo:lo + LANES] = val.astype(BF16)
    win_ref[...] = (rstd * jnp.dot(u, w_ref[:, :win_ref.shape[-1]],
                                   preferred_element_type=F32)).astype(BF16)


def _inproj(x2d, g, w, tm, dils):
    n, d_model = x2d.shape
    shapes = [(n, WIDTH_A + 2 * KV_WIDTH_A)]
    blocks = [(tm, WIDTH_A + 2 * KV_WIDTH_A)]
    for d in dils:
        assert tm % (16 * d) == 0 and n % d == 0
        shapes.append((n // d, d * 3 * WIDTH_B))
        blocks.append((tm // d, d * 3 * WIDTH_B))
    stage_dils = tuple(b for b in dils if b > 1 and any(d > b and d % b == 0 for d in dils))
    n_slab = WIDTH_B // LANES
    return pl.pallas_call(
        functools.partial(_inproj_kernel, dils=dils, stage_dils=stage_dils),
        grid=(n // tm,),
        in_specs=[
            pl.BlockSpec((tm, d_model), lambda i: (i, 0)),
            pl.BlockSpec((1, d_model), lambda i: (0, 0)),
            pl.BlockSpec(w.shape, lambda i: (0, 0)),
        ],
        out_specs=[pl.BlockSpec(bs, lambda i: (i, 0)) for bs in blocks],
        out_shape=[jax.ShapeDtypeStruct(sh, BF16) for sh in shapes],
        scratch_shapes=[pltpu.VMEM((3, n_slab, tm, LANES), F32)]
                       + [pltpu.VMEM((3, b, n_slab, tm // b, LANES), F32) for b in stage_dils],
        compiler_params=pltpu.CompilerParams(
            dimension_semantics=("parallel",), vmem_limit_bytes=_VMEM_LIMIT),
        name="inproj",
    )(x2d, g, w)


def _nt_dot(a, b):
    return lax.dot_general(a, b, (((1,), (1,)), ((), ())), preferred_element_type=F32)


def _win_kernel(sink_ref, qkv_ref, bias_ref, o_ref, *, seq, blk, unroll):
    q_ref = qkv_ref.at[:, :WIDTH_A]
    k_ref = qkv_ref.at[:, WIDTH_A:WIDTH_A + KV_WIDTH_A]
    v_ref = qkv_ref.at[:, WIDTH_A + KV_WIDTH_A:]
    nblk = seq // blk
    lane = lax.broadcasted_iota(jnp.int32, (blk, LANES), 1)
    low = lane < HEAD_DIM

    klen = 3 * blk

    def block_context(n):
        q0 = pl.multiple_of(n * blk, blk)
        k0 = pl.multiple_of(jnp.clip(q0 - blk, 0, seq - klen), blk)
        variant = jnp.where(n == 0, 0, jnp.where(n == nblk - 1, 2, 1))
        k1 = k_ref[pl.ds(k0, 2 * blk), :]
        k2 = k_ref[pl.ds(k0 + 2 * blk, blk), :]
        v1 = v_ref[pl.ds(k0, 2 * blk), :]
        v2 = v_ref[pl.ds(k0 + 2 * blk, blk), :]
        zero = jnp.zeros_like(k2)
        one_low = low.astype(F32).astype(BF16)
        one_high = (~low).astype(F32).astype(BF16)
        return dict(
            q0=q0, variant=variant, k1=k1,
            v1_aug=jnp.concatenate([v1, jnp.ones_like(v1)], axis=1),
            k2_pack=jnp.concatenate([jnp.where(low, k2, zero), jnp.where(low, zero, k2)], axis=0),
            v2_pack=jnp.concatenate(
                [jnp.concatenate([jnp.where(low, v2, zero), one_low], axis=1),
                 jnp.concatenate([jnp.where(low, zero, v2), one_high], axis=1)], axis=0))

    def scores(ctx, j):
        q_tile = q_ref[pl.ds(ctx["q0"], blk), j * LANES:(j + 1) * LANES]
        s2 = _nt_dot(q_tile, ctx["k2_pack"])
        out = []
        for hf in range(N_KV_A):
            qm = jnp.where(low if hf == 0 else ~low, q_tile, jnp.zeros_like(q_tile))
            out.append(jnp.concatenate(
                [_nt_dot(qm, ctx["k1"]), s2[:, hf * LANES:(hf + 1) * LANES]], axis=1)
                + bias_ref[ctx["variant"], hf * GROUP_A + j])
        return out

    def softmax_pv(ctx, j, logits):
        nums, dens, e2s = [], [], []
        for hf, s in enumerate(logits):
            sink = sink_ref[hf * GROUP_A + j]
            m = jnp.maximum(jnp.max(s, axis=-1, keepdims=True), sink)
            e = jnp.exp2(s - m).astype(BF16)
            o1 = jnp.dot(e[:, :2 * blk], ctx["v1_aug"], preferred_element_type=F32)
            nums.append(o1[:, :LANES])
            dens.append(o1[:, LANES:] + jnp.exp2(sink - m))
            e2s.append(e[:, 2 * blk:])
        o2 = jnp.dot(jnp.concatenate(e2s, axis=1), ctx["v2_pack"], preferred_element_type=F32)
        num = jnp.where(low, nums[0], nums[1]) + o2[:, :LANES]
        den = jnp.where(low, dens[0], dens[1]) + o2[:, LANES:]
        o_ref[pl.ds(ctx["q0"], blk), j * LANES:(j + 1) * LANES] = (num / den).astype(o_ref.dtype)

    def trip(t, c):
        pending = []
        for i in range(unroll):
            ctx = block_context(t * unroll + i)
            for j in range(GROUP_A):
                pending.append((ctx, j, scores(ctx, j)))
                if len(pending) > _WIN_LOOKAHEAD:
                    softmax_pv(*pending.pop(0))
        for task in pending:
            softmax_pv(*task)
        return c

    lax.fori_loop(0, nblk // unroll, trip, 0)


def _windowed(qkv, bias, sink):
    b, s, width = qkv.shape
    blk = WINDOW_A
    nblk = s // blk
    assert s % blk == 0 and nblk >= 3
    kern = functools.partial(_win_kernel, seq=s, blk=blk,
                             unroll=math.gcd(nblk, _BLOCKS_PER_TRIP))
    return pl.pallas_call(
        kern,
        grid=(b,),
        in_specs=[
            pl.BlockSpec(memory_space=pltpu.SMEM),
            pl.BlockSpec((None, s, width), lambda i: (i, 0, 0)),
            pl.BlockSpec(bias.shape, lambda i: (0, 0, 0, 0)),
        ],
        out_specs=pl.BlockSpec((None, s, WIDTH_A), lambda i: (i, 0, 0)),
        out_shape=jax.ShapeDtypeStruct((b, s, WIDTH_A), BF16),
        compiler_params=pltpu.CompilerParams(
            dimension_semantics=("parallel",), vmem_limit_bytes=_VMEM_LIMIT),
        name="win_gqa",
    )(sink, qkv, bias)


def _dil_kernel(qkv_ref, bias_ref, o_ref, st_ref, *, ls, half, n_res, unroll):
    qb = 2 * half
    kw = 4 * half
    nblk = ls // qb
    lane = lax.broadcasted_iota(jnp.int32, (qb, LANES), 1)
    low = lane < HEAD_DIM

    n_pairs = N_HEADS_B // 2

    def scores(n, res_i, j):
        q0 = pl.multiple_of(n * qb, qb)
        k0 = pl.multiple_of(jnp.clip(q0 - half, 0, ls - kw), half)
        variant = jnp.where(n == 0, 0, jnp.where(n == nblk - 1, 2, 1))
        cols = slice(res_i * WIDTH_B + j * LANES, res_i * WIDTH_B + (j + 1) * LANES)
        q_col = 3 * res_i * WIDTH_B + j * LANES
        q_tile = qkv_ref[pl.ds(q0, qb), q_col:q_col + LANES]
        k_win = qkv_ref[pl.ds(k0, kw), q_col + WIDTH_B:q_col + WIDTH_B + LANES]
        logits = []
        for hf in range(2):
            qm = jnp.where(low if hf == 0 else ~low, q_tile, jnp.zeros_like(q_tile))
            logits.append(_nt_dot(qm, k_win) + bias_ref[variant, 2 * j + hf])
        return dict(q0=q0, k0=k0, cols=cols, v_col=q_col + 2 * WIDTH_B, res_i=res_i, j=j,
                    logits=logits)

    def softmax_pv(task, stats):
        j, cols = task["j"], task["cols"]
        v_win = qkv_ref[pl.ds(task["k0"], kw), task["v_col"]:task["v_col"] + LANES]
        v_aug = jnp.concatenate([v_win, jnp.ones_like(v_win)], axis=1)
        st_acc = jnp.zeros((qb, LANES), F32) if j == 0 else stats.pop()
        nums = []
        for hf, s in enumerate(task["logits"]):
            h = 2 * j + hf
            m = jnp.max(s, axis=-1, keepdims=True)
            o = jnp.dot(jnp.exp2(s - m).astype(BF16), v_aug, preferred_element_type=F32)
            nums.append(o[:, :LANES])
            st_acc = jnp.where(lane == h, m,
                               jnp.where(lane == N_HEADS_B + h, o[:, LANES:], st_acc))
        o_ref[pl.ds(task["q0"], qb), cols] = jnp.where(low, nums[0], nums[1]).astype(o_ref.dtype)
        if j == n_pairs - 1:
            r = task["res_i"]
            st_ref[pl.ds(task["q0"], qb), r * LANES:(r + 1) * LANES] = st_acc
        else:
            stats.append(st_acc)

    def trip(t, c):
        pending, stats = [], []
        for i in range(unroll):
            for res_i in range(n_res):
                for j in range(n_pairs):
                    pending.append(scores(t * unroll + i, res_i, j))
                    if len(pending) > _DIL_LOOKAHEAD:
                        softmax_pv(pending.pop(0), stats)
        for task in pending:
            softmax_pv(task, stats)
        return c

    lax.fori_loop(0, nblk // unroll, trip, 0)


def _dilated(qkv, bias, pattern, batch, dil, half):
    rows = qkv.shape[0]
    assert qkv.shape[1] == dil * 3 * WIDTH_B
    w_all = dil * WIDTH_B
    ls = rows // batch
    nblk = ls // (2 * half)
    assert ls % (2 * half) == 0 and nblk >= 2
    n_res = math.gcd(dil, max(1, _BLOCKS_PER_TRIP // nblk))
    unroll = math.gcd(nblk, max(1, _BLOCKS_PER_TRIP // n_res))
    kern = functools.partial(_dil_kernel, ls=ls, half=half, n_res=n_res, unroll=unroll)
    lanes = lambda width: pl.BlockSpec((None, ls, n_res * width), lambda i, r: (i, 0, r))
    n_var = 3
    o, st = pl.pallas_call(
        kern,
        grid=(batch, dil // n_res),
        in_specs=[lanes(3 * WIDTH_B),
                  pl.BlockSpec((n_var,) + bias.shape[1:], lambda i, r: (pattern, 0, 0, 0))],
        out_specs=[lanes(WIDTH_B), lanes(LANES)],
        out_shape=[jax.ShapeDtypeStruct((batch, ls, w_all), BF16),
                   jax.ShapeDtypeStruct((batch, ls, dil * LANES), F32)],
        compiler_params=pltpu.CompilerParams(
            dimension_semantics=("parallel", "parallel"), vmem_limit_bytes=_VMEM_LIMIT),
        name=f"dilated_d{dil}",
    )(qkv.reshape(batch, ls, dil * 3 * WIDTH_B), bias)
    return o.reshape(rows, w_all), st.reshape(rows, dil * LANES)


def _post_kernel(x_ref, p_ref, oa_ref, *refs, dils, ff_chunk, chain_rows):
    n_pat = len(dils)
    ob_refs = refs[:n_pat]
    st_refs = refs[n_pat:2 * n_pat]
    (expand_ref, goa_ref, gob_ref, wo_ref, gpm_ref, gpre_ref, wup_ref, wdn_ref,
     gmlp_ref, wpp_ref, wpg_ref, bpg_ref, gple_ref, out_ref) = refs[2 * n_pat:2 * n_pat + 14]
    scratch = refs[2 * n_pat + 14:]
    tm = x_ref.shape[0]
    assert sum(chain_rows) == tm
    n_slab = WIDTH_B // LANES

    def expand(w):
        head_lane = lax.broadcasted_iota(jnp.int32, w.shape, 1) < N_HEADS_B
        w = jnp.where(head_lane, w, 0.0)
        hi = w.astype(BF16)
        lo = (w - hi.astype(F32)).astype(BF16)
        return jnp.dot(jnp.concatenate([hi, lo], axis=1), expand_ref[...],
                       preferred_element_type=F32)

    def combine(st):
        rows = st["rows"]
        nums, stats = [], []
        si = 0
        for d, o_ref, s_ref in zip(dils, ob_refs, st_refs):
            if d == 1:
                nums.append(o_ref[rows, :].astype(F32))
                stats.append(s_ref[rows, :])
                continue
            o_scr, s_scr = scratch[si], scratch[si + 1]
            si += 2
            assert rows.start % d == 0 and rows.stop % d == 0
            src = slice(rows.start // d, rows.stop // d)
            for r in range(d):
                dst = pl.ds(rows.start + r, (rows.stop - rows.start) // d, stride=d)
                for c in range(n_slab):
                    lo = r * WIDTH_B + c * LANES
                    o_scr[c, dst, :] = o_ref[src, lo:lo + LANES].astype(F32)
                s_scr[dst, :] = s_ref[src, r * LANES:(r + 1) * LANES]
            nums.append(jnp.concatenate([o_scr[c, rows, :] for c in range(n_slab)], axis=1))
            stats.append(s_scr[rows, :])
        mx = functools.reduce(jnp.maximum, stats)
        scales = [jnp.exp2(s - mx) for s in stats]
        dens = [pltpu.roll(s, LANES - N_HEADS_B, 1) for s in stats]
        tot = functools.reduce(lambda a, b: a + b, [sc * dn for sc, dn in zip(scales, dens)])
        ob = None
        for sc, num in zip(scales, nums):
            term = expand(sc / tot) * num
            ob = term if ob is None else ob + term
        st["oa"] = _rms(oa_ref[rows, :].astype(F32), goa_ref[...]).astype(BF16)
        st["ob"] = _rms(ob, gob_ref[...]).astype(BF16)

    def mix(st):
        m = (jnp.dot(st.pop("oa"), wo_ref[:WIDTH_A, :], preferred_element_type=F32)
             + jnp.dot(st.pop("ob"), wo_ref[WIDTH_A:, :], preferred_element_type=F32))
        st["h"] = x_ref[st["rows"], :] + _rms(m, gpm_ref[...])
        st["v"] = _rms(st["h"], gpre_ref[...]).astype(BF16)
        st["ff"] = None

    def mlp_chunk(c):
        def stage(st):
            a = jnp.maximum(
                jnp.dot(st["v"], wup_ref[:, c:c + ff_chunk], preferred_element_type=F32), 0.0)
            t = jnp.dot((a * a).astype(BF16), wdn_ref[c:c + ff_chunk, :],
                        preferred_element_type=F32)
            st["ff"] = t if st["ff"] is None else st["ff"] + t
        return stage

    def ple(st):
        h = st["h"] + _rms(st.pop("ff"), gmlp_ref[...])
        gate = jax.nn.sigmoid(
            jnp.dot(h.astype(BF16), wpg_ref[...], preferred_element_type=F32) + bpg_ref[...])
        emb = jnp.dot(p_ref[st["rows"], :].astype(BF16), wpp_ref[...],
                      preferred_element_type=F32)
        out_ref[st["rows"], :] = h + _rms(gate * emb, gple_ref[...])

    stages = [combine, mix] + [mlp_chunk(c) for c in range(0, wup_ref.shape[1], ff_chunk)] + [ple]
    starts = np.cumsum((0,) + chain_rows)
    states = [{"rows": slice(int(lo), int(hi))} for lo, hi in zip(starts[:-1], starts[1:])]
    for stage in stages:
        for st in states:
            stage(st)


def _post(x2d, p2d, oa, obs, stats, dils, consts, tm, ff_chunk):
    n, d_model = x2d.shape
    row = lambda width: pl.BlockSpec((tm, width), lambda i: (i, 0))
    const = lambda a: pl.BlockSpec(a.shape, lambda i: (0,) * a.ndim,
                                   pipeline_mode=pl.Buffered(1))
    scratch = []
    for d in dils:
        if d > 1:
            scratch += [pltpu.VMEM((WIDTH_B // LANES, tm, LANES), F32),
                        pltpu.VMEM((tm, LANES), F32)]
    return pl.pallas_call(
        functools.partial(_post_kernel, dils=dils, ff_chunk=ff_chunk,
                          chain_rows=tuple(tm * f // sum(_ROW_CHAINS) for f in _ROW_CHAINS)),
        grid=(n // tm,),
        in_specs=[row(d_model), row(p2d.shape[1]), row(WIDTH_A)]
                 + [pl.BlockSpec((tm // d, d * WIDTH_B), lambda i: (i, 0)) for d in dils]
                 + [pl.BlockSpec((tm // d, d * LANES), lambda i: (i, 0)) for d in dils]
                 + [const(a) for a in consts],
        out_specs=row(d_model),
        out_shape=jax.ShapeDtypeStruct((n, d_model), x2d.dtype),
        scratch_shapes=scratch,
        compiler_params=pltpu.CompilerParams(
            dimension_semantics=("parallel",), vmem_limit_bytes=_VMEM_LIMIT),
        name="post",
    )(x2d, p2d, oa, *obs, *stats, *consts)


def kernel(x, p, rel_bias_table, g_pre_mix, w_in, sink_a, g_out_a, g_out_b, w_o, g_post_mix,
           g_pre_mlp, w_up, w_down, g_post_mlp, w_ple_proj, w_ple_gate, b_ple_gate, g_post_ple):
    b, s, d_model = x.shape
    depth = w_in.shape[0]
    n = b * s
    tm = _TOKEN_TILE
    assert n % tm == 0 and s % tm == 0
    dils = tuple(dil for _, dil in DILATED_PATTERNS)

    def pair_heads(a, axis):
        shape = a.shape
        a = a.reshape(shape[:axis] + (N_KV_A, GROUP_A, HEAD_DIM) + shape[axis + 1:])
        return jnp.swapaxes(a, axis, axis + 1).reshape(shape)

    idx_a, shifts_a = _band_layout(WINDOW_A, 3 * WINDOW_A, (0, WINDOW_A, 2 * WINDOW_A),
                                   1, WINDOW_A)
    half = DILATED_PATTERNS[0][0] // (2 * DILATED_PATTERNS[0][1])
    idx_b = []
    for window, dil in DILATED_PATTERNS:
        assert window // (2 * dil) == half
        idx, shifts_b = _band_layout(2 * half, 4 * half, (0, half, 2 * half), dil, half)
        idx_b.append(idx)
    bias_a, bias_b = _bias(
        rel_bias_table.T.astype(F32) * LOG2E,
        jnp.asarray(idx_a)[None, None], shifts_a,
        (len(shifts_a), N_HEADS_A, WINDOW_A, 3 * WINDOW_A),
        jnp.asarray(np.stack(idx_b))[:, None], shifts_b,
        (len(DILATED_PATTERNS) * len(shifts_b), N_HEADS_B, 2 * half, 4 * half))

    expand_np = np.zeros((2 * LANES, WIDTH_B), np.float32)
    for hh in range(N_HEADS_B):
        expand_np[hh, hh * HEAD_DIM:(hh + 1) * HEAD_DIM] = 1.0
        expand_np[LANES + hh, hh * HEAD_DIM:(hh + 1) * HEAD_DIM] = 1.0
    expand = jnp.asarray(expand_np, BF16)

    h2d = x.reshape(n, d_model)
    row = lambda a: a.reshape(1, -1).astype(F32)
    for i in range(depth):
        scale = HEAD_DIM ** -0.5 * LOG2E
        w = w_in[i]
        o0 = WIDTH_A + 2 * KV_WIDTH_A
        w_all = jnp.concatenate([
            pair_heads(w[:, :WIDTH_A], 1) * scale,
            w[:, WIDTH_A:o0],
            w[:, o0:o0 + WIDTH_B] * scale,
            w[:, o0 + WIDTH_B:],
        ], axis=1).astype(BF16)
        outs = _inproj(h2d, row(g_pre_mix[i]), w_all, tm, dils)
        o_a = _windowed(outs[0].reshape(b, s, -1), bias_a, sink_a[i].astype(F32) * LOG2E)
        obs, stats = [], []
        for pi, (window, dil) in enumerate(DILATED_PATTERNS):
            o, st = _dilated(outs[1 + pi], bias_b, pi, b, dil, window // (2 * dil))
            obs.append(o)
            stats.append(st)

        wo = jnp.concatenate([pair_heads(w_o[i][:WIDTH_A], 0), w_o[i][WIDTH_A:]],
                             axis=0).astype(BF16)
        consts = (expand, row(pair_heads(g_out_a[i], 0)), row(g_out_b[i]), wo, row(g_post_mix[i]),
                  row(g_pre_mlp[i]), w_up[i].astype(BF16), w_down[i].astype(BF16),
                  row(g_post_mlp[i]), w_ple_proj[i].astype(BF16), w_ple_gate[i].astype(BF16),
                  row(b_ple_gate[i]), row(g_post_ple[i]))
        h2d = _post(h2d, p[i].reshape(n, -1), o_a.reshape(n, WIDTH_A), obs, stats, dils,
                    consts, tm, _FF_CHUNK)
    return h2d.reshape(b, s, d_model)
```

```python
import functools
import math

import jax
import jax.numpy as jnp
import numpy as np
from jax import lax
from jax.experimental import pallas as pl
from jax.experimental.pallas import tpu as pltpu

HEAD_DIM = 64
N_HEADS_A = 8
N_KV_A = 2
GROUP_A = N_HEADS_A // N_KV_A
WINDOW_A = 128
N_HEADS_B = 8
DILATED_PATTERNS = ((128, 1), (512, 4), (2048, 16))
WIDTH_A = N_HEADS_A * HEAD_DIM
WIDTH_B = N_HEADS_B * HEAD_DIM
KV_WIDTH_A = N_KV_A * HEAD_DIM
NUM_BUCKETS = 32
MAX_DISTANCE = 1024
EPS = 1e-6
NEG = -1e30
LOG2E = 1.4426950408889634

LANES = 128
SUBLANES = 8
F32 = jnp.float32
BF16 = jnp.bfloat16

_VMEM_LIMIT = 56 * 1024 * 1024
_TOKEN_TILE = 512
_FF_CHUNK = 2048
_BLOCKS_PER_TRIP = 16
_WIN_LOOKAHEAD = 2
_DIL_LOOKAHEAD = 4
_ROW_CHAINS = (1, 1)


def _rms(x, g):
    ms = jnp.mean(x * x, axis=-1, keepdims=True)
    return (x * lax.rsqrt(ms + EPS)) * g


def _t5_bucket_np(rel):
    half = NUM_BUCKETS // 2
    max_exact = half // 2
    sign = np.where(rel > 0, half, 0)
    n = np.abs(rel)
    nf = np.maximum(n, 1).astype(np.float32)
    large = max_exact + (np.log(nf / np.float32(max_exact))
                         / np.float32(math.log(MAX_DISTANCE / max_exact))
                         * np.float32(half - max_exact)).astype(np.int32)
    large = np.minimum(large, half - 1)
    return (sign + np.where(n < max_exact, n, large)).astype(np.int32)


def _band_layout(q_len, k_len, offs, dil, half_window):
    center = q_len - 1 + max(offs)
    length = -(-(center + k_len) // LANES) * LANES
    rel = np.arange(length) - center
    idx = np.where(np.abs(rel) <= half_window, _t5_bucket_np(rel * dil), -1).astype(np.int32)
    return idx, tuple((off - center) % length for off in offs)


def _bias_kernel(tab_ref, idxa_ref, idxb_ref, outa_ref, outb_ref, *, shifts_a, shifts_b):
    def build(idx_ref, out_ref, head0, shifts):
        n_pat, _, length = idx_ref.shape
        _, n_heads, q_len, k_len = out_ref.shape
        for p in range(n_pat):
            idx = idx_ref[p]
            vec = jnp.full((n_heads, length), NEG, F32)
            for b in range(NUM_BUCKETS):
                vec = jnp.where(idx == b, tab_ref[head0:head0 + n_heads, b:b + 1], vec)
            for h in range(n_heads):
                rows = jnp.broadcast_to(vec[h:h + 1, :], (q_len, length))
                for v, s in enumerate(shifts):
                    band = pltpu.roll(rows, s, 1, stride=1, stride_axis=0)
                    out_ref[p * len(shifts) + v, h] = band[:, :k_len]

    build(idxa_ref, outa_ref, 0, shifts_a)
    build(idxb_ref, outb_ref, N_HEADS_A, shifts_b)


def _folded_bias_kernel(tab_ref, idx_ref, out_ref, *, head0, buckets):
    n_tiles, rows, _ = idx_ref.shape
    n_heads = out_ref.shape[1]

    def chunk(i, c):
        t = i // (rows // SUBLANES)
        r0 = pl.multiple_of((i % (rows // SUBLANES)) * SUBLANES, SUBLANES)
        idx = idx_ref[t, pl.ds(r0, SUBLANES), :]
        accs = [jnp.full(idx.shape, NEG, F32)] * n_heads
        for b in buckets:
            mask = idx == b
            accs = [jnp.where(mask, tab_ref[b, head0 + h], a) for h, a in enumerate(accs)]
        for h in range(n_heads):
            out_ref[t, h, pl.ds(r0, SUBLANES), :] = accs[h]
        return c

    lax.fori_loop(0, n_tiles * (rows // SUBLANES), chunk, 0)


def _folded_bias(table, idx, head0, n_heads):
    vmem = pl.BlockSpec(memory_space=pltpu.VMEM)
    buckets = tuple(int(b) for b in np.unique(idx) if b >= 0)
    return pl.pallas_call(
        functools.partial(_folded_bias_kernel, head0=head0, buckets=buckets),
        in_specs=[pl.BlockSpec(memory_space=pltpu.SMEM), vmem],
        out_specs=vmem,
        out_shape=jax.ShapeDtypeStruct((idx.shape[0], n_heads) + idx.shape[1:], F32),
        compiler_params=pltpu.CompilerParams(vmem_limit_bytes=_VMEM_LIMIT),
        name="folded_bias",
    )(table, jnp.asarray(idx))


def _bias(table_t, idx_a, shifts_a, shape_a, idx_b, shifts_b, shape_b):
    vmem = pl.BlockSpec(memory_space=pltpu.VMEM)
    return pl.pallas_call(
        functools.partial(_bias_kernel, shifts_a=shifts_a, shifts_b=shifts_b),
        in_specs=[vmem, vmem, vmem],
        out_specs=[vmem, vmem],
        out_shape=[jax.ShapeDtypeStruct(shape_a, F32), jax.ShapeDtypeStruct(shape_b, F32)],
        compiler_params=pltpu.CompilerParams(vmem_limit_bytes=_VMEM_LIMIT),
        name="bias",
    )(table_t, idx_a, idx_b)


def _inproj_kernel(x_ref, g_ref, w_ref, *refs, dils, stage_dils):
    n_pat = len(dils)
    win_ref, d_refs = refs[0], refs[1:1 + n_pat]
    scr = refs[1 + n_pat]
    stage_scr = dict(zip(stage_dils, refs[2 + n_pat:]))
    tm = x_ref.shape[0]
    n_slab = WIDTH_B // LANES

    x = x_ref[...]
    u = (x * g_ref[...]).astype(BF16)
    rstd = lax.rsqrt(jnp.mean(x * x, axis=-1, keepdims=True) + EPS)

    col = win_ref.shape[-1]
    for t in range(3):
        seg = rstd * jnp.dot(u, w_ref[:, col:col + WIDTH_B], preferred_element_type=F32)
        col += WIDTH_B
        if any(d > 1 for d in dils):
            for c in range(n_slab):
                scr[t, c] = seg[:, c * LANES:(c + 1) * LANES]
        for d, o_ref in zip(dils, d_refs):
            if d == 1:
                o_ref[:, t * WIDTH_B:(t + 1) * WIDTH_B] = seg.astype(BF16)
                continue
            base = max([b for b in stage_dils if b < d and d % b == 0], default=1)
            step = d // base
            for r in range(d):
                for c in range(n_slab):
                    rows = pl.ds(r // base, tm // d, stride=step)
                    src = scr.at[t, c] if base == 1 else stage_scr[base].at[t, r % base, c]
                    val = src[rows, :]
                    if d in stage_scr:
                        stage_scr[d][t, r, c] = val
                    lo = (3 * r + t) * WIDTH_B + c * LANES
                    o_ref[:, lo:lo + LANES] = val.astype(BF16)
    win_ref[...] = (rstd * jnp.dot(u, w_ref[:, :win_ref.shape[-1]],
                                   preferred_element_type=F32)).astype(BF16)


def _inproj(x2d, g, w, tm, dils):
    n, d_model = x2d.shape
    shapes = [(n, WIDTH_A + 2 * KV_WIDTH_A)]
    blocks = [(tm, WIDTH_A + 2 * KV_WIDTH_A)]
    for d in dils:
        assert tm % (16 * d) == 0 and n % d == 0
        shapes.append((n // d, d * 3 * WIDTH_B))
        blocks.append((tm // d, d * 3 * WIDTH_B))
    stage_dils = tuple(b for b in dils if b > 1 and any(d > b and d % b == 0 for d in dils))
    n_slab = WIDTH_B // LANES
    return pl.pallas_call(
        functools.partial(_inproj_kernel, dils=dils, stage_dils=stage_dils),
        grid=(n // tm,),
        in_specs=[
            pl.BlockSpec((tm, d_model), lambda i: (i, 0)),
            pl.BlockSpec((1, d_model), lambda i: (0, 0)),
            pl.BlockSpec(w.shape, lambda i: (0, 0)),
        ],
        out_specs=[pl.BlockSpec(bs, lambda i: (i, 0)) for bs in blocks],
        out_shape=[jax.ShapeDtypeStruct(sh, BF16) for sh in shapes],
        scratch_shapes=[pltpu.VMEM((3, n_slab, tm, LANES), F32)]
                       + [pltpu.VMEM((3, b, n_slab, tm // b, LANES), F32) for b in stage_dils],
        compiler_params=pltpu.CompilerParams(
            dimension_semantics=("parallel",), vmem_limit_bytes=_VMEM_LIMIT),
        name="inproj",
    )(x2d, g, w)


def _nt_dot(a, b):
    return lax.dot_general(a, b, (((1,), (1,)), ((), ())), preferred_element_type=F32)


def _win_kernel(sink_ref, qkv_ref, bias_ref, o_ref, *, seq, blk, unroll):
    q_ref = qkv_ref.at[:, :WIDTH_A]
    k_ref = qkv_ref.at[:, WIDTH_A:WIDTH_A + KV_WIDTH_A]
    v_ref = qkv_ref.at[:, WIDTH_A + KV_WIDTH_A:]
    nblk = seq // blk
    lane = lax.broadcasted_iota(jnp.int32, (blk, LANES), 1)
    low = lane < HEAD_DIM

    klen = 3 * blk

    def block_context(n):
        q0 = pl.multiple_of(n * blk, blk)
        k0 = pl.multiple_of(jnp.clip(q0 - blk, 0, seq - klen), blk)
        variant = jnp.where(n == 0, 0, jnp.where(n == nblk - 1, 2, 1))
        k1 = k_ref[pl.ds(k0, 2 * blk), :]
        k2 = k_ref[pl.ds(k0 + 2 * blk, blk), :]
        v1 = v_ref[pl.ds(k0, 2 * blk), :]
        v2 = v_ref[pl.ds(k0 + 2 * blk, blk), :]
        zero = jnp.zeros_like(k2)
        one_low = low.astype(F32).astype(BF16)
        one_high = (~low).astype(F32).astype(BF16)
        return dict(
            q0=q0, variant=variant, k1=k1,
            v1_aug=jnp.concatenate([v1, jnp.ones_like(v1)], axis=1),
            k2_pack=jnp.concatenate([jnp.where(low, k2, zero), jnp.where(low, zero, k2)], axis=0),
            v2_pack=jnp.concatenate(
                [jnp.concatenate([jnp.where(low, v2, zero), one_low], axis=1),
                 jnp.concatenate([jnp.where(low, zero, v2), one_high], axis=1)], axis=0))

    def scores(ctx, j):
        q_tile = q_ref[pl.ds(ctx["q0"], blk), j * LANES:(j + 1) * LANES]
        s2 = _nt_dot(q_tile, ctx["k2_pack"])
        out = []
        for hf in range(N_KV_A):
            qm = jnp.where(low if hf == 0 else ~low, q_tile, jnp.zeros_like(q_tile))
            out.append(jnp.concatenate(
                [_nt_dot(qm, ctx["k1"]), s2[:, hf * LANES:(hf + 1) * LANES]], axis=1)
                + bias_ref[ctx["variant"], hf * GROUP_A + j])
        return out

    def softmax_pv(ctx, j, logits):
        nums, dens, e2s = [], [], []
        for hf, s in enumerate(logits):
            sink = sink_ref[hf * GROUP_A + j]
            m = jnp.maximum(jnp.max(s, axis=-1, keepdims=True), sink)
            e = jnp.exp2(s - m).astype(BF16)
            o1 = jnp.dot(e[:, :2 * blk], ctx["v1_aug"], preferred_element_type=F32)
            nums.append(o1[:, :LANES])
            dens.append(o1[:, LANES:] + jnp.exp2(sink - m))
            e2s.append(e[:, 2 * blk:])
        o2 = jnp.dot(jnp.concatenate(e2s, axis=1), ctx["v2_pack"], preferred_element_type=F32)
        num = jnp.where(low, nums[0], nums[1]) + o2[:, :LANES]
        den = jnp.where(low, dens[0], dens[1]) + o2[:, LANES:]
        o_ref[pl.ds(ctx["q0"], blk), j * LANES:(j + 1) * LANES] = (num / den).astype(o_ref.dtype)

    def trip(t, c):
        pending = []
        for i in range(unroll):
            ctx = block_context(t * unroll + i)
            for j in range(GROUP_A):
                pending.append((ctx, j, scores(ctx, j)))
                if len(pending) > _WIN_LOOKAHEAD:
                    softmax_pv(*pending.pop(0))
        for task in pending:
            softmax_pv(*task)
        return c

    lax.fori_loop(0, nblk // unroll, trip, 0)


def _windowed(qkv, bias, sink):
    b, s, width = qkv.shape
    blk = WINDOW_A
    nblk = s // blk
    assert s % blk == 0 and nblk >= 3
    kern = functools.partial(_win_kernel, seq=s, blk=blk,
                             unroll=math.gcd(nblk, _BLOCKS_PER_TRIP))
    return pl.pallas_call(
        kern,
        grid=(b,),
        in_specs=[
            pl.BlockSpec(memory_space=pltpu.SMEM),
            pl.BlockSpec((None, s, width), lambda i: (i, 0, 0)),
            pl.BlockSpec(bias.shape, lambda i: (0, 0, 0, 0)),
        ],
        out_specs=pl.BlockSpec((None, s, WIDTH_A), lambda i: (i, 0, 0)),
        out_shape=jax.ShapeDtypeStruct((b, s, WIDTH_A), BF16),
        compiler_params=pltpu.CompilerParams(
            dimension_semantics=("parallel",), vmem_limit_bytes=_VMEM_LIMIT),
        name="win_gqa",
    )(sink, qkv, bias)


def _dil_kernel(qkv_ref, bias_ref, o_ref, st_ref, *, ls, half, n_res, fold, unroll):
    qb = 2 * half
    kw = 4 * half
    nblk = ls // qb
    lane = lax.broadcasted_iota(jnp.int32, (qb, LANES), 1)
    low = lane < HEAD_DIM

    n_pairs = N_HEADS_B // 2
    assert fold == 1 or n_res == 1

    def rows_of(start, align, size):
        row0 = start if fold == 1 else start // fold
        return pl.ds(pl.multiple_of(row0, align // fold), size // fold)

    def load(tensor, start, align, size, res_i, j):
        parts = []
        for g in range(fold):
            col = ((res_i * fold + g) * 3 + tensor) * WIDTH_B + j * LANES
            parts.append(qkv_ref[rows_of(start, align, size), col:col + LANES])
        return parts[0] if fold == 1 else jnp.concatenate(parts, axis=0)

    def store(ref, start, res_i, col0, val):
        rows = qb // fold
        group_width = ref.shape[-1] // (n_res * fold)
        for g in range(fold):
            col = (res_i * fold + g) * group_width + col0
            ref[rows_of(start, qb, qb), col:col + LANES] = val[g * rows:(g + 1) * rows]

    def scores(n, res_i, j):
        q0 = pl.multiple_of(n * qb, qb)
        k0 = pl.multiple_of(jnp.clip(q0 - half, 0, ls - kw), half)
        variant = jnp.where(n == 0, 0, jnp.where(n == nblk - 1, 2, 1))
        q_tile = load(0, q0, qb, qb, res_i, j)
        k_win = load(1, k0, half, kw, res_i, j)
        logits = []
        for hf in range(2):
            qm = jnp.where(low if hf == 0 else ~low, q_tile, jnp.zeros_like(q_tile))
            logits.append(_nt_dot(qm, k_win) + bias_ref[variant, 2 * j + hf])
        return dict(q0=q0, k0=k0, res_i=res_i, j=j, logits=logits)

    def softmax_pv(task, stats):
        j, res_i = task["j"], task["res_i"]
        v_win = load(2, task["k0"], half, kw, res_i, j)
        v_aug = jnp.concatenate([v_win, jnp.ones_like(v_win)], axis=1)
        st_acc = jnp.zeros((qb, LANES), F32) if j == 0 else stats.pop()
        nums = []
        for hf, s in enumerate(task["logits"]):
            h = 2 * j + hf
            m = jnp.max(s, axis=-1, keepdims=True)
            o = jnp.dot(jnp.exp2(s - m).astype(BF16), v_aug, preferred_element_type=F32)
            nums.append(o[:, :LANES])
            st_acc = jnp.where(lane == h, m,
                               jnp.where(lane == N_HEADS_B + h, o[:, LANES:], st_acc))
        store(o_ref, task["q0"], res_i, j * LANES,
              jnp.where(low, nums[0], nums[1]).astype(o_ref.dtype))
        if j == n_pairs - 1:
            store(st_ref, task["q0"], res_i, 0, st_acc)
        else:
            stats.append(st_acc)

    def trip(t, c):
        pending, stats = [], []
        for i in range(unroll):
            for res_i in range(n_res):
                for j in range(n_pairs):
                    pending.append(scores(t * unroll + i, res_i, j))
                    if len(pending) > _DIL_LOOKAHEAD:
                        softmax_pv(pending.pop(0), stats)
        for task in pending:
            softmax_pv(task, stats)
        return c

    lax.fori_loop(0, nblk // unroll, trip, 0)


def _dilated(qkv, bias, bias_index, batch, dil, layout, half):
    rows = qkv.shape[0]
    assert qkv.shape[1] == layout * 3 * WIDTH_B and layout % dil == 0
    fold = layout // dil
    ls = rows * fold // batch
    nblk = ls // (2 * half)
    assert ls % (2 * half) == 0 and nblk >= 2 and (fold == 1 or dil == 1)
    n_res = math.gcd(dil, max(1, _BLOCKS_PER_TRIP // nblk))
    unroll = math.gcd(nblk, max(1, _BLOCKS_PER_TRIP // n_res))
    kern = functools.partial(_dil_kernel, ls=ls, half=half, n_res=n_res, fold=fold,
                             unroll=unroll)
    lanes = lambda width: pl.BlockSpec((None, ls // fold, n_res * fold * width),
                                       lambda i, r: (i, 0, r))
    n_var = 3
    o, st = pl.pallas_call(
        kern,
        grid=(batch, dil // n_res),
        in_specs=[lanes(3 * WIDTH_B),
                  pl.BlockSpec((n_var,) + bias.shape[1:], lambda i, r: (bias_index, 0, 0, 0))],
        out_specs=[lanes(WIDTH_B), lanes(LANES)],
        out_shape=[jax.ShapeDtypeStruct((batch, ls // fold, layout * WIDTH_B), BF16),
                   jax.ShapeDtypeStruct((batch, ls // fold, layout * LANES), F32)],
        compiler_params=pltpu.CompilerParams(
            dimension_semantics=("parallel", "parallel"), vmem_limit_bytes=_VMEM_LIMIT),
        name=f"dilated_d{dil}",
    )(qkv.reshape(batch, ls // fold, layout * 3 * WIDTH_B), bias)
    return o.reshape(rows, layout * WIDTH_B), st.reshape(rows, layout * LANES)


def _post_kernel(x_ref, p_ref, oa_ref, *refs, dils, ff_chunk, chain_rows):
    n_pat = len(dils)
    ob_refs = refs[:n_pat]
    st_refs = refs[n_pat:2 * n_pat]
    (expand_ref, goa_ref, gob_ref, wo_ref, gpm_ref, gpre_ref, wup_ref, wdn_ref,
     gmlp_ref, wpp_ref, wpg_ref, bpg_ref, gple_ref, out_ref) = refs[2 * n_pat:2 * n_pat + 14]
    scratch = refs[2 * n_pat + 14:]
    tm = x_ref.shape[0]
    assert sum(chain_rows) == tm
    n_slab = WIDTH_B // LANES

    def expand(w):
        head_lane = lax.broadcasted_iota(jnp.int32, w.shape, 1) < N_HEADS_B
        w = jnp.where(head_lane, w, 0.0)
        hi = w.astype(BF16)
        lo = (w - hi.astype(F32)).astype(BF16)
        return jnp.dot(jnp.concatenate([hi, lo], axis=1), expand_ref[...],
                       preferred_element_type=F32)

    def combine(st):
        rows = st["rows"]
        nums, stats = [], []
        si = 0
        for d, o_ref, s_ref in zip(dils, ob_refs, st_refs):
            if d == 1:
                nums.append(o_ref[rows, :].astype(F32))
                stats.append(s_ref[rows, :])
                continue
            o_scr, s_scr = scratch[si], scratch[si + 1]
            si += 2
            assert rows.start % d == 0 and rows.stop % d == 0
            src = slice(rows.start // d, rows.stop // d)
            for r in range(d):
                dst = pl.ds(rows.start + r, (rows.stop - rows.start) // d, stride=d)
                for c in range(n_slab):
                    lo = r * WIDTH_B + c * LANES
                    o_scr[c, dst, :] = o_ref[src, lo:lo + LANES].astype(F32)
                s_scr[dst, :] = s_ref[src, r * LANES:(r + 1) * LANES]
            nums.append(jnp.concatenate([o_scr[c, rows, :] for c in range(n_slab)], axis=1))
            stats.append(s_scr[rows, :])
        mx = functools.reduce(jnp.maximum, stats)
        scales = [jnp.exp2(s - mx) for s in stats]
        dens = [pltpu.roll(s, LANES - N_HEADS_B, 1) for s in stats]
        tot = functools.reduce(lambda a, b: a + b, [sc * dn for sc, dn in zip(scales, dens)])
        ob = None
        for sc, num in zip(scales, nums):
            term = expand(sc / tot) * num
            ob = term if ob is None else ob + term
        st["oa"] = _rms(oa_ref[rows, :].astype(F32), goa_ref[...]).astype(BF16)
        st["ob"] = _rms(ob, gob_ref[...]).astype(BF16)

    def mix(st):
        m = (jnp.dot(st.pop("oa"), wo_ref[:WIDTH_A, :], preferred_element_type=F32)
             + jnp.dot(st.pop("ob"), wo_ref[WIDTH_A:, :], preferred_element_type=F32))
        st["h"] = x_ref[st["rows"], :] + _rms(m, gpm_ref[...])
        st["v"] = _rms(st["h"], gpre_ref[...]).astype(BF16)
        st["ff"] = None

    def mlp_chunk(c):
        def stage(st):
            a = jnp.maximum(
                jnp.dot(st["v"], wup_ref[:, c:c + ff_chunk], preferred_element_type=F32), 0.0)
            t = jnp.dot((a * a).astype(BF16), wdn_ref[c:c + ff_chunk, :],
                        preferred_element_type=F32)
            st["ff"] = t if st["ff"] is None else st["ff"] + t
        return stage

    def ple(st):
        h = st["h"] + _rms(st.pop("ff"), gmlp_ref[...])
        gate = jax.nn.sigmoid(
            jnp.dot(h.astype(BF16), wpg_ref[...], preferred_element_type=F32) + bpg_ref[...])
        emb = jnp.dot(p_ref[st["rows"], :].astype(BF16), wpp_ref[...],
                      preferred_element_type=F32)
        out_ref[st["rows"], :] = h + _rms(gate * emb, gple_ref[...])

    stages = [combine, mix] + [mlp_chunk(c) for c in range(0, wup_ref.shape[1], ff_chunk)] + [ple]
    starts = np.cumsum((0,) + chain_rows)
    states = [{"rows": slice(int(lo), int(hi))} for lo, hi in zip(starts[:-1], starts[1:])]
    for stage in stages:
        for st in states:
            stage(st)


def _post(x2d, p2d, oa, obs, stats, dils, consts, tm, ff_chunk):
    n, d_model = x2d.shape
    row = lambda width: pl.BlockSpec((tm, width), lambda i: (i, 0))
    const = lambda a: pl.BlockSpec(a.shape, lambda i: (0,) * a.ndim,
                                   pipeline_mode=pl.Buffered(1))
    scratch = []
    for d in dils:
        if d > 1:
            scratch += [pltpu.VMEM((WIDTH_B // LANES, tm, LANES), F32),
                        pltpu.VMEM((tm, LANES), F32)]
    return pl.pallas_call(
        functools.partial(_post_kernel, dils=dils, ff_chunk=ff_chunk,
                          chain_rows=tuple(tm * f // sum(_ROW_CHAINS) for f in _ROW_CHAINS)),
        grid=(n // tm,),
        in_specs=[row(d_model), row(p2d.shape[1]), row(WIDTH_A)]
                 + [pl.BlockSpec((tm // d, d * WIDTH_B), lambda i: (i, 0)) for d in dils]
                 + [pl.BlockSpec((tm // d, d * LANES), lambda i: (i, 0)) for d in dils]
                 + [const(a) for a in consts],
        out_specs=row(d_model),
        out_shape=jax.ShapeDtypeStruct((n, d_model), x2d.dtype),
        scratch_shapes=scratch,
        compiler_params=pltpu.CompilerParams(
            dimension_semantics=("parallel",), vmem_limit_bytes=_VMEM_LIMIT),
        name="post",
    )(x2d, p2d, oa, *obs, *stats, *consts)


def kernel(x, p, rel_bias_table, g_pre_mix, w_in, sink_a, g_out_a, g_out_b, w_o, g_post_mix,
           g_pre_mlp, w_up, w_down, g_post_mlp, w_ple_proj, w_ple_gate, b_ple_gate, g_post_ple):
    b, s, d_model = x.shape
    depth = w_in.shape[0]
    n = b * s
    tm = _TOKEN_TILE
    assert n % tm == 0 and s % tm == 0
    dils = tuple(dil for _, dil in DILATED_PATTERNS)

    def pair_heads(a, axis):
        shape = a.shape
        a = a.reshape(shape[:axis] + (N_KV_A, GROUP_A, HEAD_DIM) + shape[axis + 1:])
        return jnp.swapaxes(a, axis, axis + 1).reshape(shape)

    idx_a, shifts_a = _band_layout(WINDOW_A, 3 * WINDOW_A, (0, WINDOW_A, 2 * WINDOW_A),
                                   1, WINDOW_A)
    half = DILATED_PATTERNS[0][0] // (2 * DILATED_PATTERNS[0][1])
    idx_b = []
    for window, dil in DILATED_PATTERNS:
        assert window // (2 * dil) == half
        idx, shifts_b = _band_layout(2 * half, 4 * half, (0, half, 2 * half), dil, half)
        idx_b.append(idx)
    table = rel_bias_table.astype(F32) * LOG2E
    bias_a, bias_b = _bias(
        table.T,
        jnp.asarray(idx_a)[None, None], shifts_a,
        (len(shifts_a), N_HEADS_A, WINDOW_A, 3 * WINDOW_A),
        jnp.asarray(np.stack(idx_b))[:, None], shifts_b,
        (len(DILATED_PATTERNS) * len(shifts_b), N_HEADS_B, 2 * half, 4 * half))

    folds_ok = [d for d in dils if d > 1 and (2 * half) % (16 * d) == 0]
    layouts = tuple(min(folds_ok) if d == 1 and folds_ok else d for d in dils)
    copy_dils = tuple(sorted(set(layouts)))
    bias_of = []
    for pi, (d, layout) in enumerate(zip(dils, layouts)):
        fold = layout // d
        if fold == 1:
            bias_of.append((bias_b, pi))
            continue
        q_rows, k_rows = 2 * half // fold, 4 * half // fold
        q_off = (np.arange(2 * half) % q_rows) * fold + np.arange(2 * half) // q_rows
        k_off = (np.arange(4 * half) % k_rows) * fold + np.arange(4 * half) // k_rows
        rel = np.stack([k_off[None, :] - off - q_off[:, None] for off in (0, half, 2 * half)])
        idx = np.where(np.abs(rel) <= half, _t5_bucket_np(rel * d), -1).astype(np.int32)
        bias_of.append((_folded_bias(table, idx, N_HEADS_A, N_HEADS_B), 0))

    expand_np = np.zeros((2 * LANES, WIDTH_B), np.float32)
    for hh in range(N_HEADS_B):
        expand_np[hh, hh * HEAD_DIM:(hh + 1) * HEAD_DIM] = 1.0
        expand_np[LANES + hh, hh * HEAD_DIM:(hh + 1) * HEAD_DIM] = 1.0
    expand = jnp.asarray(expand_np, BF16)

    h2d = x.reshape(n, d_model)
    row = lambda a: a.reshape(1, -1).astype(F32)
    for i in range(depth):
        scale = HEAD_DIM ** -0.5 * LOG2E
        w = w_in[i]
        o0 = WIDTH_A + 2 * KV_WIDTH_A
        w_all = jnp.concatenate([
            pair_heads(w[:, :WIDTH_A], 1) * scale,
            w[:, WIDTH_A:o0],
            w[:, o0:o0 + WIDTH_B] * scale,
            w[:, o0 + WIDTH_B:],
        ], axis=1).astype(BF16)
        outs = _inproj(h2d, row(g_pre_mix[i]), w_all, tm, copy_dils)
        o_a = _windowed(outs[0].reshape(b, s, -1), bias_a, sink_a[i].astype(F32) * LOG2E)
        obs, stats = [], []
        for (window, dil), layout, (bias, bias_index) in zip(DILATED_PATTERNS, layouts, bias_of):
            o, st = _dilated(outs[1 + copy_dils.index(layout)], bias, bias_index, b, dil, layout,
                             window // (2 * dil))
            obs.append(o)
            stats.append(st)

        wo = jnp.concatenate([pair_heads(w_o[i][:WIDTH_A], 0), w_o[i][WIDTH_A:]],
                             axis=0).astype(BF16)
        consts = (expand, row(pair_heads(g_out_a[i], 0)), row(g_out_b[i]), wo, row(g_post_mix[i]),
                  row(g_pre_mlp[i]), w_up[i].astype(BF16), w_down[i].astype(BF16),
                  row(g_post_mlp[i]), w_ple_proj[i].astype(BF16), w_ple_gate[i].astype(BF16),
                  row(b_ple_gate[i]), row(g_post_ple[i]))
        h2d = _post(h2d, p[i].reshape(n, -1), o_a.reshape(n, WIDTH_A), obs, stats, layouts,
                    consts, tm, _FF_CHUNK)
    return h2d.reshape(b, s, d_model)
```

```python
import functools
import math

import jax
import jax.numpy as jnp
import numpy as np
from jax import lax
from jax.experimental import pallas as pl
from jax.experimental.pallas import tpu as pltpu

HEAD_DIM = 64
N_HEADS_A = 8
N_KV_A = 2
GROUP_A = N_HEADS_A // N_KV_A
WINDOW_A = 128
N_HEADS_B = 8
DILATED_PATTERNS = ((128, 1), (512, 4), (2048, 16))
WIDTH_A = N_HEADS_A * HEAD_DIM
WIDTH_B = N_HEADS_B * HEAD_DIM
KV_WIDTH_A = N_KV_A * HEAD_DIM
NUM_BUCKETS = 32
MAX_DISTANCE = 1024
EPS = 1e-6
NEG = -1e30
LOG2E = 1.4426950408889634

LANES = 128
F32 = jnp.float32
BF16 = jnp.bfloat16

_VMEM_LIMIT = 56 * 1024 * 1024
_TOKEN_TILE = 512
_FF_CHUNK = 2048
_BLOCKS_PER_TRIP = 16
_WIN_LOOKAHEAD = 2
_DIL_LOOKAHEAD = 4
_ROW_CHAINS = (1, 1)


def _rms(x, g):
    ms = jnp.mean(x * x, axis=-1, keepdims=True)
    return (x * lax.rsqrt(ms + EPS)) * g


def _t5_bucket_np(rel):
    half = NUM_BUCKETS // 2
    max_exact = half // 2
    sign = np.where(rel > 0, half, 0)
    n = np.abs(rel)
    nf = np.maximum(n, 1).astype(np.float32)
    large = max_exact + (np.log(nf / np.float32(max_exact))
                         / np.float32(math.log(MAX_DISTANCE / max_exact))
                         * np.float32(half - max_exact)).astype(np.int32)
    large = np.minimum(large, half - 1)
    return (sign + np.where(n < max_exact, n, large)).astype(np.int32)


def _band_layout(q_len, k_len, offs, dil, half_window):
    center = q_len - 1 + max(offs)
    length = -(-(center + k_len) // LANES) * LANES
    rel = np.arange(length) - center
    idx = np.where(np.abs(rel) <= half_window, _t5_bucket_np(rel * dil), -1).astype(np.int32)
    return idx, tuple((off - center) % length for off in offs)


def _bias_kernel(tab_ref, idxa_ref, idxb_ref, outa_ref, outb_ref, *, shifts_a, shifts_b):
    def build(idx_ref, out_ref, head0, shifts):
        n_pat, _, length = idx_ref.shape
        _, n_heads, q_len, k_len = out_ref.shape
        for p in range(n_pat):
            idx = idx_ref[p]
            vec = jnp.full((n_heads, length), NEG, F32)
            for b in range(NUM_BUCKETS):
                vec = jnp.where(idx == b, tab_ref[head0:head0 + n_heads, b:b + 1], vec)
            for h in range(n_heads):
                rows = jnp.broadcast_to(vec[h:h + 1, :], (q_len, length))
                for v, s in enumerate(shifts):
                    band = pltpu.roll(rows, s, 1, stride=1, stride_axis=0)
                    out_ref[p * len(shifts) + v, h] = band[:, :k_len]

    build(idxa_ref, outa_ref, 0, shifts_a)
    build(idxb_ref, outb_ref, N_HEADS_A, shifts_b)


def _bias(table_t, idx_a, shifts_a, shape_a, idx_b, shifts_b, shape_b):
    vmem = pl.BlockSpec(memory_space=pltpu.VMEM)
    return pl.pallas_call(
        functools.partial(_bias_kernel, shifts_a=shifts_a, shifts_b=shifts_b),
        in_specs=[vmem, vmem, vmem],
        out_specs=[vmem, vmem],
        out_shape=[jax.ShapeDtypeStruct(shape_a, F32), jax.ShapeDtypeStruct(shape_b, F32)],
        compiler_params=pltpu.CompilerParams(vmem_limit_bytes=_VMEM_LIMIT),
        name="bias",
    )(table_t, idx_a, idx_b)


def _inproj_kernel(x_ref, g_ref, w_ref, *refs, dils, stage_dils):
    n_pat = len(dils)
    win_ref, d_refs = refs[0], refs[1:1 + n_pat]
    scr = refs[1 + n_pat]
    stage_scr = dict(zip(stage_dils, refs[2 + n_pat:]))
    tm = x_ref.shape[0]
    n_slab = WIDTH_B // LANES

    x = x_ref[...]
    u = (x * g_ref[...]).astype(BF16)
    rstd = lax.rsqrt(jnp.mean(x * x, axis=-1, keepdims=True) + EPS)

    col = win_ref.shape[-1]
    for t in range(3):
        seg = rstd * jnp.dot(u, w_ref[:, col:col + WIDTH_B], preferred_element_type=F32)
        col += WIDTH_B
        if any(d > 1 for d in dils):
            for c in range(n_slab):
                scr[t, c] = seg[:, c * LANES:(c + 1) * LANES]
        for d, o_ref in zip(dils, d_refs):
            if d == 1:
                o_ref[:, t * WIDTH_B:(t + 1) * WIDTH_B] = seg.astype(BF16)
                continue
            base = max([b for b in stage_dils if b < d and d % b == 0], default=1)
            step = d // base
            for r in range(d):
                for c in range(n_slab):
                    rows = pl.ds(r // base, tm // d, stride=step)
                    src = scr.at[t, c] if base == 1 else stage_scr[base].at[t, r % base, c]
                    val = src[rows, :]
                    if d in stage_scr:
                        stage_scr[d][t, r, c] = val
                    lo = (3 * r + t) * WIDTH_B + c * LANES
                    o_ref[:, lo:lo + LANES] = val.astype(BF16)
    win_ref[...] = (rstd * jnp.dot(u, w_ref[:, :win_ref.shape[-1]],
                                   preferred_element_type=F32)).astype(BF16)


def _inproj(x2d, g, w, tm, dils):
    n, d_model = x2d.shape
    shapes = [(n, WIDTH_A + 2 * KV_WIDTH_A)]
    blocks = [(tm, WIDTH_A + 2 * KV_WIDTH_A)]
    for d in dils:
        assert tm % (16 * d) == 0 and n % d == 0
        shapes.append((n // d, d * 3 * WIDTH_B))
        blocks.append((tm // d, d * 3 * WIDTH_B))
    stage_dils = tuple(b for b in dils if b > 1 and any(d > b and d % b == 0 for d in dils))
    n_slab = WIDTH_B // LANES
    return pl.pallas_call(
        functools.partial(_inproj_kernel, dils=dils, stage_dils=stage_dils),
        grid=(n // tm,),
        in_specs=[
            pl.BlockSpec((tm, d_model), lambda i: (i, 0)),
            pl.BlockSpec((1, d_model), lambda i: (0, 0)),
            pl.BlockSpec(w.shape, lambda i: (0, 0)),
        ],
        out_specs=[pl.BlockSpec(bs, lambda i: (i, 0)) for bs in blocks],
        out_shape=[jax.ShapeDtypeStruct(sh, BF16) for sh in shapes],
        scratch_shapes=[pltpu.VMEM((3, n_slab, tm, LANES), F32)]
                       + [pltpu.VMEM((3, b, n_slab, tm // b, LANES), F32) for b in stage_dils],
        compiler_params=pltpu.CompilerParams(
            dimension_semantics=("parallel",), vmem_limit_bytes=_VMEM_LIMIT),
        name="inproj",
    )(x2d, g, w)


def _nt_dot(a, b):
    return lax.dot_general(a, b, (((1,), (1,)), ((), ())), preferred_element_type=F32)


def _win_kernel(sink_ref, qkv_ref, bias_ref, o_ref, *, seq, blk, unroll):
    q_ref = qkv_ref.at[:, :WIDTH_A]
    k_ref = qkv_ref.at[:, WIDTH_A:WIDTH_A + KV_WIDTH_A]
    v_ref = qkv_ref.at[:, WIDTH_A + KV_WIDTH_A:]
    nblk = seq // blk
    lane = lax.broadcasted_iota(jnp.int32, (blk, LANES), 1)
    low = lane < HEAD_DIM

    klen = 3 * blk

    def block_context(n):
        q0 = pl.multiple_of(n * blk, blk)
        k0 = pl.multiple_of(jnp.clip(q0 - blk, 0, seq - klen), blk)
        variant = jnp.where(n == 0, 0, jnp.where(n == nblk - 1, 2, 1))
        k1 = k_ref[pl.ds(k0, 2 * blk), :]
        k2 = k_ref[pl.ds(k0 + 2 * blk, blk), :]
        v1 = v_ref[pl.ds(k0, 2 * blk), :]
        v2 = v_ref[pl.ds(k0 + 2 * blk, blk), :]
        zero = jnp.zeros_like(k2)
        one_low = low.astype(F32).astype(BF16)
        one_high = (~low).astype(F32).astype(BF16)
        return dict(
            q0=q0, variant=variant, k1=k1,
            v1_aug=jnp.concatenate([v1, jnp.ones_like(v1)], axis=1),
            k2_pack=jnp.concatenate([jnp.where(low, k2, zero), jnp.where(low, zero, k2)], axis=0),
            v2_pack=jnp.concatenate(
                [jnp.concatenate([jnp.where(low, v2, zero), one_low], axis=1),
                 jnp.concatenate([jnp.where(low, zero, v2), one_high], axis=1)], axis=0))

    def scores(ctx, j):
        q_tile = q_ref[pl.ds(ctx["q0"], blk), j * LANES:(j + 1) * LANES]
        s2 = _nt_dot(q_tile, ctx["k2_pack"])
        out = []
        for hf in range(N_KV_A):
            qm = jnp.where(low if hf == 0 else ~low, q_tile, jnp.zeros_like(q_tile))
            out.append(jnp.concatenate(
                [_nt_dot(qm, ctx["k1"]), s2[:, hf * LANES:(hf + 1) * LANES]], axis=1)
                + bias_ref[ctx["variant"], hf * GROUP_A + j])
        return out

    def softmax_pv(ctx, j, logits):
        nums, dens, e2s = [], [], []
        for hf, s in enumerate(logits):
            sink = sink_ref[hf * GROUP_A + j]
            m = jnp.maximum(jnp.max(s, axis=-1, keepdims=True), sink)
            e = jnp.exp2(s - m).astype(BF16)
            o1 = jnp.dot(e[:, :2 * blk], ctx["v1_aug"], preferred_element_type=F32)
            nums.append(o1[:, :LANES])
            dens.append(o1[:, LANES:] + jnp.exp2(sink - m))
            e2s.append(e[:, 2 * blk:])
        o2 = jnp.dot(jnp.concatenate(e2s, axis=1), ctx["v2_pack"], preferred_element_type=F32)
        num = jnp.where(low, nums[0], nums[1]) + o2[:, :LANES]
        den = jnp.where(low, dens[0], dens[1]) + o2[:, LANES:]
        o_ref[pl.ds(ctx["q0"], blk), j * LANES:(j + 1) * LANES] = (num / den).astype(o_ref.dtype)

    def trip(t, c):
        pending = []
        for i in range(unroll):
            ctx = block_context(t * unroll + i)
            for j in range(GROUP_A):
                pending.append((ctx, j, scores(ctx, j)))
                if len(pending) > _WIN_LOOKAHEAD:
                    softmax_pv(*pending.pop(0))
        for task in pending:
            softmax_pv(*task)
        return c

    lax.fori_loop(0, nblk // unroll, trip, 0)


def _windowed(qkv, bias, sink):
    b, s, width = qkv.shape
    blk = WINDOW_A
    nblk = s // blk
    assert s % blk == 0 and nblk >= 3
    kern = functools.partial(_win_kernel, seq=s, blk=blk,
                             unroll=math.gcd(nblk, _BLOCKS_PER_TRIP))
    return pl.pallas_call(
        kern,
        grid=(b,),
        in_specs=[
            pl.BlockSpec(memory_space=pltpu.SMEM),
            pl.BlockSpec((None, s, width), lambda i: (i, 0, 0)),
            pl.BlockSpec(bias.shape, lambda i: (0, 0, 0, 0)),
        ],
        out_specs=pl.BlockSpec((None, s, WIDTH_A), lambda i: (i, 0, 0)),
        out_shape=jax.ShapeDtypeStruct((b, s, WIDTH_A), BF16),
        compiler_params=pltpu.CompilerParams(
            dimension_semantics=("parallel",), vmem_limit_bytes=_VMEM_LIMIT),
        name="win_gqa",
    )(sink, qkv, bias)


def _dil_kernel(qkv_ref, bias_ref, o_ref, st_ref, *, ls, half, n_res, unroll):
    qb = 2 * half
    kw = 4 * half
    nblk = ls // qb
    lane = lax.broadcasted_iota(jnp.int32, (qb, LANES), 1)
    low = lane < HEAD_DIM

    n_pairs = N_HEADS_B // 2

    def scores(n, res_i, j):
        q0 = pl.multiple_of(n * qb, qb)
        k0 = pl.multiple_of(jnp.clip(q0 - half, 0, ls - kw), half)
        variant = jnp.where(n == 0, 0, jnp.where(n == nblk - 1, 2, 1))
        cols = slice(res_i * WIDTH_B + j * LANES, res_i * WIDTH_B + (j + 1) * LANES)
        q_col = 3 * res_i * WIDTH_B + j * LANES
        q_tile = qkv_ref[pl.ds(q0, qb), q_col:q_col + LANES]
        k_win = qkv_ref[pl.ds(k0, kw), q_col + WIDTH_B:q_col + WIDTH_B + LANES]
        logits = []
        for hf in range(2):
            qm = jnp.where(low if hf == 0 else ~low, q_tile, jnp.zeros_like(q_tile))
            logits.append(_nt_dot(qm, k_win) + bias_ref[variant, 2 * j + hf])
        return dict(q0=q0, k0=k0, cols=cols, v_col=q_col + 2 * WIDTH_B, res_i=res_i, j=j,
                    logits=logits)

    def softmax_pv(task, stats):
        j, cols = task["j"], task["cols"]
        v_win = qkv_ref[pl.ds(task["k0"], kw), task["v_col"]:task["v_col"] + LANES]
        v_aug = jnp.concatenate([v_win, jnp.ones_like(v_win)], axis=1)
        st_acc = jnp.zeros((qb, LANES), F32) if j == 0 else stats.pop()
        nums = []
        for hf, s in enumerate(task["logits"]):
            h = 2 * j + hf
            m = jnp.max(s, axis=-1, keepdims=True)
            o = jnp.dot(jnp.exp2(s - m).astype(BF16), v_aug, preferred_element_type=F32)
            nums.append(o[:, :LANES])
            st_acc = jnp.where(lane == h, m,
                               jnp.where(lane == N_HEADS_B + h, o[:, LANES:], st_acc))
        o_ref[pl.ds(task["q0"], qb), cols] = jnp.where(low, nums[0], nums[1]).astype(o_ref.dtype)
        if j == n_pairs - 1:
            r = task["res_i"]
            st_ref[pl.ds(task["q0"], qb), r * LANES:(r + 1) * LANES] = st_acc
        else:
            stats.append(st_acc)

    def trip(t, c):
        pending, stats = [], []
        for i in range(unroll):
            for res_i in range(n_res):
                for j in range(n_pairs):
                    pending.append(scores(t * unroll + i, res_i, j))
                    if len(pending) > _DIL_LOOKAHEAD:
                        softmax_pv(pending.pop(0), stats)
        for task in pending:
            softmax_pv(task, stats)
        return c

    lax.fori_loop(0, nblk // unroll, trip, 0)


def _dilated(qkv, bias, pattern, batch, dil, half):
    rows = qkv.shape[0]
    assert qkv.shape[1] == dil * 3 * WIDTH_B
    w_all = dil * WIDTH_B
    ls = rows // batch
    nblk = ls // (2 * half)
    assert ls % (2 * half) == 0 and nblk >= 2
    n_res = math.gcd(dil, max(1, _BLOCKS_PER_TRIP // nblk))
    unroll = math.gcd(nblk, max(1, _BLOCKS_PER_TRIP // n_res))
    kern = functools.partial(_dil_kernel, ls=ls, half=half, n_res=n_res, unroll=unroll)
    lanes = lambda width: pl.BlockSpec((None, ls, n_res * width), lambda i, r: (i, 0, r))
    n_var = 3
    o, st = pl.pallas_call(
        kern,
        grid=(batch, dil // n_res),
        in_specs=[lanes(3 * WIDTH_B),
                  pl.BlockSpec((n_var,) + bias.shape[1:], lambda i, r: (pattern, 0, 0, 0))],
        out_specs=[lanes(WIDTH_B), lanes(LANES)],
        out_shape=[jax.ShapeDtypeStruct((batch, ls, w_all), BF16),
                   jax.ShapeDtypeStruct((batch, ls, dil * LANES), F32)],
        compiler_params=pltpu.CompilerParams(
            dimension_semantics=("parallel", "parallel"), vmem_limit_bytes=_VMEM_LIMIT),
        name=f"dilated_d{dil}",
    )(qkv.reshape(batch, ls, dil * 3 * WIDTH_B), bias)
    return o.reshape(rows, w_all), st.reshape(rows, dil * LANES)


def _post_kernel(x_ref, p_ref, oa_ref, *refs, dils, ff_chunk, chain_rows):
    n_pat = len(dils)
    ob_refs = refs[:n_pat]
    st_refs = refs[n_pat:2 * n_pat]
    (expand_ref, goa_ref, gob_ref, wo_ref, gpm_ref, gpre_ref, wup_ref, wdn_ref,
     gmlp_ref, wpp_ref, wpg_ref, bpg_ref, gple_ref, out_ref) = refs[2 * n_pat:2 * n_pat + 14]
    scratch = refs[2 * n_pat + 14:]
    tm = x_ref.shape[0]
    assert sum(chain_rows) == tm
    n_slab = WIDTH_B // LANES

    def expand(w):
        head_lane = lax.broadcasted_iota(jnp.int32, w.shape, 1) < N_HEADS_B
        w = jnp.where(head_lane, w, 0.0)
        hi = w.astype(BF16)
        lo = (w - hi.astype(F32)).astype(BF16)
        return jnp.dot(jnp.concatenate([hi, lo], axis=1), expand_ref[...],
                       preferred_element_type=F32)

    def combine(st):
        rows = st["rows"]
        nums, stats = [], []
        si = 0
        for d, o_ref, s_ref in zip(dils, ob_refs, st_refs):
            if d == 1:
                nums.append(o_ref[rows, :].astype(F32))
                stats.append(s_ref[rows, :])
                continue
            o_scr, s_scr = scratch[si], scratch[si + 1]
            si += 2
            assert rows.start % d == 0 and rows.stop % d == 0
            src = slice(rows.start // d, rows.stop // d)
            for r in range(d):
                dst = pl.ds(rows.start + r, (rows.stop - rows.start) // d, stride=d)
                for c in range(n_slab):
                    lo = r * WIDTH_B + c * LANES
                    o_scr[c, dst, :] = o_ref[src, lo:lo + LANES].astype(F32)
                s_scr[dst, :] = s_ref[src, r * LANES:(r + 1) * LANES]
            nums.append(jnp.concatenate([o_scr[c, rows, :] for c in range(n_slab)], axis=1))
            stats.append(s_scr[rows, :])
        mx = functools.reduce(jnp.maximum, stats)
        scales = [jnp.exp2(s - mx) for s in stats]
        dens = [pltpu.roll(s, LANES - N_HEADS_B, 1) for s in stats]
        tot = functools.reduce(lambda a, b: a + b, [sc * dn for sc, dn in zip(scales, dens)])
        ob = None
        for sc, num in zip(scales, nums):
            term = expand(sc / tot) * num
            ob = term if ob is None else ob + term
        st["oa"] = _rms(oa_ref[rows, :].astype(F32), goa_ref[...]).astype(BF16)
        st["ob"] = _rms(ob, gob_ref[...]).astype(BF16)

    def mix(st):
        m = (jnp.dot(st.pop("oa"), wo_ref[:WIDTH_A, :], preferred_element_type=F32)
             + jnp.dot(st.pop("ob"), wo_ref[WIDTH_A:, :], preferred_element_type=F32))
        st["h"] = x_ref[st["rows"], :] + _rms(m, gpm_ref[...])
        st["v"] = _rms(st["h"], gpre_ref[...]).astype(BF16)
        st["ff"] = None

    def mlp_chunk(c):
        def stage(st):
            a = jnp.maximum(
                jnp.dot(st["v"], wup_ref[:, c:c + ff_chunk], preferred_element_type=F32), 0.0)
            t = jnp.dot((a * a).astype(BF16), wdn_ref[c:c + ff_chunk, :],
                        preferred_element_type=F32)
            st["ff"] = t if st["ff"] is None else st["ff"] + t
        return stage

    def ple(st):
        h = st["h"] + _rms(st.pop("ff"), gmlp_ref[...])
        gate = jax.nn.sigmoid(
            jnp.dot(h.astype(BF16), wpg_ref[...], preferred_element_type=F32) + bpg_ref[...])
        emb = jnp.dot(p_ref[st["rows"], :].astype(BF16), wpp_ref[...],
                      preferred_element_type=F32)
        out_ref[st["rows"], :] = h + _rms(gate * emb, gple_ref[...])

    stages = [combine, mix] + [mlp_chunk(c) for c in range(0, wup_ref.shape[1], ff_chunk)] + [ple]
    starts = np.cumsum((0,) + chain_rows)
    states = [{"rows": slice(int(lo), int(hi))} for lo, hi in zip(starts[:-1], starts[1:])]
    for stage in stages:
        for st in states:
            stage(st)


def _post(x2d, p2d, oa, obs, stats, dils, consts, tm, ff_chunk):
    n, d_model = x2d.shape
    row = lambda width: pl.BlockSpec((tm, width), lambda i: (i, 0))
    const = lambda a: pl.BlockSpec(a.shape, lambda i: (0,) * a.ndim,
                                   pipeline_mode=pl.Buffered(1))
    scratch = []
    for d in dils:
        if d > 1:
            scratch += [pltpu.VMEM((WIDTH_B // LANES, tm, LANES), F32),
                        pltpu.VMEM((tm, LANES), F32)]
    return pl.pallas_call(
        functools.partial(_post_kernel, dils=dils, ff_chunk=ff_chunk,
                          chain_rows=tuple(tm * f // sum(_ROW_CHAINS) for f in _ROW_CHAINS)),
        grid=(n // tm,),
        in_specs=[row(d_model), row(p2d.shape[1]), row(WIDTH_A)]
                 + [pl.BlockSpec((tm // d, d * WIDTH_B), lambda i: (i, 0)) for d in dils]
                 + [pl.BlockSpec((tm // d, d * LANES), lambda i: (i, 0)) for d in dils]
                 + [const(a) for a in consts],
        out_specs=row(d_model),
        out_shape=jax.ShapeDtypeStruct((n, d_model), x2d.dtype),
        scratch_shapes=scratch,
        compiler_params=pltpu.CompilerParams(
            dimension_semantics=("parallel",), vmem_limit_bytes=_VMEM_LIMIT),
        name="post",
    )(x2d, p2d, oa, *obs, *stats, *consts)


def kernel(x, p, rel_bias_table, g_pre_mix, w_in, sink_a, g_out_a, g_out_b, w_o, g_post_mix,
           g_pre_mlp, w_up, w_down, g_post_mlp, w_ple_proj, w_ple_gate, b_ple_gate, g_post_ple):
    b, s, d_model = x.shape
    depth = w_in.shape[0]
    n = b * s
    tm = _TOKEN_TILE
    assert n % tm == 0 and s % tm == 0
    dils = tuple(dil for _, dil in DILATED_PATTERNS)

    def pair_heads(a, axis):
        shape = a.shape
        a = a.reshape(shape[:axis] + (N_KV_A, GROUP_A, HEAD_DIM) + shape[axis + 1:])
        return jnp.swapaxes(a, axis, axis + 1).reshape(shape)

    idx_a, shifts_a = _band_layout(WINDOW_A, 3 * WINDOW_A, (0, WINDOW_A, 2 * WINDOW_A),
                                   1, WINDOW_A)
    half = DILATED_PATTERNS[0][0] // (2 * DILATED_PATTERNS[0][1])
    idx_b = []
    for window, dil in DILATED_PATTERNS:
        assert window // (2 * dil) == half
        idx, shifts_b = _band_layout(2 * half, 4 * half, (0, half, 2 * half), dil, half)
        idx_b.append(idx)
    bias_a, bias_b = _bias(
        rel_bias_table.T.astype(F32) * LOG2E,
        jnp.asarray(idx_a)[None, None], shifts_a,
        (len(shifts_a), N_HEADS_A, WINDOW_A, 3 * WINDOW_A),
        jnp.asarray(np.stack(idx_b))[:, None], shifts_b,
        (len(DILATED_PATTERNS) * len(shifts_b), N_HEADS_B, 2 * half, 4 * half))

    expand_np = np.zeros((2 * LANES, WIDTH_B), np.float32)
    for hh in range(N_HEADS_B):
        expand_np[hh, hh * HEAD_DIM:(hh + 1) * HEAD_DIM] = 1.0
        expand_np[LANES + hh, hh * HEAD_DIM:(hh + 1) * HEAD_DIM] = 1.0
    expand = jnp.asarray(expand_np, BF16)

    h2d = x.reshape(n, d_model)
    row = lambda a: a.reshape(1, -1).astype(F32)
    for i in range(depth):
        scale = HEAD_DIM ** -0.5 * LOG2E
        w = w_in[i]
        o0 = WIDTH_A + 2 * KV_WIDTH_A
        w_all = jnp.concatenate([
            pair_heads(w[:, :WIDTH_A], 1) * scale,
            w[:, WIDTH_A:o0],
            w[:, o0:o0 + WIDTH_B] * scale,
            w[:, o0 + WIDTH_B:],
        ], axis=1).astype(BF16)
        outs = _inproj(h2d, row(g_pre_mix[i]), w_all, tm, dils)
        o_a = _windowed(outs[0].reshape(b, s, -1), bias_a, sink_a[i].astype(F32) * LOG2E)
        obs, stats = [], []
        for pi, (window, dil) in enumerate(DILATED_PATTERNS):
            o, st = _dilated(outs[1 + pi], bias_b, pi, b, dil, window // (2 * dil))
            obs.append(o)
            stats.append(st)

        wo = jnp.concatenate([pair_heads(w_o[i][:WIDTH_A], 0), w_o[i][WIDTH_A:]],
                             axis=0).astype(BF16)
        consts = (expand, row(pair_heads(g_out_a[i], 0)), row(g_out_b[i]), wo, row(g_post_mix[i]),
                  row(g_pre_mlp[i]), w_up[i].astype(BF16), w_down[i].astype(BF16),
                  row(g_post_mlp[i]), w_ple_proj[i].astype(BF16), w_ple_gate[i].astype(BF16),
                  row(b_ple_gate[i]), row(g_post_ple[i]))
        h2d = _post(h2d, p[i].reshape(n, -1), o_a.reshape(n, WIDTH_A), obs, stats, dils,
                    consts, tm, _FF_CHUNK)
    return h2d.reshape(b, s, d_model)
```

```python
import functools
import math

import jax
import jax.numpy as jnp
import numpy as np
from jax import lax
from jax.experimental import pallas as pl
from jax.experimental.pallas import tpu as pltpu

HEAD_DIM = 64
N_HEADS_A = 8
N_KV_A = 2
GROUP_A = N_HEADS_A // N_KV_A
WINDOW_A = 128
N_HEADS_B = 8
DILATED_PATTERNS = ((128, 1), (512, 4), (2048, 16))
WIDTH_A = N_HEADS_A * HEAD_DIM
WIDTH_B = N_HEADS_B * HEAD_DIM
KV_WIDTH_A = N_KV_A * HEAD_DIM
NUM_BUCKETS = 32
MAX_DISTANCE = 1024
EPS = 1e-6
NEG = -1e30
LOG2E = 1.4426950408889634

LANES = 128
F32 = jnp.float32
BF16 = jnp.bfloat16

_VMEM_LIMIT = 56 * 1024 * 1024
_TOKEN_TILE = 512
_FF_CHUNK = 2048
_OUT_RING = 3
_BLOCKS_PER_TRIP = 16
_WIN_LOOKAHEAD = 2
_DIL_LOOKAHEAD = 4
_ROW_CHAINS = (1, 1)


def _rms(x, g):
    ms = jnp.mean(x * x, axis=-1, keepdims=True)
    return (x * lax.rsqrt(ms + EPS)) * g


def _t5_bucket_np(rel):
    half = NUM_BUCKETS // 2
    max_exact = half // 2
    sign = np.where(rel > 0, half, 0)
    n = np.abs(rel)
    nf = np.maximum(n, 1).astype(np.float32)
    large = max_exact + (np.log(nf / np.float32(max_exact))
                         / np.float32(math.log(MAX_DISTANCE / max_exact))
                         * np.float32(half - max_exact)).astype(np.int32)
    large = np.minimum(large, half - 1)
    return (sign + np.where(n < max_exact, n, large)).astype(np.int32)


def _band_layout(q_len, k_len, offs, dil, half_window):
    center = q_len - 1 + max(offs)
    length = -(-(center + k_len) // LANES) * LANES
    rel = np.arange(length) - center
    idx = np.where(np.abs(rel) <= half_window, _t5_bucket_np(rel * dil), -1).astype(np.int32)
    return idx, tuple((off - center) % length for off in offs)


def _bias_kernel(tab_ref, idxa_ref, idxb_ref, outa_ref, outb_ref, *, shifts_a, shifts_b):
    def build(idx_ref, out_ref, head0, shifts):
        n_pat, _, length = idx_ref.shape
        _, n_heads, q_len, k_len = out_ref.shape
        for p in range(n_pat):
            idx = idx_ref[p]
            vec = jnp.full((n_heads, length), NEG, F32)
            for b in range(NUM_BUCKETS):
                vec = jnp.where(idx == b, tab_ref[head0:head0 + n_heads, b:b + 1], vec)
            for h in range(n_heads):
                rows = jnp.broadcast_to(vec[h:h + 1, :], (q_len, length))
                for v, s in enumerate(shifts):
                    band = pltpu.roll(rows, s, 1, stride=1, stride_axis=0)
                    out_ref[p * len(shifts) + v, h] = band[:, :k_len]

    build(idxa_ref, outa_ref, 0, shifts_a)
    build(idxb_ref, outb_ref, N_HEADS_A, shifts_b)


def _bias(table_t, idx_a, shifts_a, shape_a, idx_b, shifts_b, shape_b):
    vmem = pl.BlockSpec(memory_space=pltpu.VMEM)
    return pl.pallas_call(
        functools.partial(_bias_kernel, shifts_a=shifts_a, shifts_b=shifts_b),
        in_specs=[vmem, vmem, vmem],
        out_specs=[vmem, vmem],
        out_shape=[jax.ShapeDtypeStruct(shape_a, F32), jax.ShapeDtypeStruct(shape_b, F32)],
        compiler_params=pltpu.CompilerParams(vmem_limit_bytes=_VMEM_LIMIT),
        name="bias",
    )(table_t, idx_a, idx_b)


def _inproj_kernel(x_ref, g_ref, w_ref, *refs, dils, stage_dils):
    n_out = 1 + len(dils)
    out_hbm, rings, sem = refs[:n_out], refs[n_out:2 * n_out], refs[2 * n_out]
    scr = refs[2 * n_out + 1]
    stage_scr = dict(zip(stage_dils, refs[2 * n_out + 2:]))
    tm = x_ref.shape[0]
    n_slab = WIDTH_B // LANES
    step, n_steps = pl.program_id(0), pl.num_programs(0)
    slot = step % _OUT_RING

    def write_back(k, at_step, at_slot):
        rows = rings[k].shape[1]
        return pltpu.make_async_copy(
            rings[k].at[at_slot],
            out_hbm[k].at[pl.ds(pl.multiple_of(at_step * rows, rows), rows), :],
            sem.at[k, at_slot])

    @pl.when(step >= _OUT_RING)
    def _():
        for k in range(n_out):
            write_back(k, step - _OUT_RING, slot).wait()

    win_ref, d_refs = rings[0].at[slot], [r.at[slot] for r in rings[1:]]

    x = x_ref[...]
    u = (x * g_ref[...]).astype(BF16)
    rstd = lax.rsqrt(jnp.mean(x * x, axis=-1, keepdims=True) + EPS)

    col = win_ref.shape[-1]
    for t in range(3):
        seg = rstd * jnp.dot(u, w_ref[:, col:col + WIDTH_B], preferred_element_type=F32)
        col += WIDTH_B
        if any(d > 1 for d in dils):
            for c in range(n_slab):
                scr[t, c] = seg[:, c * LANES:(c + 1) * LANES]
        for d, o_ref in zip(dils, d_refs):
            if d == 1:
                o_ref[:, t * WIDTH_B:(t + 1) * WIDTH_B] = seg.astype(BF16)
                continue
            base = max([b for b in stage_dils if b < d and d % b == 0], default=1)
            stride = d // base
            for r in range(d):
                for c in range(n_slab):
                    rows = pl.ds(r // base, tm // d, stride=stride)
                    src = scr.at[t, c] if base == 1 else stage_scr[base].at[t, r % base, c]
                    val = src[rows, :]
                    if d in stage_scr:
                        stage_scr[d][t, r, c] = val
                    lo = (3 * r + t) * WIDTH_B + c * LANES
                    o_ref[:, lo:lo + LANES] = val.astype(BF16)
    win_ref[...] = (rstd * jnp.dot(u, w_ref[:, :win_ref.shape[-1]],
                                   preferred_element_type=F32)).astype(BF16)

    for k in range(n_out):
        write_back(k, step, slot).start()

    @pl.when(step == n_steps - 1)
    def _():
        for back in range(_OUT_RING):
            @pl.when(step >= back)
            def _():
                for k in range(n_out):
                    write_back(k, step - back, (step - back) % _OUT_RING).wait()


def _inproj(x2d, g, w, tm, dils):
    n, d_model = x2d.shape
    shapes = [(n, WIDTH_A + 2 * KV_WIDTH_A)]
    blocks = [(tm, WIDTH_A + 2 * KV_WIDTH_A)]
    for d in dils:
        assert tm % (16 * d) == 0 and n % d == 0
        shapes.append((n // d, d * 3 * WIDTH_B))
        blocks.append((tm // d, d * 3 * WIDTH_B))
    stage_dils = tuple(b for b in dils if b > 1 and any(d > b and d % b == 0 for d in dils))
    n_slab = WIDTH_B // LANES
    return pl.pallas_call(
        functools.partial(_inproj_kernel, dils=dils, stage_dils=stage_dils),
        grid=(n // tm,),
        in_specs=[
            pl.BlockSpec((tm, d_model), lambda i: (i, 0)),
            pl.BlockSpec((1, d_model), lambda i: (0, 0)),
            pl.BlockSpec(w.shape, lambda i: (0, 0)),
        ],
        out_specs=[pl.BlockSpec(memory_space=pl.ANY) for _ in blocks],
        out_shape=[jax.ShapeDtypeStruct(sh, BF16) for sh in shapes],
        scratch_shapes=[pltpu.VMEM((_OUT_RING,) + bs, BF16) for bs in blocks]
                       + [pltpu.SemaphoreType.DMA((len(blocks), _OUT_RING))]
                       + [pltpu.VMEM((3, n_slab, tm, LANES), F32)]
                       + [pltpu.VMEM((3, b, n_slab, tm // b, LANES), F32) for b in stage_dils],
        compiler_params=pltpu.CompilerParams(
            dimension_semantics=("arbitrary",), vmem_limit_bytes=_VMEM_LIMIT),
        name="inproj",
    )(x2d, g, w)


def _nt_dot(a, b):
    return lax.dot_general(a, b, (((1,), (1,)), ((), ())), preferred_element_type=F32)


def _win_kernel(sink_ref, qkv_ref, bias_ref, o_ref, *, seq, blk, unroll):
    q_ref = qkv_ref.at[:, :WIDTH_A]
    k_ref = qkv_ref.at[:, WIDTH_A:WIDTH_A + KV_WIDTH_A]
    v_ref = qkv_ref.at[:, WIDTH_A + KV_WIDTH_A:]
    nblk = seq // blk
    lane = lax.broadcasted_iota(jnp.int32, (blk, LANES), 1)
    low = lane < HEAD_DIM

    klen = 3 * blk

    def block_context(n):
        q0 = pl.multiple_of(n * blk, blk)
        k0 = pl.multiple_of(jnp.clip(q0 - blk, 0, seq - klen), blk)
        variant = jnp.where(n == 0, 0, jnp.where(n == nblk - 1, 2, 1))
        k1 = k_ref[pl.ds(k0, 2 * blk), :]
        k2 = k_ref[pl.ds(k0 + 2 * blk, blk), :]
        v1 = v_ref[pl.ds(k0, 2 * blk), :]
        v2 = v_ref[pl.ds(k0 + 2 * blk, blk), :]
        zero = jnp.zeros_like(k2)
        one_low = low.astype(F32).astype(BF16)
        one_high = (~low).astype(F32).astype(BF16)
        return dict(
            q0=q0, variant=variant, k1=k1,
            v1_aug=jnp.concatenate([v1, jnp.ones_like(v1)], axis=1),
            k2_pack=jnp.concatenate([jnp.where(low, k2, zero), jnp.where(low, zero, k2)], axis=0),
            v2_pack=jnp.concatenate(
                [jnp.concatenate([jnp.where(low, v2, zero), one_low], axis=1),
                 jnp.concatenate([jnp.where(low, zero, v2), one_high], axis=1)], axis=0))

    def scores(ctx, j):
        q_tile = q_ref[pl.ds(ctx["q0"], blk), j * LANES:(j + 1) * LANES]
        s2 = _nt_dot(q_tile, ctx["k2_pack"])
        out = []
        for hf in range(N_KV_A):
            qm = jnp.where(low if hf == 0 else ~low, q_tile, jnp.zeros_like(q_tile))
            out.append(jnp.concatenate(
                [_nt_dot(qm, ctx["k1"]), s2[:, hf * LANES:(hf + 1) * LANES]], axis=1)
                + bias_ref[ctx["variant"], hf * GROUP_A + j])
        return out

    def softmax_pv(ctx, j, logits):
        nums, dens, e2s = [], [], []
        for hf, s in enumerate(logits):
            sink = sink_ref[hf * GROUP_A + j]
            m = jnp.maximum(jnp.max(s, axis=-1, keepdims=True), sink)
            e = jnp.exp2(s - m).astype(BF16)
            o1 = jnp.dot(e[:, :2 * blk], ctx["v1_aug"], preferred_element_type=F32)
            nums.append(o1[:, :LANES])
            dens.append(o1[:, LANES:] + jnp.exp2(sink - m))
            e2s.append(e[:, 2 * blk:])
        o2 = jnp.dot(jnp.concatenate(e2s, axis=1), ctx["v2_pack"], preferred_element_type=F32)
        num = jnp.where(low, nums[0], nums[1]) + o2[:, :LANES]
        den = jnp.where(low, dens[0], dens[1]) + o2[:, LANES:]
        o_ref[pl.ds(ctx["q0"], blk), j * LANES:(j + 1) * LANES] = (num / den).astype(o_ref.dtype)

    def trip(t, c):
        pending = []
        for i in range(unroll):
            ctx = block_context(t * unroll + i)
            for j in range(GROUP_A):
                pending.append((ctx, j, scores(ctx, j)))
                if len(pending) > _WIN_LOOKAHEAD:
                    softmax_pv(*pending.pop(0))
        for task in pending:
            softmax_pv(*task)
        return c

    lax.fori_loop(0, nblk // unroll, trip, 0)


def _windowed(qkv, bias, sink):
    b, s, width = qkv.shape
    blk = WINDOW_A
    nblk = s // blk
    assert s % blk == 0 and nblk >= 3
    kern = functools.partial(_win_kernel, seq=s, blk=blk,
                             unroll=math.gcd(nblk, _BLOCKS_PER_TRIP))
    return pl.pallas_call(
        kern,
        grid=(b,),
        in_specs=[
            pl.BlockSpec(memory_space=pltpu.SMEM),
            pl.BlockSpec((None, s, width), lambda i: (i, 0, 0)),
            pl.BlockSpec(bias.shape, lambda i: (0, 0, 0, 0)),
        ],
        out_specs=pl.BlockSpec((None, s, WIDTH_A), lambda i: (i, 0, 0)),
        out_shape=jax.ShapeDtypeStruct((b, s, WIDTH_A), BF16),
        compiler_params=pltpu.CompilerParams(
            dimension_semantics=("parallel",), vmem_limit_bytes=_VMEM_LIMIT),
        name="win_gqa",
    )(sink, qkv, bias)


def _dil_kernel(qkv_ref, bias_ref, o_ref, st_ref, *, ls, half, n_res, unroll):
    qb = 2 * half
    kw = 4 * half
    nblk = ls // qb
    lane = lax.broadcasted_iota(jnp.int32, (qb, LANES), 1)
    low = lane < HEAD_DIM

    n_pairs = N_HEADS_B // 2

    def scores(n, res_i, j):
        q0 = pl.multiple_of(n * qb, qb)
        k0 = pl.multiple_of(jnp.clip(q0 - half, 0, ls - kw), half)
        variant = jnp.where(n == 0, 0, jnp.where(n == nblk - 1, 2, 1))
        cols = slice(res_i * WIDTH_B + j * LANES, res_i * WIDTH_B + (j + 1) * LANES)
        q_col = 3 * res_i * WIDTH_B + j * LANES
        q_tile = qkv_ref[pl.ds(q0, qb), q_col:q_col + LANES]
        k_win = qkv_ref[pl.ds(k0, kw), q_col + WIDTH_B:q_col + WIDTH_B + LANES]
        logits = []
        for hf in range(2):
            qm = jnp.where(low if hf == 0 else ~low, q_tile, jnp.zeros_like(q_tile))
            logits.append(_nt_dot(qm, k_win) + bias_ref[variant, 2 * j + hf])
        return dict(q0=q0, k0=k0, cols=cols, v_col=q_col + 2 * WIDTH_B, res_i=res_i, j=j,
                    logits=logits)

    def softmax_pv(task, stats):
        j, cols = task["j"], task["cols"]
        v_win = qkv_ref[pl.ds(task["k0"], kw), task["v_col"]:task["v_col"] + LANES]
        v_aug = jnp.concatenate([v_win, jnp.ones_like(v_win)], axis=1)
        st_acc = jnp.zeros((qb, LANES), F32) if j == 0 else stats.pop()
        nums = []
        for hf, s in enumerate(task["logits"]):
            h = 2 * j + hf
            m = jnp.max(s, axis=-1, keepdims=True)
            o = jnp.dot(jnp.exp2(s - m).astype(BF16), v_aug, preferred_element_type=F32)
            nums.append(o[:, :LANES])
            st_acc = jnp.where(lane == h, m,
                               jnp.where(lane == N_HEADS_B + h, o[:, LANES:], st_acc))
        o_ref[pl.ds(task["q0"], qb), cols] = jnp.where(low, nums[0], nums[1]).astype(o_ref.dtype)
        if j == n_pairs - 1:
            r = task["res_i"]
            st_ref[pl.ds(task["q0"], qb), r * LANES:(r + 1) * LANES] = st_acc
        else:
            stats.append(st_acc)

    def trip(t, c):
        pending, stats = [], []
        for i in range(unroll):
            for res_i in range(n_res):
                for j in range(n_pairs):
                    pending.append(scores(t * unroll + i, res_i, j))
                    if len(pending) > _DIL_LOOKAHEAD:
                        softmax_pv(pending.pop(0), stats)
        for task in pending:
            softmax_pv(task, stats)
        return c

    lax.fori_loop(0, nblk // unroll, trip, 0)


def _dilated(qkv, bias, pattern, batch, dil, half):
    rows = qkv.shape[0]
    assert qkv.shape[1] == dil * 3 * WIDTH_B
    w_all = dil * WIDTH_B
    ls = rows // batch
    nblk = ls // (2 * half)
    assert ls % (2 * half) == 0 and nblk >= 2
    n_res = math.gcd(dil, max(1, _BLOCKS_PER_TRIP // nblk))
    unroll = math.gcd(nblk, max(1, _BLOCKS_PER_TRIP // n_res))
    kern = functools.partial(_dil_kernel, ls=ls, half=half, n_res=n_res, unroll=unroll)
    lanes = lambda width: pl.BlockSpec((None, ls, n_res * width), lambda i, r: (i, 0, r))
    n_var = 3
    o, st = pl.pallas_call(
        kern,
        grid=(batch, dil // n_res),
        in_specs=[lanes(3 * WIDTH_B),
                  pl.BlockSpec((n_var,) + bias.shape[1:], lambda i, r: (pattern, 0, 0, 0))],
        out_specs=[lanes(WIDTH_B), lanes(LANES)],
        out_shape=[jax.ShapeDtypeStruct((batch, ls, w_all), BF16),
                   jax.ShapeDtypeStruct((batch, ls, dil * LANES), F32)],
        compiler_params=pltpu.CompilerParams(
            dimension_semantics=("parallel", "parallel"), vmem_limit_bytes=_VMEM_LIMIT),
        name=f"dilated_d{dil}",
    )(qkv.reshape(batch, ls, dil * 3 * WIDTH_B), bias)
    return o.reshape(rows, w_all), st.reshape(rows, dil * LANES)


def _post_kernel(x_ref, p_ref, oa_ref, *refs, dils, ff_chunk, chain_rows):
    n_pat = len(dils)
    ob_refs = refs[:n_pat]
    st_refs = refs[n_pat:2 * n_pat]
    (expand_ref, goa_ref, gob_ref, wo_ref, gpm_ref, gpre_ref, wup_ref, wdn_ref,
     gmlp_ref, wpp_ref, wpg_ref, bpg_ref, gple_ref, out_ref) = refs[2 * n_pat:2 * n_pat + 14]
    scratch = refs[2 * n_pat + 14:]
    tm = x_ref.shape[0]
    assert sum(chain_rows) == tm
    n_slab = WIDTH_B // LANES

    def expand(w):
        head_lane = lax.broadcasted_iota(jnp.int32, w.shape, 1) < N_HEADS_B
        w = jnp.where(head_lane, w, 0.0)
        hi = w.astype(BF16)
        lo = (w - hi.astype(F32)).astype(BF16)
        return jnp.dot(jnp.concatenate([hi, lo], axis=1), expand_ref[...],
                       preferred_element_type=F32)

    def combine(st):
        rows = st["rows"]
        nums, stats = [], []
        si = 0
        for d, o_ref, s_ref in zip(dils, ob_refs, st_refs):
            if d == 1:
                nums.append(o_ref[rows, :].astype(F32))
                stats.append(s_ref[rows, :])
                continue
            o_scr, s_scr = scratch[si], scratch[si + 1]
            si += 2
            assert rows.start % d == 0 and rows.stop % d == 0
            src = slice(rows.start // d, rows.stop // d)
            for r in range(d):
                dst = pl.ds(rows.start + r, (rows.stop - rows.start) // d, stride=d)
                for c in range(n_slab):
                    lo = r * WIDTH_B + c * LANES
                    o_scr[c, dst, :] = o_ref[src, lo:lo + LANES].astype(F32)
                s_scr[dst, :] = s_ref[src, r * LANES:(r + 1) * LANES]
            nums.append(jnp.concatenate([o_scr[c, rows, :] for c in range(n_slab)], axis=1))
            stats.append(s_scr[rows, :])
        mx = functools.reduce(jnp.maximum, stats)
        scales = [jnp.exp2(s - mx) for s in stats]
        dens = [pltpu.roll(s, LANES - N_HEADS_B, 1) for s in stats]
        tot = functools.reduce(lambda a, b: a + b, [sc * dn for sc, dn in zip(scales, dens)])
        ob = None
        for sc, num in zip(scales, nums):
            term = expand(sc / tot) * num
            ob = term if ob is None else ob + term
        st["oa"] = _rms(oa_ref[rows, :].astype(F32), goa_ref[...]).astype(BF16)
        st["ob"] = _rms(ob, gob_ref[...]).astype(BF16)

    def mix(st):
        m = (jnp.dot(st.pop("oa"), wo_ref[:WIDTH_A, :], preferred_element_type=F32)
             + jnp.dot(st.pop("ob"), wo_ref[WIDTH_A:, :], preferred_element_type=F32))
        st["h"] = x_ref[st["rows"], :] + _rms(m, gpm_ref[...])
        st["v"] = _rms(st["h"], gpre_ref[...]).astype(BF16)
        st["ff"] = None

    def mlp_chunk(c):
        def stage(st):
            a = jnp.maximum(
                jnp.dot(st["v"], wup_ref[:, c:c + ff_chunk], preferred_element_type=F32), 0.0)
            t = jnp.dot((a * a).astype(BF16), wdn_ref[c:c + ff_chunk, :],
                        preferred_element_type=F32)
            st["ff"] = t if st["ff"] is None else st["ff"] + t
        return stage

    def ple(st):
        h = st["h"] + _rms(st.pop("ff"), gmlp_ref[...])
        gate = jax.nn.sigmoid(
            jnp.dot(h.astype(BF16), wpg_ref[...], preferred_element_type=F32) + bpg_ref[...])
        emb = jnp.dot(p_ref[st["rows"], :].astype(BF16), wpp_ref[...],
                      preferred_element_type=F32)
        out_ref[st["rows"], :] = h + _rms(gate * emb, gple_ref[...])

    stages = [combine, mix] + [mlp_chunk(c) for c in range(0, wup_ref.shape[1], ff_chunk)] + [ple]
    starts = np.cumsum((0,) + chain_rows)
    states = [{"rows": slice(int(lo), int(hi))} for lo, hi in zip(starts[:-1], starts[1:])]
    for stage in stages:
        for st in states:
            stage(st)


def _post(x2d, p2d, oa, obs, stats, dils, consts, tm, ff_chunk):
    n, d_model = x2d.shape
    row = lambda width: pl.BlockSpec((tm, width), lambda i: (i, 0))
    const = lambda a: pl.BlockSpec(a.shape, lambda i: (0,) * a.ndim,
                                   pipeline_mode=pl.Buffered(1))
    scratch = []
    for d in dils:
        if d > 1:
            scratch += [pltpu.VMEM((WIDTH_B // LANES, tm, LANES), F32),
                        pltpu.VMEM((tm, LANES), F32)]
    return pl.pallas_call(
        functools.partial(_post_kernel, dils=dils, ff_chunk=ff_chunk,
                          chain_rows=tuple(tm * f // sum(_ROW_CHAINS) for f in _ROW_CHAINS)),
        grid=(n // tm,),
        in_specs=[row(d_model), row(p2d.shape[1]), row(WIDTH_A)]
                 + [pl.BlockSpec((tm // d, d * WIDTH_B), lambda i: (i, 0)) for d in dils]
                 + [pl.BlockSpec((tm // d, d * LANES), lambda i: (i, 0)) for d in dils]
                 + [const(a) for a in consts],
        out_specs=row(d_model),
        out_shape=jax.ShapeDtypeStruct((n, d_model), x2d.dtype),
        scratch_shapes=scratch,
        compiler_params=pltpu.CompilerParams(
            dimension_semantics=("parallel",), vmem_limit_bytes=_VMEM_LIMIT),
        name="post",
    )(x2d, p2d, oa, *obs, *stats, *consts)


def kernel(x, p, rel_bias_table, g_pre_mix, w_in, sink_a, g_out_a, g_out_b, w_o, g_post_mix,
           g_pre_mlp, w_up, w_down, g_post_mlp, w_ple_proj, w_ple_gate, b_ple_gate, g_post_ple):
    b, s, d_model = x.shape
    depth = w_in.shape[0]
    n = b * s
    tm = _TOKEN_TILE
    assert n % tm == 0 and s % tm == 0
    dils = tuple(dil for _, dil in DILATED_PATTERNS)

    def pair_heads(a, axis):
        shape = a.shape
        a = a.reshape(shape[:axis] + (N_KV_A, GROUP_A, HEAD_DIM) + shape[axis + 1:])
        return jnp.swapaxes(a, axis, axis + 1).reshape(shape)

    idx_a, shifts_a = _band_layout(WINDOW_A, 3 * WINDOW_A, (0, WINDOW_A, 2 * WINDOW_A),
                                   1, WINDOW_A)
    half = DILATED_PATTERNS[0][0] // (2 * DILATED_PATTERNS[0][1])
    idx_b = []
    for window, dil in DILATED_PATTERNS:
        assert window // (2 * dil) == half
        idx, shifts_b = _band_layout(2 * half, 4 * half, (0, half, 2 * half), dil, half)
        idx_b.append(idx)
    bias_a, bias_b = _bias(
        rel_bias_table.T.astype(F32) * LOG2E,
        jnp.asarray(idx_a)[None, None], shifts_a,
        (len(shifts_a), N_HEADS_A, WINDOW_A, 3 * WINDOW_A),
        jnp.asarray(np.stack(idx_b))[:, None], shifts_b,
        (len(DILATED_PATTERNS) * len(shifts_b), N_HEADS_B, 2 * half, 4 * half))

    expand_np = np.zeros((2 * LANES, WIDTH_B), np.float32)
    for hh in range(N_HEADS_B):
        expand_np[hh, hh * HEAD_DIM:(hh + 1) * HEAD_DIM] = 1.0
        expand_np[LANES + hh, hh * HEAD_DIM:(hh + 1) * HEAD_DIM] = 1.0
    expand = jnp.asarray(expand_np, BF16)

    h2d = x.reshape(n, d_model)
    row = lambda a: a.reshape(1, -1).astype(F32)
    for i in range(depth):
        scale = HEAD_DIM ** -0.5 * LOG2E
        w = w_in[i]
        o0 = WIDTH_A + 2 * KV_WIDTH_A
        w_all = jnp.concatenate([
            pair_heads(w[:, :WIDTH_A], 1) * scale,
            w[:, WIDTH_A:o0],
            w[:, o0:o0 + WIDTH_B] * scale,
            w[:, o0 + WIDTH_B:],
        ], axis=1).astype(BF16)
        outs = _inproj(h2d, row(g_pre_mix[i]), w_all, tm, dils)
        o_a = _windowed(outs[0].reshape(b, s, -1), bias_a, sink_a[i].astype(F32) * LOG2E)
        obs, stats = [], []
        for pi, (window, dil) in enumerate(DILATED_PATTERNS):
            o, st = _dilated(outs[1 + pi], bias_b, pi, b, dil, window // (2 * dil))
            obs.append(o)
            stats.append(st)

        wo = jnp.concatenate([pair_heads(w_o[i][:WIDTH_A], 0), w_o[i][WIDTH_A:]],
                             axis=0).astype(BF16)
        consts = (expand, row(pair_heads(g_out_a[i], 0)), row(g_out_b[i]), wo, row(g_post_mix[i]),
                  row(g_pre_mlp[i]), w_up[i].astype(BF16), w_down[i].astype(BF16),
                  row(g_post_mlp[i]), w_ple_proj[i].astype(BF16), w_ple_gate[i].astype(BF16),
                  row(b_ple_gate[i]), row(g_post_ple[i]))
        h2d = _post(h2d, p[i].reshape(n, -1), o_a.reshape(n, WIDTH_A), obs, stats, dils,
                    consts, tm, _FF_CHUNK)
    return h2d.reshape(b, s, d_model)
```

```python
import functools
import math

import jax
import jax.numpy as jnp
import numpy as np
from jax import lax
from jax.experimental import pallas as pl
from jax.experimental.pallas import tpu as pltpu

HEAD_DIM = 64
N_HEADS_A = 8
N_KV_A = 2
GROUP_A = N_HEADS_A // N_KV_A
WINDOW_A = 128
N_HEADS_B = 8
DILATED_PATTERNS = ((128, 1), (512, 4), (2048, 16))
WIDTH_A = N_HEADS_A * HEAD_DIM
WIDTH_B = N_HEADS_B * HEAD_DIM
KV_WIDTH_A = N_KV_A * HEAD_DIM
NUM_BUCKETS = 32
MAX_DISTANCE = 1024
EPS = 1e-6
NEG = -1e30
LOG2E = 1.4426950408889634

LANES = 128
F32 = jnp.float32
BF16 = jnp.bfloat16

_VMEM_LIMIT = 56 * 1024 * 1024
_TOKEN_TILE = 512
_FF_CHUNK = 2048
_BLOCKS_PER_TRIP = 16
_WIN_LOOKAHEAD = 2
_DIL_LOOKAHEAD = 4
_ROW_CHAINS = (1, 1)


def _rms(x, g):
    ms = jnp.mean(x * x, axis=-1, keepdims=True)
    return (x * lax.rsqrt(ms + EPS)) * g


def _t5_bucket_np(rel):
    half = NUM_BUCKETS // 2
    max_exact = half // 2
    sign = np.where(rel > 0, half, 0)
    n = np.abs(rel)
    nf = np.maximum(n, 1).astype(np.float32)
    large = max_exact + (np.log(nf / np.float32(max_exact))
                         / np.float32(math.log(MAX_DISTANCE / max_exact))
                         * np.float32(half - max_exact)).astype(np.int32)
    large = np.minimum(large, half - 1)
    return (sign + np.where(n < max_exact, n, large)).astype(np.int32)


def _band_layout(q_len, k_len, offs, dil, half_window):
    center = q_len - 1 + max(offs)
    length = -(-(center + k_len) // LANES) * LANES
    rel = np.arange(length) - center
    idx = np.where(np.abs(rel) <= half_window, _t5_bucket_np(rel * dil), -1).astype(np.int32)
    return idx, tuple((off - center) % length for off in offs)


def _bias_kernel(tab_ref, idxa_ref, idxb_ref, outa_ref, outb_ref, *, shifts_a, shifts_b):
    def build(idx_ref, out_ref, head0, shifts):
        n_pat, _, length = idx_ref.shape
        _, n_heads, q_len, k_len = out_ref.shape
        for p in range(n_pat):
            idx = idx_ref[p]
            vec = jnp.full((n_heads, length), NEG, F32)
            for b in range(NUM_BUCKETS):
                vec = jnp.where(idx == b, tab_ref[head0:head0 + n_heads, b:b + 1], vec)
            for h in range(n_heads):
                rows = jnp.broadcast_to(vec[h:h + 1, :], (q_len, length))
                for v, s in enumerate(shifts):
                    band = pltpu.roll(rows, s, 1, stride=1, stride_axis=0)
                    out_ref[p * len(shifts) + v, h] = band[:, :k_len]

    build(idxa_ref, outa_ref, 0, shifts_a)
    build(idxb_ref, outb_ref, N_HEADS_A, shifts_b)


def _bias(table_t, idx_a, shifts_a, shape_a, idx_b, shifts_b, shape_b):
    vmem = pl.BlockSpec(memory_space=pltpu.VMEM)
    return pl.pallas_call(
        functools.partial(_bias_kernel, shifts_a=shifts_a, shifts_b=shifts_b),
        in_specs=[vmem, vmem, vmem],
        out_specs=[vmem, vmem],
        out_shape=[jax.ShapeDtypeStruct(shape_a, F32), jax.ShapeDtypeStruct(shape_b, F32)],
        compiler_params=pltpu.CompilerParams(vmem_limit_bytes=_VMEM_LIMIT),
        name="bias",
    )(table_t, idx_a, idx_b)


def _inproj_kernel(x_ref, g_ref, w_ref, *refs, dils, stage_dils):
    n_pat = len(dils)
    win_ref, d_refs = refs[0], refs[1:1 + n_pat]
    scr = refs[1 + n_pat]
    stage_scr = dict(zip(stage_dils, refs[2 + n_pat:]))
    tm = x_ref.shape[0]
    n_slab = WIDTH_B // LANES

    x = x_ref[...]
    u = (x * g_ref[...]).astype(BF16)
    rstd = lax.rsqrt(jnp.mean(x * x, axis=-1, keepdims=True) + EPS)

    col = win_ref.shape[-1]
    for t in range(3):
        seg = rstd * jnp.dot(u, w_ref[:, col:col + WIDTH_B], preferred_element_type=F32)
        col += WIDTH_B
        if any(d > 1 for d in dils):
            for c in range(n_slab):
                scr[t, c] = seg[:, c * LANES:(c + 1) * LANES]
        for d, o_ref in zip(dils, d_refs):
            if d == 1:
                o_ref[:, t * WIDTH_B:(t + 1) * WIDTH_B] = seg.astype(BF16)
                continue
            base = max([b for b in stage_dils if b < d and d % b == 0], default=1)
            step = d // base
            for r in range(d):
                for c in range(n_slab):
                    rows = pl.ds(r // base, tm // d, stride=step)
                    src = scr.at[t, c] if base == 1 else stage_scr[base].at[t, r % base, c]
                    val = src[rows, :]
                    if d in stage_scr:
                        stage_scr[d][t, r, c] = val
                    lo = (3 * r + t) * WIDTH_B + c * LANES
                    o_ref[:, lo:lo + LANES] = val.astype(BF16)
    win_ref[...] = (rstd * jnp.dot(u, w_ref[:, :win_ref.shape[-1]],
                                   preferred_element_type=F32)).astype(BF16)


def _inproj(x2d, g, w, tm, dils):
    n, d_model = x2d.shape
    shapes = [(n, WIDTH_A + 2 * KV_WIDTH_A)]
    blocks = [(tm, WIDTH_A + 2 * KV_WIDTH_A)]
    for d in dils:
        assert tm % (16 * d) == 0 and n % d == 0
        shapes.append((n // d, d * 3 * WIDTH_B))
        blocks.append((tm // d, d * 3 * WIDTH_B))
    stage_dils = tuple(b for b in dils if b > 1 and any(d > b and d % b == 0 for d in dils))
    n_slab = WIDTH_B // LANES
    return pl.pallas_call(
        functools.partial(_inproj_kernel, dils=dils, stage_dils=stage_dils),
        grid=(n // tm,),
        in_specs=[
            pl.BlockSpec((tm, d_model), lambda i: (i, 0)),
            pl.BlockSpec((1, d_model), lambda i: (0, 0)),
            pl.BlockSpec(w.shape, lambda i: (0, 0)),
        ],
        out_specs=[pl.BlockSpec(bs, lambda i: (i, 0)) for bs in blocks],
        out_shape=[jax.ShapeDtypeStruct(sh, BF16) for sh in shapes],
        scratch_shapes=[pltpu.VMEM((3, n_slab, tm, LANES), F32)]
                       + [pltpu.VMEM((3, b, n_slab, tm // b, LANES), F32) for b in stage_dils],
        compiler_params=pltpu.CompilerParams(
            dimension_semantics=("parallel",), vmem_limit_bytes=_VMEM_LIMIT),
        name="inproj",
    )(x2d, g, w)


def _nt_dot(a, b):
    return lax.dot_general(a, b, (((1,), (1,)), ((), ())), preferred_element_type=F32)


def _win_kernel(sink_ref, qkv_ref, bias_ref, o_ref, *, seq, blk, unroll):
    q_ref = qkv_ref.at[:, :WIDTH_A]
    k_ref = qkv_ref.at[:, WIDTH_A:WIDTH_A + KV_WIDTH_A]
    v_ref = qkv_ref.at[:, WIDTH_A + KV_WIDTH_A:]
    nblk = seq // blk
    lane = lax.broadcasted_iota(jnp.int32, (blk, LANES), 1)
    low = lane < HEAD_DIM

    klen = 3 * blk

    def block_context(n):
        q0 = pl.multiple_of(n * blk, blk)
        k0 = pl.multiple_of(jnp.clip(q0 - blk, 0, seq - klen), blk)
        variant = jnp.where(n == 0, 0, jnp.where(n == nblk - 1, 2, 1))
        k1 = k_ref[pl.ds(k0, 2 * blk), :]
        k2 = k_ref[pl.ds(k0 + 2 * blk, blk), :]
        v1 = v_ref[pl.ds(k0, 2 * blk), :]
        v2 = v_ref[pl.ds(k0 + 2 * blk, blk), :]
        zero = jnp.zeros_like(k2)
        one_low = low.astype(F32).astype(BF16)
        one_high = (~low).astype(F32).astype(BF16)
        return dict(
            q0=q0, variant=variant, k1=k1,
            v1_aug=jnp.concatenate([v1, jnp.ones_like(v1)], axis=1),
            k2_pack=jnp.concatenate([jnp.where(low, k2, zero), jnp.where(low, zero, k2)], axis=0),
            v2_pack=jnp.concatenate(
                [jnp.concatenate([jnp.where(low, v2, zero), one_low], axis=1),
                 jnp.concatenate([jnp.where(low, zero, v2), one_high], axis=1)], axis=0))

    def scores(ctx, j):
        q_tile = q_ref[pl.ds(ctx["q0"], blk), j * LANES:(j + 1) * LANES]
        s2 = _nt_dot(q_tile, ctx["k2_pack"])
        out = []
        for hf in range(N_KV_A):
            qm = jnp.where(low if hf == 0 else ~low, q_tile, jnp.zeros_like(q_tile))
            out.append(jnp.concatenate(
                [_nt_dot(qm, ctx["k1"]), s2[:, hf * LANES:(hf + 1) * LANES]], axis=1)
                + bias_ref[ctx["variant"], hf * GROUP_A + j])
        return out

    def softmax_pv(ctx, j, logits):
        nums, dens, e2s = [], [], []
        for hf, s in enumerate(logits):
            sink = sink_ref[hf * GROUP_A + j]
            m = jnp.maximum(jnp.max(s, axis=-1, keepdims=True), sink)
            e = jnp.exp2(s - m).astype(BF16)
            o1 = jnp.dot(e[:, :2 * blk], ctx["v1_aug"], preferred_element_type=F32)
            nums.append(o1[:, :LANES])
            dens.append(o1[:, LANES:] + jnp.exp2(sink - m))
            e2s.append(e[:, 2 * blk:])
        o2 = jnp.dot(jnp.concatenate(e2s, axis=1), ctx["v2_pack"], preferred_element_type=F32)
        num = jnp.where(low, nums[0], nums[1]) + o2[:, :LANES]
        den = jnp.where(low, dens[0], dens[1]) + o2[:, LANES:]
        o_ref[pl.ds(ctx["q0"], blk), j * LANES:(j + 1) * LANES] = (num / den).astype(o_ref.dtype)

    def trip(t, c):
        pending = []
        for i in range(unroll):
            ctx = block_context(t * unroll + i)
            for j in range(GROUP_A):
                pending.append((ctx, j, scores(ctx, j)))
                if len(pending) > _WIN_LOOKAHEAD:
                    softmax_pv(*pending.pop(0))
        for task in pending:
            softmax_pv(*task)
        return c

    lax.fori_loop(0, nblk // unroll, trip, 0)


def _windowed(qkv, bias, sink):
    b, s, width = qkv.shape
    blk = WINDOW_A
    nblk = s // blk
    assert s % blk == 0 and nblk >= 3
    kern = functools.partial(_win_kernel, seq=s, blk=blk,
                             unroll=math.gcd(nblk, _BLOCKS_PER_TRIP))
    return pl.pallas_call(
        kern,
        grid=(b,),
        in_specs=[
            pl.BlockSpec(memory_space=pltpu.SMEM),
            pl.BlockSpec((None, s, width), lambda i: (i, 0, 0)),
            pl.BlockSpec(bias.shape, lambda i: (0, 0, 0, 0)),
        ],
        out_specs=pl.BlockSpec((None, s, WIDTH_A), lambda i: (i, 0, 0)),
        out_shape=jax.ShapeDtypeStruct((b, s, WIDTH_A), BF16),
        compiler_params=pltpu.CompilerParams(
            dimension_semantics=("parallel",), vmem_limit_bytes=_VMEM_LIMIT),
        name="win_gqa",
    )(sink, qkv, bias)


def _dil_kernel(qkv_ref, bias_ref, o_ref, st_ref, *, ls, half, n_res, unroll):
    qb = 2 * half
    kw = 4 * half
    nblk = ls // qb
    lane = lax.broadcasted_iota(jnp.int32, (qb, LANES), 1)
    low = lane < HEAD_DIM

    n_pairs = N_HEADS_B // 2

    def scores(n, res_i, j):
        q0 = pl.multiple_of(n * qb, qb)
        k0 = pl.multiple_of(jnp.clip(q0 - half, 0, ls - kw), half)
        variant = jnp.where(n == 0, 0, jnp.where(n == nblk - 1, 2, 1))
        cols = slice(res_i * WIDTH_B + j * LANES, res_i * WIDTH_B + (j + 1) * LANES)
        q_col = 3 * res_i * WIDTH_B + j * LANES
        q_tile = qkv_ref[pl.ds(q0, qb), q_col:q_col + LANES]
        k_win = qkv_ref[pl.ds(k0, kw), q_col + WIDTH_B:q_col + WIDTH_B + LANES]
        logits = []
        for hf in range(2):
            qm = jnp.where(low if hf == 0 else ~low, q_tile, jnp.zeros_like(q_tile))
            logits.append(_nt_dot(qm, k_win) + bias_ref[variant, 2 * j + hf])
        return dict(q0=q0, k0=k0, cols=cols, v_col=q_col + 2 * WIDTH_B, res_i=res_i, j=j,
                    logits=logits)

    def softmax_pv(task, stats):
        j, cols = task["j"], task["cols"]
        v_win = qkv_ref[pl.ds(task["k0"], kw), task["v_col"]:task["v_col"] + LANES]
        v_aug = jnp.concatenate([v_win, jnp.ones_like(v_win)], axis=1)
        st_acc = jnp.zeros((qb, LANES), F32) if j == 0 else stats.pop()
        nums = []
        for hf, s in enumerate(task["logits"]):
            h = 2 * j + hf
            m = jnp.max(s, axis=-1, keepdims=True)
            o = jnp.dot(jnp.exp2(s - m).astype(BF16), v_aug, preferred_element_type=F32)
            nums.append(o[:, :LANES])
            st_acc = jnp.where(lane == h, m,
                               jnp.where(lane == N_HEADS_B + h, o[:, LANES:], st_acc))
        o_ref[pl.ds(task["q0"], qb), cols] = jnp.where(low, nums[0], nums[1]).astype(o_ref.dtype)
        if j == n_pairs - 1:
            r = task["res_i"]
            st_ref[pl.ds(task["q0"], qb), r * LANES:(r + 1) * LANES] = st_acc
        else:
            stats.append(st_acc)

    def trip(t, c):
        pending, stats = [], []
        for i in range(unroll):
            for res_i in range(n_res):
                for j in range(n_pairs):
                    pending.append(scores(t * unroll + i, res_i, j))
                    if len(pending) > _DIL_LOOKAHEAD:
                        softmax_pv(pending.pop(0), stats)
        for task in pending:
            softmax_pv(task, stats)
        return c

    lax.fori_loop(0, nblk // unroll, trip, 0)


def _dilated(qkv, bias, pattern, batch, dil, half):
    rows = qkv.shape[0]
    assert qkv.shape[1] == dil * 3 * WIDTH_B
    w_all = dil * WIDTH_B
    ls = rows // batch
    nblk = ls // (2 * half)
    assert ls % (2 * half) == 0 and nblk >= 2
    n_res = math.gcd(dil, max(1, _BLOCKS_PER_TRIP // nblk))
    unroll = math.gcd(nblk, max(1, _BLOCKS_PER_TRIP // n_res))
    kern = functools.partial(_dil_kernel, ls=ls, half=half, n_res=n_res, unroll=unroll)
    lanes = lambda width: pl.BlockSpec((None, ls, n_res * width), lambda i, r: (i, 0, r))
    n_var = 3
    o, st = pl.pallas_call(
        kern,
        grid=(batch, dil // n_res),
        in_specs=[lanes(3 * WIDTH_B),
                  pl.BlockSpec((n_var,) + bias.shape[1:], lambda i, r: (pattern, 0, 0, 0))],
        out_specs=[lanes(WIDTH_B), lanes(LANES)],
        out_shape=[jax.ShapeDtypeStruct((batch, ls, w_all), BF16),
                   jax.ShapeDtypeStruct((batch, ls, dil * LANES), F32)],
        compiler_params=pltpu.CompilerParams(
            dimension_semantics=("parallel", "parallel"), vmem_limit_bytes=_VMEM_LIMIT),
        name=f"dilated_d{dil}",
    )(qkv.reshape(batch, ls, dil * 3 * WIDTH_B), bias)
    return o.reshape(rows, w_all), st.reshape(rows, dil * LANES)


def _post_kernel(x_ref, p_ref, oa_ref, *refs, dils, ff_chunk, chain_rows):
    n_pat = len(dils)
    ob_refs = refs[:n_pat]
    st_refs = refs[n_pat:2 * n_pat]
    (expand_ref, goa_ref, gob_ref, wo_ref, gpm_ref, gpre_ref, wup_ref, wdn_ref,
     gmlp_ref, wpp_ref, wpg_ref, bpg_ref, gple_ref, out_ref) = refs[2 * n_pat:2 * n_pat + 14]
    scratch = refs[2 * n_pat + 14:]
    tm = x_ref.shape[0]
    assert sum(chain_rows) == tm
    n_slab = WIDTH_B // LANES

    def expand(w):
        head_lane = lax.broadcasted_iota(jnp.int32, w.shape, 1) < N_HEADS_B
        w = jnp.where(head_lane, w, 0.0)
        hi = w.astype(BF16)
        lo = (w - hi.astype(F32)).astype(BF16)
        return jnp.dot(jnp.concatenate([hi, lo], axis=1), expand_ref[...],
                       preferred_element_type=F32)

    def combine(st):
        rows = st["rows"]
        nums, stats = [], []
        si = 0
        for d, o_ref, s_ref in zip(dils, ob_refs, st_refs):
            if d == 1:
                nums.append(o_ref[rows, :].astype(F32))
                stats.append(s_ref[rows, :])
                continue
            o_scr, s_scr = scratch[si], scratch[si + 1]
            si += 2
            assert rows.start % d == 0 and rows.stop % d == 0
            src = slice(rows.start // d, rows.stop // d)
            for r in range(d):
                dst = pl.ds(rows.start + r, (rows.stop - rows.start) // d, stride=d)
                for c in range(n_slab):
                    lo = r * WIDTH_B + c * LANES
                    o_scr[c, dst, :] = o_ref[src, lo:lo + LANES].astype(F32)
                s_scr[dst, :] = s_ref[src, r * LANES:(r + 1) * LANES]
            nums.append(jnp.concatenate([o_scr[c, rows, :] for c in range(n_slab)], axis=1))
            stats.append(s_scr[rows, :])
        mx = functools.reduce(jnp.maximum, stats)
        scales = [jnp.exp2(s - mx) for s in stats]
        dens = [pltpu.roll(s, LANES - N_HEADS_B, 1) for s in stats]
        tot = functools.reduce(lambda a, b: a + b, [sc * dn for sc, dn in zip(scales, dens)])
        ob = None
        for sc, num in zip(scales, nums):
            term = expand(sc / tot) * num
            ob = term if ob is None else ob + term
        st["oa"] = _rms(oa_ref[rows, :].astype(F32), goa_ref[...]).astype(BF16)
        st["ob"] = _rms(ob, gob_ref[...]).astype(BF16)

    def mix(st):
        m = (jnp.dot(st.pop("oa"), wo_ref[:WIDTH_A, :], preferred_element_type=F32)
             + jnp.dot(st.pop("ob"), wo_ref[WIDTH_A:, :], preferred_element_type=F32))
        st["h"] = x_ref[st["rows"], :] + _rms(m, gpm_ref[...])
        st["v"] = _rms(st["h"], gpre_ref[...]).astype(BF16)
        st["ff"] = None

    def mlp_chunk(c):
        def stage(st):
            a = jnp.maximum(
                jnp.dot(st["v"], wup_ref[:, c:c + ff_chunk], preferred_element_type=F32), 0.0)
            t = jnp.dot((a * a).astype(BF16), wdn_ref[c:c + ff_chunk, :],
                        preferred_element_type=F32)
            st["ff"] = t if st["ff"] is None else st["ff"] + t
        return stage

    def ple(st):
        h = st["h"] + _rms(st.pop("ff"), gmlp_ref[...])
        gate = jax.nn.sigmoid(
            jnp.dot(h.astype(BF16), wpg_ref[...], preferred_element_type=F32) + bpg_ref[...])
        emb = jnp.dot(p_ref[st["rows"], :].astype(BF16), wpp_ref[...],
                      preferred_element_type=F32)
        out_ref[st["rows"], :] = h + _rms(gate * emb, gple_ref[...])

    stages = [combine, mix] + [mlp_chunk(c) for c in range(0, wup_ref.shape[1], ff_chunk)] + [ple]
    starts = np.cumsum((0,) + chain_rows)
    states = [{"rows": slice(int(lo), int(hi))} for lo, hi in zip(starts[:-1], starts[1:])]
    for stage in stages:
        for st in states:
            stage(st)


def _post(x2d, p2d, oa, obs, stats, dils, consts, tm, ff_chunk):
    n, d_model = x2d.shape
    row = lambda width: pl.BlockSpec((tm, width), lambda i: (i, 0))
    const = lambda a: pl.BlockSpec(a.shape, lambda i: (0,) * a.ndim,
                                   pipeline_mode=pl.Buffered(1))
    scratch = []
    for d in dils:
        if d > 1:
            scratch += [pltpu.VMEM((WIDTH_B // LANES, tm, LANES), F32),
                        pltpu.VMEM((tm, LANES), F32)]
    return pl.pallas_call(
        functools.partial(_post_kernel, dils=dils, ff_chunk=ff_chunk,
                          chain_rows=tuple(tm * f // sum(_ROW_CHAINS) for f in _ROW_CHAINS)),
        grid=(n // tm,),
        in_specs=[row(d_model), row(p2d.shape[1]), row(WIDTH_A)]
                 + [pl.BlockSpec((tm // d, d * WIDTH_B), lambda i: (i, 0)) for d in dils]
                 + [pl.BlockSpec((tm // d, d * LANES), lambda i: (i, 0)) for d in dils]
                 + [const(a) for a in consts],
        out_specs=row(d_model),
        out_shape=jax.ShapeDtypeStruct((n, d_model), x2d.dtype),
        scratch_shapes=scratch,
        compiler_params=pltpu.CompilerParams(
            dimension_semantics=("parallel",), vmem_limit_bytes=_VMEM_LIMIT),
        name="post",
    )(x2d, p2d, oa, *obs, *stats, *consts)


def kernel(x, p, rel_bias_table, g_pre_mix, w_in, sink_a, g_out_a, g_out_b, w_o, g_post_mix,
           g_pre_mlp, w_up, w_down, g_post_mlp, w_ple_proj, w_ple_gate, b_ple_gate, g_post_ple):
    b, s, d_model = x.shape
    depth = w_in.shape[0]
    n = b * s
    tm = _TOKEN_TILE
    assert n % tm == 0 and s % tm == 0
    dils = tuple(dil for _, dil in DILATED_PATTERNS)

    def pair_heads(a, axis):
        shape = a.shape
        a = a.reshape(shape[:axis] + (N_KV_A, GROUP_A, HEAD_DIM) + shape[axis + 1:])
        return jnp.swapaxes(a, axis, axis + 1).reshape(shape)

    idx_a, shifts_a = _band_layout(WINDOW_A, 3 * WINDOW_A, (0, WINDOW_A, 2 * WINDOW_A),
                                   1, WINDOW_A)
    half = DILATED_PATTERNS[0][0] // (2 * DILATED_PATTERNS[0][1])
    idx_b = []
    for window, dil in DILATED_PATTERNS:
        assert window // (2 * dil) == half
        idx, shifts_b = _band_layout(2 * half, 4 * half, (0, half, 2 * half), dil, half)
        idx_b.append(idx)
    bias_a, bias_b = _bias(
        rel_bias_table.T.astype(F32) * LOG2E,
        jnp.asarray(idx_a)[None, None], shifts_a,
        (len(shifts_a), N_HEADS_A, WINDOW_A, 3 * WINDOW_A),
        jnp.asarray(np.stack(idx_b))[:, None], shifts_b,
        (len(DILATED_PATTERNS) * len(shifts_b), N_HEADS_B, 2 * half, 4 * half))

    expand_np = np.zeros((2 * LANES, WIDTH_B), np.float32)
    for hh in range(N_HEADS_B):
        expand_np[hh, hh * HEAD_DIM:(hh + 1) * HEAD_DIM] = 1.0
        expand_np[LANES + hh, hh * HEAD_DIM:(hh + 1) * HEAD_DIM] = 1.0
    expand = jnp.asarray(expand_np, BF16)

    h2d = x.reshape(n, d_model)
    row = lambda a: a.reshape(1, -1).astype(F32)
    for i in range(depth):
        scale = HEAD_DIM ** -0.5 * LOG2E
        w = w_in[i]
        o0 = WIDTH_A + 2 * KV_WIDTH_A
        w_all = jnp.concatenate([
            pair_heads(w[:, :WIDTH_A], 1) * scale,
            w[:, WIDTH_A:o0],
            w[:, o0:o0 + WIDTH_B] * scale,
            w[:, o0 + WIDTH_B:],
        ], axis=1).astype(BF16)
        outs = _inproj(h2d, row(g_pre_mix[i]), w_all, tm // 2, dils)
        o_a = _windowed(outs[0].reshape(b, s, -1), bias_a, sink_a[i].astype(F32) * LOG2E)
        obs, stats = [], []
        for pi, (window, dil) in enumerate(DILATED_PATTERNS):
            o, st = _dilated(outs[1 + pi], bias_b, pi, b, dil, window // (2 * dil))
            obs.append(o)
            stats.append(st)

        wo = jnp.concatenate([pair_heads(w_o[i][:WIDTH_A], 0), w_o[i][WIDTH_A:]],
                             axis=0).astype(BF16)
        consts = (expand, row(pair_heads(g_out_a[i], 0)), row(g_out_b[i]), wo, row(g_post_mix[i]),
                  row(g_pre_mlp[i]), w_up[i].astype(BF16), w_down[i].astype(BF16),
                  row(g_post_mlp[i]), w_ple_proj[i].astype(BF16), w_ple_gate[i].astype(BF16),
                  row(b_ple_gate[i]), row(g_post_ple[i]))
        h2d = _post(h2d, p[i].reshape(n, -1), o_a.reshape(n, WIDTH_A), obs, stats, dils,
                    consts, tm, _FF_CHUNK)
    return h2d.reshape(b, s, d_model)
```

```python
import functools
import math

import jax
import jax.numpy as jnp
import numpy as np
from jax import lax
from jax.experimental import pallas as pl
from jax.experimental.pallas import tpu as pltpu

HEAD_DIM = 64
N_HEADS_A = 8
N_KV_A = 2
GROUP_A = N_HEADS_A // N_KV_A
WINDOW_A = 128
N_HEADS_B = 8
DILATED_PATTERNS = ((128, 1), (512, 4), (2048, 16))
WIDTH_A = N_HEADS_A * HEAD_DIM
WIDTH_B = N_HEADS_B * HEAD_DIM
KV_WIDTH_A = N_KV_A * HEAD_DIM
NUM_BUCKETS = 32
MAX_DISTANCE = 1024
EPS = 1e-6
NEG = -1e30
LOG2E = 1.4426950408889634

LANES = 128
F32 = jnp.float32
BF16 = jnp.bfloat16

_VMEM_LIMIT = 56 * 1024 * 1024
_TOKEN_TILE = 512
_FF_CHUNK = 2048
_BLOCKS_PER_TRIP = 16
_WIN_LOOKAHEAD = 2
_DIL_LOOKAHEAD = 4
_ROW_CHAINS = (1, 1)


def _rms(x, g):
    ms = jnp.mean(x * x, axis=-1, keepdims=True)
    return (x * lax.rsqrt(ms + EPS)) * g


def _t5_bucket_np(rel):
    half = NUM_BUCKETS // 2
    max_exact = half // 2
    sign = np.where(rel > 0, half, 0)
    n = np.abs(rel)
    nf = np.maximum(n, 1).astype(np.float32)
    large = max_exact + (np.log(nf / np.float32(max_exact))
                         / np.float32(math.log(MAX_DISTANCE / max_exact))
                         * np.float32(half - max_exact)).astype(np.int32)
    large = np.minimum(large, half - 1)
    return (sign + np.where(n < max_exact, n, large)).astype(np.int32)


def _band_layout(q_len, k_len, offs, dil, half_window):
    center = q_len - 1 + max(offs)
    length = -(-(center + k_len) // LANES) * LANES
    rel = np.arange(length) - center
    idx = np.where(np.abs(rel) <= half_window, _t5_bucket_np(rel * dil), -1).astype(np.int32)
    return idx, tuple((off - center) % length for off in offs)


def _bias_kernel(tab_ref, idxa_ref, idxb_ref, outa_ref, outb_ref, *, shifts_a, shifts_b):
    def build(idx_ref, out_ref, head0, shifts):
        n_pat, _, length = idx_ref.shape
        _, n_heads, q_len, k_len = out_ref.shape
        for p in range(n_pat):
            idx = idx_ref[p]
            vec = jnp.full((n_heads, length), NEG, F32)
            for b in range(NUM_BUCKETS):
                vec = jnp.where(idx == b, tab_ref[head0:head0 + n_heads, b:b + 1], vec)
            for h in range(n_heads):
                rows = jnp.broadcast_to(vec[h:h + 1, :], (q_len, length))
                for v, s in enumerate(shifts):
                    band = pltpu.roll(rows, s, 1, stride=1, stride_axis=0)
                    out_ref[p * len(shifts) + v, h] = band[:, :k_len]

    build(idxa_ref, outa_ref, 0, shifts_a)
    build(idxb_ref, outb_ref, N_HEADS_A, shifts_b)


def _bias(table_t, idx_a, shifts_a, shape_a, idx_b, shifts_b, shape_b):
    vmem = pl.BlockSpec(memory_space=pltpu.VMEM)
    return pl.pallas_call(
        functools.partial(_bias_kernel, shifts_a=shifts_a, shifts_b=shifts_b),
        in_specs=[vmem, vmem, vmem],
        out_specs=[vmem, vmem],
        out_shape=[jax.ShapeDtypeStruct(shape_a, F32), jax.ShapeDtypeStruct(shape_b, F32)],
        compiler_params=pltpu.CompilerParams(vmem_limit_bytes=_VMEM_LIMIT),
        name="bias",
    )(table_t, idx_a, idx_b)


def _inproj_kernel(x_ref, g_ref, w_ref, *refs, dils, stage_dils):
    n_pat = len(dils)
    win_ref, d_refs = refs[0], refs[1:1 + n_pat]
    scr = refs[1 + n_pat]
    stage_scr = dict(zip(stage_dils, refs[2 + n_pat:]))
    tm = x_ref.shape[0]
    n_slab = WIDTH_B // LANES

    x = x_ref[...]
    u = (x * g_ref[...]).astype(BF16)
    rstd = lax.rsqrt(jnp.mean(x * x, axis=-1, keepdims=True) + EPS)

    col = win_ref.shape[-1]
    for t in range(3):
        seg = rstd * jnp.dot(u, w_ref[:, col:col + WIDTH_B], preferred_element_type=F32)
        col += WIDTH_B
        if any(d > 1 for d in dils):
            for c in range(n_slab):
                scr[t, c] = seg[:, c * LANES:(c + 1) * LANES]
        for d, o_ref in zip(dils, d_refs):
            if d == 1:
                o_ref[:, t * WIDTH_B:(t + 1) * WIDTH_B] = seg.astype(BF16)
                continue
            base = max([b for b in stage_dils if b < d and d % b == 0], default=1)
            step = d // base
            for r in range(d):
                for c in range(n_slab):
                    rows = pl.ds(r // base, tm // d, stride=step)
                    src = scr.at[t, c] if base == 1 else stage_scr[base].at[t, r % base, c]
                    val = src[rows, :]
                    if d in stage_scr:
                        stage_scr[d][t, r, c] = val
                    lo = (3 * r + t) * WIDTH_B + c * LANES
                    o_ref[:, lo:lo + LANES] = val.astype(BF16)
    win_ref[...] = (rstd * jnp.dot(u, w_ref[:, :win_ref.shape[-1]],
                                   preferred_element_type=F32)).astype(BF16)


def _inproj(x2d, g, w, tm, dils):
    n, d_model = x2d.shape
    shapes = [(n, WIDTH_A + 2 * KV_WIDTH_A)]
    blocks = [(tm, WIDTH_A + 2 * KV_WIDTH_A)]
    for d in dils:
        assert tm % (16 * d) == 0 and n % d == 0
        shapes.append((n // d, d * 3 * WIDTH_B))
        blocks.append((tm // d, d * 3 * WIDTH_B))
    stage_dils = tuple(b for b in dils if b > 1 and any(d > b and d % b == 0 for d in dils))
    n_slab = WIDTH_B // LANES
    return pl.pallas_call(
        functools.partial(_inproj_kernel, dils=dils, stage_dils=stage_dils),
        grid=(n // tm,),
        in_specs=[
            pl.BlockSpec((tm, d_model), lambda i: (i, 0)),
            pl.BlockSpec((1, d_model), lambda i: (0, 0)),
            pl.BlockSpec(w.shape, lambda i: (0, 0)),
        ],
        out_specs=[pl.BlockSpec(bs, lambda i: (i, 0)) for bs in blocks],
        out_shape=[jax.ShapeDtypeStruct(sh, BF16) for sh in shapes],
        scratch_shapes=[pltpu.VMEM((3, n_slab, tm, LANES), F32)]
                       + [pltpu.VMEM((3, b, n_slab, tm // b, LANES), F32) for b in stage_dils],
        compiler_params=pltpu.CompilerParams(
            dimension_semantics=("parallel",), vmem_limit_bytes=_VMEM_LIMIT),
        name="inproj",
    )(x2d, g, w)


def _nt_dot(a, b):
    return lax.dot_general(a, b, (((1,), (1,)), ((), ())), preferred_element_type=F32)


def _win_kernel(sink_ref, qkv_ref, bias_ref, o_ref, *, seq, blk, unroll):
    q_ref = qkv_ref.at[:, :WIDTH_A]
    k_ref = qkv_ref.at[:, WIDTH_A:WIDTH_A + KV_WIDTH_A]
    v_ref = qkv_ref.at[:, WIDTH_A + KV_WIDTH_A:]
    nblk = seq // blk
    lane = lax.broadcasted_iota(jnp.int32, (blk, LANES), 1)
    low = lane < HEAD_DIM

    klen = 3 * blk

    def block_context(n):
        q0 = pl.multiple_of(n * blk, blk)
        k0 = pl.multiple_of(jnp.clip(q0 - blk, 0, seq - klen), blk)
        variant = jnp.where(n == 0, 0, jnp.where(n == nblk - 1, 2, 1))
        k1 = k_ref[pl.ds(k0, 2 * blk), :]
        k2 = k_ref[pl.ds(k0 + 2 * blk, blk), :]
        v1 = v_ref[pl.ds(k0, 2 * blk), :]
        v2 = v_ref[pl.ds(k0 + 2 * blk, blk), :]
        zero = jnp.zeros_like(k2)
        one_low = low.astype(F32).astype(BF16)
        one_high = (~low).astype(F32).astype(BF16)
        return dict(
            q0=q0, variant=variant, k1=k1,
            v1_aug=jnp.concatenate([v1, jnp.ones_like(v1)], axis=1),
            k2_pack=jnp.concatenate([jnp.where(low, k2, zero), jnp.where(low, zero, k2)], axis=0),
            v2_pack=jnp.concatenate(
                [jnp.concatenate([jnp.where(low, v2, zero), one_low], axis=1),
                 jnp.concatenate([jnp.where(low, zero, v2), one_high], axis=1)], axis=0))

    def scores(ctx, j):
        q_tile = q_ref[pl.ds(ctx["q0"], blk), j * LANES:(j + 1) * LANES]
        s2 = _nt_dot(q_tile, ctx["k2_pack"])
        out = []
        for hf in range(N_KV_A):
            qm = jnp.where(low if hf == 0 else ~low, q_tile, jnp.zeros_like(q_tile))
            out.append(jnp.concatenate(
                [_nt_dot(qm, ctx["k1"]), s2[:, hf * LANES:(hf + 1) * LANES]], axis=1)
                + bias_ref[ctx["variant"], hf * GROUP_A + j])
        return out

    def softmax_pv(ctx, j, logits):
        nums, dens, e2s = [], [], []
        for hf, s in enumerate(logits):
            sink = sink_ref[hf * GROUP_A + j]
            m = jnp.maximum(jnp.max(s, axis=-1, keepdims=True), sink)
            e = jnp.exp2(s - m).astype(BF16)
            o1 = jnp.dot(e[:, :2 * blk], ctx["v1_aug"], preferred_element_type=F32)
            nums.append(o1[:, :LANES])
            dens.append(o1[:, LANES:] + jnp.exp2(sink - m))
            e2s.append(e[:, 2 * blk:])
        o2 = jnp.dot(jnp.concatenate(e2s, axis=1), ctx["v2_pack"], preferred_element_type=F32)
        num = jnp.where(low, nums[0], nums[1]) + o2[:, :LANES]
        den = jnp.where(low, dens[0], dens[1]) + o2[:, LANES:]
        o_ref[pl.ds(ctx["q0"], blk), j * LANES:(j + 1) * LANES] = (num / den).astype(o_ref.dtype)

    def trip(t, c):
        pending = []
        for i in range(unroll):
            ctx = block_context(t * unroll + i)
            for j in range(GROUP_A):
                pending.append((ctx, j, scores(ctx, j)))
                if len(pending) > _WIN_LOOKAHEAD:
                    softmax_pv(*pending.pop(0))
        for task in pending:
            softmax_pv(*task)
        return c

    lax.fori_loop(0, nblk // unroll, trip, 0)


def _windowed(qkv, bias, sink):
    b, s, width = qkv.shape
    blk = WINDOW_A
    nblk = s // blk
    assert s % blk == 0 and nblk >= 3
    kern = functools.partial(_win_kernel, seq=s, blk=blk,
                             unroll=math.gcd(nblk, _BLOCKS_PER_TRIP))
    return pl.pallas_call(
        kern,
        grid=(b,),
        in_specs=[
            pl.BlockSpec(memory_space=pltpu.SMEM),
            pl.BlockSpec((None, s, width), lambda i: (i, 0, 0)),
            pl.BlockSpec(bias.shape, lambda i: (0, 0, 0, 0), pipeline_mode=pl.Buffered(1)),
        ],
        out_specs=pl.BlockSpec((None, s, WIDTH_A), lambda i: (i, 0, 0)),
        out_shape=jax.ShapeDtypeStruct((b, s, WIDTH_A), BF16),
        compiler_params=pltpu.CompilerParams(
            dimension_semantics=("parallel",), vmem_limit_bytes=_VMEM_LIMIT),
        name="win_gqa",
    )(sink, qkv, bias)


def _dil_kernel(qkv_ref, bias_ref, o_ref, st_ref, *, ls, half, n_res, unroll):
    qb = 2 * half
    kw = 4 * half
    nblk = ls // qb
    lane = lax.broadcasted_iota(jnp.int32, (qb, LANES), 1)
    low = lane < HEAD_DIM

    n_pairs = N_HEADS_B // 2

    def scores(n, res_i, j):
        q0 = pl.multiple_of(n * qb, qb)
        k0 = pl.multiple_of(jnp.clip(q0 - half, 0, ls - kw), half)
        variant = jnp.where(n == 0, 0, jnp.where(n == nblk - 1, 2, 1))
        cols = slice(res_i * WIDTH_B + j * LANES, res_i * WIDTH_B + (j + 1) * LANES)
        q_col = 3 * res_i * WIDTH_B + j * LANES
        q_tile = qkv_ref[pl.ds(q0, qb), q_col:q_col + LANES]
        k_win = qkv_ref[pl.ds(k0, kw), q_col + WIDTH_B:q_col + WIDTH_B + LANES]
        logits = []
        for hf in range(2):
            qm = jnp.where(low if hf == 0 else ~low, q_tile, jnp.zeros_like(q_tile))
            logits.append(_nt_dot(qm, k_win) + bias_ref[variant, 2 * j + hf])
        return dict(q0=q0, k0=k0, cols=cols, v_col=q_col + 2 * WIDTH_B, res_i=res_i, j=j,
                    logits=logits)

    def softmax_pv(task, stats):
        j, cols = task["j"], task["cols"]
        v_win = qkv_ref[pl.ds(task["k0"], kw), task["v_col"]:task["v_col"] + LANES]
        v_aug = jnp.concatenate([v_win, jnp.ones_like(v_win)], axis=1)
        st_acc = jnp.zeros((qb, LANES), F32) if j == 0 else stats.pop()
        nums = []
        for hf, s in enumerate(task["logits"]):
            h = 2 * j + hf
            m = jnp.max(s, axis=-1, keepdims=True)
            o = jnp.dot(jnp.exp2(s - m).astype(BF16), v_aug, preferred_element_type=F32)
            nums.append(o[:, :LANES])
            st_acc = jnp.where(lane == h, m,
                               jnp.where(lane == N_HEADS_B + h, o[:, LANES:], st_acc))
        o_ref[pl.ds(task["q0"], qb), cols] = jnp.where(low, nums[0], nums[1]).astype(o_ref.dtype)
        if j == n_pairs - 1:
            r = task["res_i"]
            st_ref[pl.ds(task["q0"], qb), r * LANES:(r + 1) * LANES] = st_acc
        else:
            stats.append(st_acc)

    def trip(t, c):
        pending, stats = [], []
        for i in range(unroll):
            for res_i in range(n_res):
                for j in range(n_pairs):
                    pending.append(scores(t * unroll + i, res_i, j))
                    if len(pending) > _DIL_LOOKAHEAD:
                        softmax_pv(pending.pop(0), stats)
        for task in pending:
            softmax_pv(task, stats)
        return c

    lax.fori_loop(0, nblk // unroll, trip, 0)


def _dilated(qkv, bias, pattern, batch, dil, half):
    rows = qkv.shape[0]
    assert qkv.shape[1] == dil * 3 * WIDTH_B
    w_all = dil * WIDTH_B
    ls = rows // batch
    nblk = ls // (2 * half)
    assert ls % (2 * half) == 0 and nblk >= 2
    n_res = math.gcd(dil, max(1, _BLOCKS_PER_TRIP // nblk))
    unroll = math.gcd(nblk, max(1, _BLOCKS_PER_TRIP // n_res))
    kern = functools.partial(_dil_kernel, ls=ls, half=half, n_res=n_res, unroll=unroll)
    lanes = lambda width: pl.BlockSpec((None, ls, n_res * width), lambda i, r: (i, 0, r))
    n_var = 3
    o, st = pl.pallas_call(
        kern,
        grid=(batch, dil // n_res),
        in_specs=[lanes(3 * WIDTH_B),
                  pl.BlockSpec((n_var,) + bias.shape[1:], lambda i, r: (pattern, 0, 0, 0),
                               pipeline_mode=pl.Buffered(1))],
        out_specs=[lanes(WIDTH_B), lanes(LANES)],
        out_shape=[jax.ShapeDtypeStruct((batch, ls, w_all), BF16),
                   jax.ShapeDtypeStruct((batch, ls, dil * LANES), F32)],
        compiler_params=pltpu.CompilerParams(
            dimension_semantics=("parallel", "parallel"), vmem_limit_bytes=_VMEM_LIMIT),
        name=f"dilated_d{dil}",
    )(qkv.reshape(batch, ls, dil * 3 * WIDTH_B), bias)
    return o.reshape(rows, w_all), st.reshape(rows, dil * LANES)


def _post_kernel(x_ref, p_ref, oa_ref, *refs, dils, ff_chunk, chain_rows):
    n_pat = len(dils)
    ob_refs = refs[:n_pat]
    st_refs = refs[n_pat:2 * n_pat]
    (expand_ref, goa_ref, gob_ref, wo_ref, gpm_ref, gpre_ref, wup_ref, wdn_ref,
     gmlp_ref, wpp_ref, wpg_ref, bpg_ref, gple_ref, out_ref) = refs[2 * n_pat:2 * n_pat + 14]
    scratch = refs[2 * n_pat + 14:]
    tm = x_ref.shape[0]
    assert sum(chain_rows) == tm
    n_slab = WIDTH_B // LANES

    def expand(w):
        head_lane = lax.broadcasted_iota(jnp.int32, w.shape, 1) < N_HEADS_B
        w = jnp.where(head_lane, w, 0.0)
        hi = w.astype(BF16)
        lo = (w - hi.astype(F32)).astype(BF16)
        return jnp.dot(jnp.concatenate([hi, lo], axis=1), expand_ref[...],
                       preferred_element_type=F32)

    def combine(st):
        rows = st["rows"]
        nums, stats = [], []
        si = 0
        for d, o_ref, s_ref in zip(dils, ob_refs, st_refs):
            if d == 1:
                nums.append(o_ref[rows, :].astype(F32))
                stats.append(s_ref[rows, :])
                continue
            o_scr, s_scr = scratch[si], scratch[si + 1]
            si += 2
            assert rows.start % d == 0 and rows.stop % d == 0
            src = slice(rows.start // d, rows.stop // d)
            for r in range(d):
                dst = pl.ds(rows.start + r, (rows.stop - rows.start) // d, stride=d)
                for c in range(n_slab):
                    lo = r * WIDTH_B + c * LANES
                    o_scr[c, dst, :] = o_ref[src, lo:lo + LANES].astype(F32)
                s_scr[dst, :] = s_ref[src, r * LANES:(r + 1) * LANES]
            nums.append(jnp.concatenate([o_scr[c, rows, :] for c in range(n_slab)], axis=1))
            stats.append(s_scr[rows, :])
        mx = functools.reduce(jnp.maximum, stats)
        scales = [jnp.exp2(s - mx) for s in stats]
        dens = [pltpu.roll(s, LANES - N_HEADS_B, 1) for s in stats]
        tot = functools.reduce(lambda a, b: a + b, [sc * dn for sc, dn in zip(scales, dens)])
        ob = None
        for sc, num in zip(scales, nums):
            term = expand(sc / tot) * num
            ob = term if ob is None else ob + term
        st["oa"] = _rms(oa_ref[rows, :].astype(F32), goa_ref[...]).astype(BF16)
        st["ob"] = _rms(ob, gob_ref[...]).astype(BF16)

    def mix(st):
        m = (jnp.dot(st.pop("oa"), wo_ref[:WIDTH_A, :], preferred_element_type=F32)
             + jnp.dot(st.pop("ob"), wo_ref[WIDTH_A:, :], preferred_element_type=F32))
        st["h"] = x_ref[st["rows"], :] + _rms(m, gpm_ref[...])
        st["v"] = _rms(st["h"], gpre_ref[...]).astype(BF16)
        st["ff"] = None

    def mlp_chunk(c):
        def stage(st):
            a = jnp.maximum(
                jnp.dot(st["v"], wup_ref[:, c:c + ff_chunk], preferred_element_type=F32), 0.0)
            t = jnp.dot((a * a).astype(BF16), wdn_ref[c:c + ff_chunk, :],
                        preferred_element_type=F32)
            st["ff"] = t if st["ff"] is None else st["ff"] + t
        return stage

    def ple(st):
        h = st["h"] + _rms(st.pop("ff"), gmlp_ref[...])
        gate = jax.nn.sigmoid(
            jnp.dot(h.astype(BF16), wpg_ref[...], preferred_element_type=F32) + bpg_ref[...])
        emb = jnp.dot(p_ref[st["rows"], :].astype(BF16), wpp_ref[...],
                      preferred_element_type=F32)
        out_ref[st["rows"], :] = h + _rms(gate * emb, gple_ref[...])

    stages = [combine, mix] + [mlp_chunk(c) for c in range(0, wup_ref.shape[1], ff_chunk)] + [ple]
    starts = np.cumsum((0,) + chain_rows)
    states = [{"rows": slice(int(lo), int(hi))} for lo, hi in zip(starts[:-1], starts[1:])]
    for stage in stages:
        for st in states:
            stage(st)


def _post(x2d, p2d, oa, obs, stats, dils, consts, tm, ff_chunk):
    n, d_model = x2d.shape
    row = lambda width: pl.BlockSpec((tm, width), lambda i: (i, 0))
    const = lambda a: pl.BlockSpec(a.shape, lambda i: (0,) * a.ndim,
                                   pipeline_mode=pl.Buffered(1))
    scratch = []
    for d in dils:
        if d > 1:
            scratch += [pltpu.VMEM((WIDTH_B // LANES, tm, LANES), F32),
                        pltpu.VMEM((tm, LANES), F32)]
    return pl.pallas_call(
        functools.partial(_post_kernel, dils=dils, ff_chunk=ff_chunk,
                          chain_rows=tuple(tm * f // sum(_ROW_CHAINS) for f in _ROW_CHAINS)),
        grid=(n // tm,),
        in_specs=[row(d_model), row(p2d.shape[1]), row(WIDTH_A)]
                 + [pl.BlockSpec((tm // d, d * WIDTH_B), lambda i: (i, 0)) for d in dils]
                 + [pl.BlockSpec((tm // d, d * LANES), lambda i: (i, 0)) for d in dils]
                 + [const(a) for a in consts],
        out_specs=row(d_model),
        out_shape=jax.ShapeDtypeStruct((n, d_model), x2d.dtype),
        scratch_shapes=scratch,
        compiler_params=pltpu.CompilerParams(
            dimension_semantics=("parallel",), vmem_limit_bytes=_VMEM_LIMIT),
        name="post",
    )(x2d, p2d, oa, *obs, *stats, *consts)


def kernel(x, p, rel_bias_table, g_pre_mix, w_in, sink_a, g_out_a, g_out_b, w_o, g_post_mix,
           g_pre_mlp, w_up, w_down, g_post_mlp, w_ple_proj, w_ple_gate, b_ple_gate, g_post_ple):
    b, s, d_model = x.shape
    depth = w_in.shape[0]
    n = b * s
    tm = _TOKEN_TILE
    assert n % tm == 0 and s % tm == 0
    dils = tuple(dil for _, dil in DILATED_PATTERNS)

    def pair_heads(a, axis):
        shape = a.shape
        a = a.reshape(shape[:axis] + (N_KV_A, GROUP_A, HEAD_DIM) + shape[axis + 1:])
        return jnp.swapaxes(a, axis, axis + 1).reshape(shape)

    idx_a, shifts_a = _band_layout(WINDOW_A, 3 * WINDOW_A, (0, WINDOW_A, 2 * WINDOW_A),
                                   1, WINDOW_A)
    half = DILATED_PATTERNS[0][0] // (2 * DILATED_PATTERNS[0][1])
    idx_b = []
    for window, dil in DILATED_PATTERNS:
        assert window // (2 * dil) == half
        idx, shifts_b = _band_layout(2 * half, 4 * half, (0, half, 2 * half), dil, half)
        idx_b.append(idx)
    bias_a, bias_b = _bias(
        rel_bias_table.T.astype(F32) * LOG2E,
        jnp.asarray(idx_a)[None, None], shifts_a,
        (len(shifts_a), N_HEADS_A, WINDOW_A, 3 * WINDOW_A),
        jnp.asarray(np.stack(idx_b))[:, None], shifts_b,
        (len(DILATED_PATTERNS) * len(shifts_b), N_HEADS_B, 2 * half, 4 * half))

    expand_np = np.zeros((2 * LANES, WIDTH_B), np.float32)
    for hh in range(N_HEADS_B):
        expand_np[hh, hh * HEAD_DIM:(hh + 1) * HEAD_DIM] = 1.0
        expand_np[LANES + hh, hh * HEAD_DIM:(hh + 1) * HEAD_DIM] = 1.0
    expand = jnp.asarray(expand_np, BF16)

    h2d = x.reshape(n, d_model)
    row = lambda a: a.reshape(1, -1).astype(F32)
    for i in range(depth):
        scale = HEAD_DIM ** -0.5 * LOG2E
        w = w_in[i]
        o0 = WIDTH_A + 2 * KV_WIDTH_A
        w_all = jnp.concatenate([
            pair_heads(w[:, :WIDTH_A], 1) * scale,
            w[:, WIDTH_A:o0],
            w[:, o0:o0 + WIDTH_B] * scale,
            w[:, o0 + WIDTH_B:],
        ], axis=1).astype(BF16)
        outs = _inproj(h2d, row(g_pre_mix[i]), w_all, tm, dils)
        o_a = _windowed(outs[0].reshape(b, s, -1), bias_a, sink_a[i].astype(F32) * LOG2E)
        obs, stats = [], []
        for pi, (window, dil) in enumerate(DILATED_PATTERNS):
            o, st = _dilated(outs[1 + pi], bias_b, pi, b, dil, window // (2 * dil))
            obs.append(o)
            stats.append(st)

        wo = jnp.concatenate([pair_heads(w_o[i][:WIDTH_A], 0), w_o[i][WIDTH_A:]],
                             axis=0).astype(BF16)
        consts = (expand, row(pair_heads(g_out_a[i], 0)), row(g_out_b[i]), wo, row(g_post_mix[i]),
                  row(g_pre_mlp[i]), w_up[i].astype(BF16), w_down[i].astype(BF16),
                  row(g_post_mlp[i]), w_ple_proj[i].astype(BF16), w_ple_gate[i].astype(BF16),
                  row(b_ple_gate[i]), row(g_post_ple[i]))
        h2d = _post(h2d, p[i].reshape(n, -1), o_a.reshape(n, WIDTH_A), obs, stats, dils,
                    consts, tm, _FF_CHUNK)
    return h2d.reshape(b, s, d_model)
```
